```python
import jax, jax.numpy as jnp
from jax import lax
import numpy as np

D_MODEL = 1024
BATCH = 8
SEQ = 2048
DEPTH = 1
DEC_BATCH = 128
DEC_SEQ = 4
PAST_LEN = 16384
PAGE_SIZE = 128

PLE_DIM = 256
D_FF = 2816
EPS = 1e-6
SSD_HEADS = 16
SSD_HEADDIM = 64
SSD_DINNER = SSD_HEADS * SSD_HEADDIM
SSD_STATE = 128
SSD_GROUPS = 2
CONV_W = 4
SSD_CHUNK = 128
CONV_DIM = SSD_DINNER + 2 * SSD_GROUPS * SSD_STATE
HG_HEADS = 8
HG_DK = 128
HG_DV = 128
HG_WIDTH = HG_HEADS * HG_DV
HG_CHUNK = 16
D_MIX = SSD_DINNER + HG_WIDTH
D_IN_PROJ = SSD_DINNER + CONV_DIM + SSD_HEADS + 2 * HG_HEADS * HG_DK + 2 * HG_WIDTH

kernel_name = "hymba_ssd_hgrn2_macaron_step"


def _rmsnorm(x, w):
    xf = x.astype(jnp.float32)
    r = lax.rsqrt(jnp.mean(xf * xf, axis=-1, keepdims=True) + EPS)
    return (xf * r * w.astype(jnp.float32)).astype(x.dtype)


def _swiglu(h, w_up, w_down):
    g, u = jnp.split(h @ w_up, 2, axis=-1)
    return (jax.nn.silu(g) * u) @ w_down


def _pad_time(t, npad):
    if npad == 0:
        return t
    return jnp.pad(t, [(0, 0), (0, npad)] + [(0, 0)] * (t.ndim - 2))


def _ssd_chunked(x, dt, a, bm, cm, h0):
    b, L = x.shape[:2]
    G, R, P, N = SSD_GROUPS, SSD_HEADS // SSD_GROUPS, SSD_HEADDIM, SSD_STATE
    c = min(SSD_CHUNK, L)
    npad = (-L) % c
    x, dt, bm, cm = [_pad_time(t, npad) for t in (x, dt, bm, cm)]
    nc = (L + npad) // c
    x = x.reshape(b, nc, c, G, R, P)
    dt = dt.reshape(b, nc, c, G, R)
    bm = bm.reshape(b, nc, c, G, N)
    cm = cm.reshape(b, nc, c, G, N)
    acs = jnp.cumsum(dt * a.reshape(G, R), axis=2)
    xdt = x * dt[..., None]
    mask = jnp.tril(jnp.ones((c, c), dtype=bool))[None, None, :, :, None, None]
    diff = acs[:, :, :, None] - acs[:, :, None, :]
    decay = jnp.exp(jnp.where(mask, diff, -jnp.inf))
    cb = jnp.einsum('bctgn,bcsgn->bctsg', cm, bm)
    y_intra = jnp.einsum('bctsg,bctsgr,bcsgrp->bctgrp', cb, decay, xdt)
    dec_end = jnp.exp(acs[:, :, -1:] - acs)
    chunk_decay = jnp.exp(acs[:, :, -1])
    dec_in = jnp.exp(acs)

    def step(h, inp):
        b_c, c_c, x_c, de_c, di_c, dc_c = inp
        y_int = jnp.einsum('btgn,bgrpn,btgr->btgrp', c_c, h, di_c)
        h_new = h * dc_c[..., None, None] + jnp.einsum('bsgn,bsgr,bsgrp->bgrpn', b_c, de_c, x_c)
        return h_new, y_int

    mv = lambda t: jnp.moveaxis(t, 1, 0)
    h_fin, y_inter = lax.scan(step, h0.reshape(b, G, R, P, N),
                              (mv(bm), mv(cm), mv(xdt), mv(dec_end), mv(dec_in), mv(chunk_decay)))
    y = y_intra + jnp.moveaxis(y_inter, 0, 1)
    y = y.reshape(b, nc * c, SSD_HEADS, P)[:, :L]
    return y, h_fin.reshape(b, SSD_HEADS, P, N)


def _gla_chunked(q, k, v, logf, s0):
    b, L, H, K = q.shape
    V = v.shape[-1]
    c = min(HG_CHUNK, L)
    npad = (-L) % c
    q, k, v, logf = [_pad_time(t, npad) for t in (q, k, v, logf)]
    nc = (L + npad) // c
    q = q.reshape(b, nc, c, H, K)
    k = k.reshape(b, nc, c, H, K)
    v = v.reshape(b, nc, c, H, V)
    bcum = jnp.cumsum(logf.reshape(b, nc, c, H, K), axis=2)
    qe = q * jnp.exp(bcum)
    ke = k * jnp.exp(-bcum)
    mask = jnp.tril(jnp.ones((c, c), dtype=bool))
    att = jnp.where(mask, jnp.einsum('bcthk,bcshk->bchts', qe, ke), 0.0)
    o_intra = jnp.einsum('bchts,bcshv->bcthv', att, v)
    kd = k * jnp.exp(bcum[:, :, -1:] - bcum)
    chunk_decay = jnp.exp(bcum[:, :, -1])

    def step(S, inp):
        qe_c, kd_c, v_c, dc_c = inp
        o_int = jnp.einsum('bthk,bhkv->bthv', qe_c, S)
        S_new = S * dc_c[..., None] + jnp.einsum('bshk,bshv->bhkv', kd_c, v_c)
        return S_new, o_int

    mv = lambda t: jnp.moveaxis(t, 1, 0)
    s_fin, o_inter = lax.scan(step, s0, (mv(qe), mv(kd), mv(v), mv(chunk_decay)))
    o = o_intra + jnp.moveaxis(o_inter, 0, 1)
    return o.reshape(b, nc * c, H, V)[:, :L], s_fin


def _mixer(h, conv0, ssm0, hg0, lb, w_in, conv_w, conv_b, dt_bias, a_log, d_skip, ssd_norm, hg_norm, w_out):
    b, L, _ = h.shape
    f32 = jnp.float32
    proj = (h @ w_in).astype(f32)
    sizes = [SSD_DINNER, CONV_DIM, SSD_HEADS, HG_HEADS * HG_DK, HG_HEADS * HG_DK, HG_WIDTH, HG_WIDTH]
    points = [int(v) for v in np.cumsum(sizes)[:-1]]
    z, xbc, dt_raw, q, fr, iv, og = jnp.split(proj, points, axis=-1)
    xbc_full = jnp.concatenate([conv0.astype(f32), xbc], axis=1)
    cw = conv_w.astype(f32)
    conv = conv_b.astype(f32) + sum(cw[j] * xbc_full[:, j:j + L] for j in range(CONV_W))
    conv = jax.nn.silu(conv)
    new_conv = xbc_full[:, -(CONV_W - 1):]
    xs, bm, cm = jnp.split(conv, [SSD_DINNER, SSD_DINNER + SSD_GROUPS * SSD_STATE], axis=-1)
    xs = xs.reshape(b, L, SSD_HEADS, SSD_HEADDIM)
    bm = bm.reshape(b, L, SSD_GROUPS, SSD_STATE)
    cm = cm.reshape(b, L, SSD_GROUPS, SSD_STATE)
    dt = jax.nn.softplus(dt_raw + dt_bias.astype(f32))
    a = -jnp.exp(a_log.astype(f32))
    y, new_ssm = _ssd_chunked(xs, dt, a, bm, cm, ssm0.astype(f32))
    y = y + d_skip.astype(f32)[:, None] * xs
    yg = (y.reshape(b, L, SSD_DINNER) * jax.nn.silu(z)).reshape(b, L, SSD_GROUPS, SSD_DINNER // SSD_GROUPS)
    yg = yg * lax.rsqrt(jnp.mean(yg * yg, axis=-1, keepdims=True) + EPS)
    y_ssd = yg.reshape(b, L, SSD_DINNER) * ssd_norm.astype(f32)
    f = lb + (1.0 - lb) * jax.nn.sigmoid(fr)
    logf = jnp.log(f).reshape(b, L, HG_HEADS, HG_DK)
    kk = (1.0 - f).reshape(b, L, HG_HEADS, HG_DK)
    qq = jax.nn.silu(q).reshape(b, L, HG_HEADS, HG_DK)
    vv = iv.reshape(b, L, HG_HEADS, HG_DV)
    o, new_hg = _gla_chunked(qq, kk, vv, logf, hg0.astype(f32))
    o = o * lax.rsqrt(jnp.mean(o * o, axis=-1, keepdims=True) + EPS)
    o = o.reshape(b, L, HG_WIDTH) * hg_norm.astype(f32) * jax.nn.silu(og)
    mixed = jnp.concatenate([y_ssd, o], axis=-1).astype(h.dtype) @ w_out
    return mixed, new_conv, new_ssm, new_hg


def _layer(x, p, conv0, ssm0, hg0, lb, lw):
    (n_f1, w_f1u, w_f1d, n_mix, w_in, conv_w, conv_b, dt_bias, a_log, d_skip, ssd_norm, hg_norm, w_out,
     n_f2, w_f2u, w_f2d, n_ple, w_ple_gate, w_ple_proj, ple_post) = lw
    x = x + 0.5 * _swiglu(_rmsnorm(x, n_f1), w_f1u, w_f1d)
    m, new_conv, new_ssm, new_hg = _mixer(_rmsnorm(x, n_mix), conv0, ssm0, hg0, lb, w_in, conv_w, conv_b,
                                          dt_bias, a_log, d_skip, ssd_norm, hg_norm, w_out)
    x = x + m.astype(x.dtype)
    x = x + 0.5 * _swiglu(_rmsnorm(x, n_f2), w_f2u, w_f2d)
    gate = jax.nn.sigmoid((_rmsnorm(x, n_ple) @ w_ple_gate).astype(jnp.float32))
    e = _rmsnorm(p.astype(x.dtype) @ w_ple_proj, ple_post).astype(jnp.float32)
    x = x + (gate * e).astype(x.dtype)
    return x, new_conv, new_ssm, new_hg


def setup_inputs(seed: int = 0) -> dict:
    key = jax.random.key(seed)
    ks = iter(jax.random.split(key, 40))
    f32 = jnp.float32
    nrm = lambda shape, s: jax.random.normal(next(ks), shape, f32) * s
    gain = lambda shape: 1.0 + nrm(shape, 0.02)
    dt0 = jnp.exp(jax.random.uniform(next(ks), (DEPTH, SSD_HEADS), f32, np.log(1e-3), np.log(1e-1)))
    return {
        "x_prompt": nrm((BATCH, SEQ, D_MODEL), 1.0),
        "x_sample": nrm((DEC_BATCH, DEC_SEQ, D_MODEL), 1.0),
        "state_conv": nrm((DEPTH, DEC_BATCH, CONV_W - 1, CONV_DIM), 1.0),
        "state_ssm": nrm((DEPTH, DEC_BATCH, SSD_HEADS, SSD_HEADDIM, SSD_STATE), 0.1),
        "state_hgrn": nrm((DEPTH, DEC_BATCH, HG_HEADS, HG_DK, HG_DV), 0.3),
        "p_prompt": nrm((DEPTH, BATCH, SEQ, PLE_DIM), 1.0),
        "p_sample": nrm((DEPTH, DEC_BATCH, DEC_SEQ, PLE_DIM), 1.0),
        "norm_ffn1": gain((DEPTH, D_MODEL)),
        "w_ffn1_up": nrm((DEPTH, D_MODEL, 2 * D_FF), D_MODEL ** -0.5),
        "w_ffn1_down": nrm((DEPTH, D_FF, D_MODEL), D_FF ** -0.5),
        "norm_mix": gain((DEPTH, D_MODEL)),
        "w_in": nrm((DEPTH, D_MODEL, D_IN_PROJ), D_MODEL ** -0.5),
        "conv_w": nrm((DEPTH, CONV_W, CONV_DIM), CONV_W ** -0.5),
        "conv_b": nrm((DEPTH, CONV_DIM), 0.02),
        "dt_bias": dt0 + jnp.log(-jnp.expm1(-dt0)),
        "a_log": jnp.log(jax.random.uniform(next(ks), (DEPTH, SSD_HEADS), f32, 1.0, 16.0)),
        "d_skip": gain((DEPTH, SSD_HEADS)),
        "ssd_norm": gain((DEPTH, SSD_DINNER)),
        "hg_lb_logits": nrm((DEPTH + 1, HG_HEADS * HG_DK), 0.1),
        "hg_norm": gain((DEPTH, HG_WIDTH)),
        "w_out": nrm((DEPTH, D_MIX, D_MODEL), D_MIX ** -0.5),
        "norm_ffn2": gain((DEPTH, D_MODEL)),
        "w_ffn2_up": nrm((DEPTH, D_MODEL, 2 * D_FF), D_MODEL ** -0.5),
        "w_ffn2_down": nrm((DEPTH, D_FF, D_MODEL), D_FF ** -0.5),
        "norm_ple": gain((DEPTH, D_MODEL)),
        "w_ple_gate": nrm((DEPTH, D_MODEL, D_MODEL), D_MODEL ** -0.5),
        "w_ple_proj": nrm((DEPTH, PLE_DIM, D_MODEL), PLE_DIM ** -0.5),
        "ple_post_norm": gain((DEPTH, D_MODEL)),
        "norm_final": gain((D_MODEL,)),
    }


def reference(x_prompt, x_sample, state_conv, state_ssm, state_hgrn, p_prompt, p_sample,
              norm_ffn1, w_ffn1_up, w_ffn1_down, norm_mix, w_in, conv_w, conv_b, dt_bias, a_log, d_skip,
              ssd_norm, hg_lb_logits, hg_norm, w_out, norm_ffn2, w_ffn2_up, w_ffn2_down,
              norm_ple, w_ple_gate, w_ple_proj, ple_post_norm, norm_final):
    f32 = jnp.float32
    bp = x_prompt.shape[0]
    lb_all = jnp.cumsum(jax.nn.softmax(hg_lb_logits.astype(f32), axis=0), axis=0)
    xp, xs = x_prompt, x_sample
    conv_p, ssm_p, hg_p, conv_s, ssm_s, hg_s = [], [], [], [], [], []
    for i in range(DEPTH):
        lw = (norm_ffn1[i], w_ffn1_up[i], w_ffn1_down[i], norm_mix[i], w_in[i], conv_w[i], conv_b[i],
              dt_bias[i], a_log[i], d_skip[i], ssd_norm[i], hg_norm[i], w_out[i],
              norm_ffn2[i], w_ffn2_up[i], w_ffn2_down[i], norm_ple[i], w_ple_gate[i], w_ple_proj[i],
              ple_post_norm[i])
        lb = lb_all[i]
        xp, c_p, s_p, h_p = _layer(xp, p_prompt[i],
                                   jnp.zeros((bp, CONV_W - 1, CONV_DIM), f32),
                                   jnp.zeros((bp, SSD_HEADS, SSD_HEADDIM, SSD_STATE), f32),
                                   jnp.zeros((bp, HG_HEADS, HG_DK, HG_DV), f32), lb, lw)
        xs, c_s, s_s, h_s = _layer(xs, p_sample[i], state_conv[i], state_ssm[i], state_hgrn[i], lb, lw)
        conv_p.append(c_p.astype(state_conv.dtype))
        ssm_p.append(s_p.astype(state_ssm.dtype))
        hg_p.append(h_p.astype(state_hgrn.dtype))
        conv_s.append(c_s.astype(state_conv.dtype))
        ssm_s.append(s_s.astype(state_ssm.dtype))
        hg_s.append(h_s.astype(state_hgrn.dtype))
    y_prompt = _rmsnorm(xp, norm_final)
    y_sample = _rmsnorm(xs, norm_final)
    return (y_prompt, y_sample, jnp.stack(conv_p), jnp.stack(ssm_p), jnp.stack(hg_p),
            jnp.stack(conv_s), jnp.stack(ssm_s), jnp.stack(hg_s))
```

```python
import functools
from typing import NamedTuple

import jax
import jax.numpy as jnp
from jax import lax
from jax.experimental import pallas as pl
from jax.experimental.pallas import tpu as pltpu

F32 = jnp.float32
BF16 = jnp.bfloat16

D_MODEL = 1024
D_FF = 2816
PLE_DIM = 256
EPS = 1e-6
SSD_HEADS = 16
SSD_HEADDIM = 64
SSD_DINNER = SSD_HEADS * SSD_HEADDIM
SSD_STATE = 128
SSD_GROUPS = 2
SSD_HEADS_PER_GROUP = SSD_HEADS // SSD_GROUPS
SSD_GROUP_WIDTH = SSD_DINNER // SSD_GROUPS
CONV_W = 4
CONV_DIM = SSD_DINNER + 2 * SSD_GROUPS * SSD_STATE
HG_HEADS = 8
HG_DK = 128
HG_DV = 128
HG_WIDTH = HG_HEADS * HG_DV
HG_BLOCK = 16
D_MIX = SSD_DINNER + HG_WIDTH

LANES = 128
SUBLANES = 8
VMEM_LIMIT_BYTES = 56 * 1024 * 1024

CHUNK = 128
SAMPLE_ROWS = 8
FFN_TILE = 512
FF_CHUNK = 256
N_FF_CHUNKS = D_FF // FF_CHUNK

OFF_Z = 0
OFF_XBC = OFF_Z + SSD_DINNER
OFF_Q = OFF_XBC + CONV_DIM
OFF_FR = OFF_Q + HG_WIDTH
OFF_IV = OFF_FR + HG_WIDTH
OFF_OG = OFF_IV + HG_WIDTH
OFF_DTE = OFF_OG + HG_WIDTH
OFF_DTC = OFF_DTE + SSD_DINNER
PROJ_W = OFF_DTC + LANES


class MixParams(NamedTuple):
  conv_w: jax.Array
  conv_b: jax.Array
  dtb_e: jax.Array
  dtb_c: jax.Array
  alog_e: jax.Array
  alog_c: jax.Array
  dskip_e: jax.Array
  ssd_norm: jax.Array
  lb_logits: jax.Array
  hg_norm: jax.Array


def _dot(a, b):
  return jnp.dot(a, b, preferred_element_type=F32)


def _dot_nt(a, b):
  return lax.dot_general(a, b, (((1,), (1,)), ((), ())), preferred_element_type=F32)


def _dot_tn(a, b):
  return lax.dot_general(a, b, (((0,), (0,)), ((), ())), preferred_element_type=F32)


def _sigmoid(x):
  return 1.0 / (1.0 + jnp.exp(-x))


def _silu(x):
  return x * _sigmoid(x)


def _softplus(x):
  return jnp.maximum(x, 0.0) + jnp.log1p(jnp.exp(-jnp.abs(x)))


def _rmsnorm(x, w):
  ms = jnp.mean(x * x, axis=-1, keepdims=True)
  return x * lax.rsqrt(ms + EPS) * w


def _cumsum_rows(x):
  c, n = x.shape
  row = lax.broadcasted_iota(jnp.int32, (c, n), 0)
  s = 1
  while s < c:
    if s < SUBLANES:
      shifted = jnp.where(row >= s, pltpu.roll(x, s, axis=0), 0.0)
    else:
      shifted = jnp.concatenate([jnp.zeros((s, n), x.dtype), x[: c - s]], axis=0)
    x = x + shifted
    s *= 2
  return x


def _col_matrix(row_vec, n_lanes):
  n = row_vec.shape[1]
  return jnp.transpose(jnp.broadcast_to(row_vec, (n_lanes, n)))


def _mix_chunk(proj, *, c, valid, p, xb_ref, ssm_ref, hg_ref):
  cs = max(c, LANES)

  def padr(x):
    if x.shape[0] == cs:
      return x
    return jnp.concatenate([x, jnp.zeros((cs - x.shape[0], x.shape[1]), x.dtype)], axis=0)

  row_valid = None
  if valid < c:
    row_valid = lax.broadcasted_iota(jnp.int32, (c, 1), 0) < valid

  def mask_rows(x):
    return x if row_valid is None else jnp.where(row_valid, x, 0.0)

  z = proj[:, OFF_Z:OFF_Z + SSD_DINNER]
  xbc = proj[:, OFF_XBC:OFF_XBC + CONV_DIM]
  q = proj[:, OFF_Q:OFF_Q + HG_WIDTH]
  fr = proj[:, OFF_FR:OFF_FR + HG_WIDTH]
  iv = proj[:, OFF_IV:OFF_IV + HG_WIDTH]
  og = proj[:, OFF_OG:OFF_OG + HG_WIDTH]
  dt_raw_e = proj[:, OFF_DTE:OFF_DTE + SSD_DINNER]
  dt_raw_c = proj[:, OFF_DTC:OFF_DTC + LANES]

  xb_ref[SUBLANES:SUBLANES + c, :] = xbc
  conv = p.conv_b + p.conv_w[CONV_W - 1:CONV_W] * xbc
  for j in range(CONV_W - 1):
    conv = conv + p.conv_w[j:j + 1] * xb_ref[SUBLANES - (CONV_W - 1) + j:SUBLANES - (CONV_W - 1) + j + c, :]
  conv = _silu(conv)
  xs = conv[:, :SSD_DINNER]
  bm = conv[:, SSD_DINNER:SSD_DINNER + SSD_GROUPS * SSD_STATE]
  cm = conv[:, SSD_DINNER + SSD_GROUPS * SSD_STATE:]

  head_lane = lax.broadcasted_iota(jnp.int32, (1, LANES), 1) < SSD_HEADS
  dt_e = mask_rows(_softplus(dt_raw_e + p.dtb_e))
  dt_c = mask_rows(jnp.where(head_lane, _softplus(dt_raw_c + p.dtb_c), 0.0))
  acs_e = _cumsum_rows(dt_e * -jnp.exp(p.alog_e))
  acs_c = _cumsum_rows(dt_c * -jnp.exp(p.alog_c))
  last_e = acs_e[c - 1:c, :]
  dec_in = jnp.exp(acs_e)
  dec_end = jnp.exp(last_e - acs_e)
  chunk_decay = jnp.exp(last_e)
  xdt = xs * dt_e
  xd_end = xdt * dec_end
  acs_t = jnp.transpose(padr(acs_c))

  t_idx = lax.broadcasted_iota(jnp.int32, (c, cs), 0)
  s_idx = lax.broadcasted_iota(jnp.int32, (c, cs), 1)
  causal = s_idx <= t_idx
  lane = lax.broadcasted_iota(jnp.int32, (1, LANES), 1)
  first_head = lane < SSD_HEADDIM

  y_intra, y_inter = [], []
  for g in range(SSD_GROUPS):
    bg = bm[:, g * SSD_STATE:(g + 1) * SSD_STATE]
    cg = cm[:, g * SSD_STATE:(g + 1) * SSD_STATE].astype(BF16)
    bg_p = padr(bg).astype(BF16)
    cb = _dot_nt(cg, bg_p)
    h0 = ssm_ref[g * SSD_HEADS_PER_GROUP:(g + 1) * SSD_HEADS_PER_GROUP].reshape(SSD_GROUP_WIDTH, SSD_STATE)
    y_inter.append(_dot_nt(cg, h0.astype(BF16)))
    for j in range(SSD_HEADS_PER_GROUP // 2):
      h_first = g * SSD_HEADS_PER_GROUP + 2 * j
      scores = []
      for h in (h_first, h_first + 1):
        diff = acs_c[:, h:h + 1] - acs_t[h:h + 1, :]
        scores.append((cb * jnp.where(causal, jnp.exp(diff), 0.0)).astype(BF16))
      pair = padr(xdt[:, h_first * SSD_HEADDIM:(h_first + 2) * SSD_HEADDIM])
      rhs = jnp.concatenate([jnp.where(first_head, pair, 0.0), jnp.where(first_head, 0.0, pair)], axis=0)
      y_intra.append(_dot(jnp.concatenate(scores, axis=1), rhs.astype(BF16)))
    xd_g = padr(xd_end[:, g * SSD_GROUP_WIDTH:(g + 1) * SSD_GROUP_WIDTH]).astype(BF16)
    decay_cols = _col_matrix(chunk_decay[:, g * SSD_GROUP_WIDTH:(g + 1) * SSD_GROUP_WIDTH], SSD_STATE)
    h1 = h0 * decay_cols + _dot_tn(xd_g, bg_p)
    ssm_ref[g * SSD_HEADS_PER_GROUP:(g + 1) * SSD_HEADS_PER_GROUP] = h1.reshape(
        SSD_HEADS_PER_GROUP, SSD_HEADDIM, SSD_STATE)
  y = jnp.concatenate(y_intra, axis=1) + jnp.concatenate(y_inter, axis=1) * dec_in + p.dskip_e * xs
  yg = y * _silu(z)
  y_norm = []
  for g in range(SSD_GROUPS):
    blk = yg[:, g * SSD_GROUP_WIDTH:(g + 1) * SSD_GROUP_WIDTH]
    y_norm.append(blk * lax.rsqrt(jnp.mean(blk * blk, axis=-1, keepdims=True) + EPS))
  y_ssd = jnp.concatenate(y_norm, axis=1) * p.ssd_norm

  lg = p.lb_logits
  lg_max = jnp.max(lg, axis=0, keepdims=True)
  lg_exp = jnp.exp(lg - lg_max)
  lb = lg_exp[0:1] / jnp.sum(lg_exp, axis=0, keepdims=True)
  f = lb + (1.0 - lb) * _sigmoid(fr)
  logf = mask_rows(jnp.log(f))
  kk = mask_rows(1.0 - f)
  qq = _silu(q)
  b = _cumsum_rows(logf)
  b_last = b[c - 1:c, :]
  q_inter = (qq * jnp.exp(b)).astype(BF16)
  k_end = kk * jnp.exp(b_last - b)
  state_decay = jnp.exp(b_last)

  levels = []
  m = c // 2
  while m >= HG_BLOCK:
    q_rows, k_rows = [], []
    for i in range(c // (2 * m)):
      lo = i * 2 * m
      b_ref_row = b[lo + m - 1:lo + m, :]
      k_rows += [kk[lo:lo + m] * jnp.exp(b_ref_row - b[lo:lo + m]), jnp.zeros((m, HG_WIDTH), F32)]
      q_rows += [jnp.zeros((m, HG_WIDTH), F32), qq[lo + m:lo + 2 * m] * jnp.exp(b[lo + m:lo + 2 * m] - b_ref_row)]
    shift = m.bit_length() - 1
    mask = ((t_idx >> (shift + 1)) == (s_idx >> (shift + 1))) & (((t_idx >> shift) & 1) == 1) & (
        ((s_idx >> shift) & 1) == 0)
    levels.append((jnp.concatenate(q_rows, axis=0), jnp.concatenate(k_rows, axis=0), mask))
    m //= 2
  blk = min(HG_BLOCK, c)
  q_rows, k_rows = [], []
  for i in range(c // blk):
    lo = i * blk
    b_loc = b[lo:lo + blk] if i == 0 else b[lo:lo + blk] - b[lo - 1:lo, :]
    q_rows.append(qq[lo:lo + blk] * jnp.exp(b_loc))
    k_rows.append(kk[lo:lo + blk] * jnp.exp(-b_loc))
  shift = blk.bit_length() - 1
  levels.append((jnp.concatenate(q_rows, axis=0), jnp.concatenate(k_rows, axis=0),
                 ((t_idx >> shift) == (s_idx >> shift)) & causal))
  levels = [(ql.astype(BF16), padr(kl).astype(BF16), mk) for ql, kl, mk in levels]

  o_parts = []
  for h in range(HG_HEADS):
    sl = slice(h * HG_DK, (h + 1) * HG_DK)
    att = jnp.zeros((c, cs), F32)
    for ql, kl, mk in levels:
      att = jnp.where(mk, _dot_nt(ql[:, sl], kl[:, sl]), att)
    v_h = padr(iv[:, sl]).astype(BF16)
    s0 = hg_ref[h]
    o_h = _dot(att.astype(BF16), v_h) + _dot(q_inter[:, sl], s0.astype(BF16))
    s1 = s0 * _col_matrix(state_decay[:, sl], HG_DV) + _dot_tn(padr(k_end[:, sl]).astype(BF16), v_h)
    hg_ref[h] = s1
    o_parts.append(o_h * lax.rsqrt(jnp.mean(o_h * o_h, axis=-1, keepdims=True) + EPS))
  o = jnp.concatenate(o_parts, axis=1) * p.hg_norm * _silu(og)
  return jnp.concatenate([y_ssd, o], axis=1)


def _load_mix_params(refs):
  return MixParams(*[r[...] for r in refs])


def _swiglu(h, wup_ref, wdn_ref):
  acc = jnp.zeros((h.shape[0], D_MODEL), F32)
  for ci in range(N_FF_CHUNKS):
    g = _dot(h, wup_ref[ci])
    u = _dot(h, wup_ref[N_FF_CHUNKS + ci])
    acc = acc + _dot((_silu(g) * u).astype(BF16), wdn_ref[ci])
  return acc


def _ffn1_kernel(x_ref, nw_ref, wup_ref, wdn_ref, o_ref):
  x = x_ref[...]
  h = _rmsnorm(x, nw_ref[...]).astype(BF16)
  o_ref[...] = x + 0.5 * _swiglu(h, wup_ref, wdn_ref)


def _mix_prompt_kernel(x_ref, nmix_ref, wall_ref, *rest):
  prm_refs = rest[:len(MixParams._fields)]
  mixed_ref, conv_ref, ssm_ref, hg_ref, xb_ref = rest[len(MixParams._fields):]
  t = pl.program_id(1)

  @pl.when(t == 0)
  def _():
    xb_ref[0:SUBLANES, :] = jnp.zeros((SUBLANES, CONV_DIM), F32)
    ssm_ref[...] = jnp.zeros(ssm_ref.shape, F32)
    hg_ref[...] = jnp.zeros(hg_ref.shape, F32)

  h = _rmsnorm(x_ref[0], nmix_ref[...]).astype(BF16)
  proj = _dot(h, wall_ref[...])
  mixed = _mix_chunk(proj, c=CHUNK, valid=CHUNK, p=_load_mix_params(prm_refs), xb_ref=xb_ref,
                     ssm_ref=ssm_ref.at[0], hg_ref=hg_ref.at[0])
  mixed_ref[0] = mixed.astype(BF16)
  tail = xb_ref[CHUNK:CHUNK + SUBLANES, :]
  xb_ref[0:SUBLANES, :] = tail

  @pl.when(t == pl.num_programs(1) - 1)
  def _():
    conv_ref[0] = xb_ref[SUBLANES + CHUNK - (CONV_W - 1):SUBLANES + CHUNK, :]


def _proj_sample_kernel(x_ref, nmix_ref, wall_ref, o_ref):
  h = _rmsnorm(x_ref[...], nmix_ref[...]).astype(BF16)
  o_ref[...] = _dot(h, wall_ref[...])


def _rec_sample_kernel(proj_ref, conv_in_ref, ssm_in_ref, hg_in_ref, *rest, valid):
  prm_refs = rest[:len(MixParams._fields)]
  mixed_ref, conv_ref, ssm_ref, hg_ref, xb_ref = rest[len(MixParams._fields):]
  xb_ref[0:SUBLANES, :] = jnp.zeros((SUBLANES, CONV_DIM), F32)
  xb_ref[SUBLANES - (CONV_W - 1):SUBLANES, :] = conv_in_ref[0]
  ssm_ref[...] = ssm_in_ref[...]
  hg_ref[...] = hg_in_ref[...]
  mixed = _mix_chunk(proj_ref[0], c=SAMPLE_ROWS, valid=valid, p=_load_mix_params(prm_refs), xb_ref=xb_ref,
                     ssm_ref=ssm_ref.at[0], hg_ref=hg_ref.at[0])
  mixed_ref[0] = mixed.astype(BF16)
  conv_ref[0] = xb_ref[SUBLANES + valid - (CONV_W - 1):SUBLANES + valid, :]


def _tail_kernel(x1_ref, mixed_ref, p_ref, wout_ref, nf2_ref, wup_ref, wdn_ref, nple_ref, wgate_ref, wproj_ref,
                 ppost_ref, nfin_ref, o_ref):
  x2 = x1_ref[...] + _dot(mixed_ref[...], wout_ref[...])
  h = _rmsnorm(x2, nf2_ref[...]).astype(BF16)
  x3 = x2 + 0.5 * _swiglu(h, wup_ref, wdn_ref)
  gate = _sigmoid(_dot(_rmsnorm(x3, nple_ref[...]).astype(BF16), wgate_ref[...]))
  e = _rmsnorm(_dot(p_ref[...].astype(BF16), wproj_ref[...]), ppost_ref[...])
  x4 = x3 + gate * e
  o_ref[...] = _rmsnorm(x4, nfin_ref[...])


def _resident(shape):
  nd = len(shape)
  return pl.BlockSpec(shape, lambda *_: (0,) * nd, pipeline_mode=pl.Buffered(1))


def _params(semantics):
  return pltpu.CompilerParams(dimension_semantics=semantics, vmem_limit_bytes=VMEM_LIMIT_BYTES)


def _ffn1(x, nw, wup, wdn):
  n = x.shape[0]
  return pl.pallas_call(
      _ffn1_kernel,
      grid=(n // FFN_TILE,),
      in_specs=[pl.BlockSpec((FFN_TILE, D_MODEL), lambda i: (i, 0)), _resident(nw.shape), _resident(wup.shape),
                _resident(wdn.shape)],
      out_specs=pl.BlockSpec((FFN_TILE, D_MODEL), lambda i: (i, 0)),
      out_shape=jax.ShapeDtypeStruct((n, D_MODEL), F32),
      compiler_params=_params(("arbitrary",)),
      name="ffn1",
  )(x, nw, wup, wdn)


def _mix_prompt(x1, nmix, wall, prm):
  bsz, seq, _ = x1.shape
  nt = seq // CHUNK
  out_shape = (
      jax.ShapeDtypeStruct((bsz, seq, D_MIX), BF16),
      jax.ShapeDtypeStruct((bsz, CONV_W - 1, CONV_DIM), F32),
      jax.ShapeDtypeStruct((bsz, SSD_HEADS, SSD_HEADDIM, SSD_STATE), F32),
      jax.ShapeDtypeStruct((bsz, HG_HEADS, HG_DK, HG_DV), F32),
  )
  out_specs = (
      pl.BlockSpec((1, CHUNK, D_MIX), lambda b, t: (b, t, 0)),
      pl.BlockSpec((1, CONV_W - 1, CONV_DIM), lambda b, t: (b, 0, 0)),
      pl.BlockSpec((1, SSD_HEADS, SSD_HEADDIM, SSD_STATE), lambda b, t: (b, 0, 0, 0)),
      pl.BlockSpec((1, HG_HEADS, HG_DK, HG_DV), lambda b, t: (b, 0, 0, 0)),
  )
  in_specs = [pl.BlockSpec((1, CHUNK, D_MODEL), lambda b, t: (b, t, 0)), _resident(nmix.shape),
              _resident(wall.shape)] + [_resident(a.shape) for a in prm]
  return pl.pallas_call(
      _mix_prompt_kernel,
      grid=(bsz, nt),
      in_specs=in_specs,
      out_specs=out_specs,
      out_shape=out_shape,
      scratch_shapes=[pltpu.VMEM((SUBLANES + CHUNK, CONV_DIM), F32)],
      compiler_params=_params(("arbitrary", "arbitrary")),
      name="mix_prompt",
  )(x1, nmix, wall, *prm)


def _proj_sample(x1p, nmix, wall):
  n = x1p.shape[0]
  tile = 256
  return pl.pallas_call(
      _proj_sample_kernel,
      grid=(n // tile,),
      in_specs=[pl.BlockSpec((tile, D_MODEL), lambda i: (i, 0)), _resident(nmix.shape), _resident(wall.shape)],
      out_specs=pl.BlockSpec((tile, PROJ_W), lambda i: (i, 0)),
      out_shape=jax.ShapeDtypeStruct((n, PROJ_W), F32),
      compiler_params=_params(("arbitrary",)),
      name="proj_sample",
  )(x1p, nmix, wall)


def _rec_sample(proj, conv0, ssm0, hg0, prm, valid):
  bsz = proj.shape[0]
  out_shape = (
      jax.ShapeDtypeStruct((bsz, SAMPLE_ROWS, D_MIX), BF16),
      jax.ShapeDtypeStruct((bsz, CONV_W - 1, CONV_DIM), F32),
      jax.ShapeDtypeStruct((bsz, SSD_HEADS, SSD_HEADDIM, SSD_STATE), F32),
      jax.ShapeDtypeStruct((bsz, HG_HEADS, HG_DK, HG_DV), F32),
  )
  state_specs = [
      pl.BlockSpec((1, CONV_W - 1, CONV_DIM), lambda b: (b, 0, 0)),
      pl.BlockSpec((1, SSD_HEADS, SSD_HEADDIM, SSD_STATE), lambda b: (b, 0, 0, 0)),
      pl.BlockSpec((1, HG_HEADS, HG_DK, HG_DV), lambda b: (b, 0, 0, 0)),
  ]
  in_specs = [pl.BlockSpec((1, SAMPLE_ROWS, PROJ_W), lambda b: (b, 0, 0))] + state_specs + [
      _resident(a.shape) for a in prm]
  out_specs = tuple([pl.BlockSpec((1, SAMPLE_ROWS, D_MIX), lambda b: (b, 0, 0))] + state_specs)
  return pl.pallas_call(
      functools.partial(_rec_sample_kernel, valid=valid),
      grid=(bsz,),
      in_specs=in_specs,
      out_specs=out_specs,
      out_shape=out_shape,
      scratch_shapes=[pltpu.VMEM((2 * SUBLANES, CONV_DIM), F32)],
      compiler_params=_params(("arbitrary",)),
      name="rec_sample",
  )(proj, conv0, ssm0, hg0, *prm)


def _tail(x1, mixed, p, wout, nf2, wup, wdn, nple, wgate, wproj, ppost, nfin):
  n = x1.shape[0]
  tile = FFN_TILE
  row = lambda w: pl.BlockSpec((tile, w), lambda i: (i, 0))
  weights = (wout, nf2, wup, wdn, nple, wgate, wproj, ppost, nfin)
  return pl.pallas_call(
      _tail_kernel,
      grid=(n // tile,),
      in_specs=[row(D_MODEL), row(D_MIX), row(PLE_DIM)] + [_resident(w.shape) for w in weights],
      out_specs=row(D_MODEL),
      out_shape=jax.ShapeDtypeStruct((n, D_MODEL), F32),
      compiler_params=_params(("arbitrary",)),
      name="tail",
  )(x1, mixed, p, *weights)


def _ffn_weights(w_up, w_down):
  wup = w_up.reshape(D_MODEL, 2 * N_FF_CHUNKS, FF_CHUNK).transpose(1, 0, 2).astype(BF16)
  wdn = w_down.reshape(N_FF_CHUNKS, FF_CHUNK, D_MODEL).astype(BF16)
  return wup, wdn


def kernel(x_prompt, x_sample, state_conv, state_ssm, state_hgrn, p_prompt, p_sample, norm_ffn1, w_ffn1_up, w_ffn1_down, norm_mix, w_in, conv_w, conv_b, dt_bias, a_log, d_skip, ssd_norm, hg_lb_logits, hg_norm, w_out, norm_ffn2, w_ffn2_up, w_ffn2_down, norm_ple, w_ple_gate, w_ple_proj, ple_post_norm, norm_final):
  bp, seq, _ = x_prompt.shape
  bs, dec_seq, _ = x_sample.shape
  row = lambda v: v.reshape(1, -1).astype(F32)
  per_head = lambda v: jnp.repeat(v.astype(F32), SSD_HEADDIM).reshape(1, SSD_DINNER)
  head_lanes = lambda v: jnp.pad(v.astype(F32), (0, LANES - SSD_HEADS)).reshape(1, LANES)

  w1u, w1d = _ffn_weights(w_ffn1_up[0], w_ffn1_down[0])
  w2u, w2d = _ffn_weights(w_ffn2_up[0], w_ffn2_down[0])
  wi = w_in[0]
  o_dt = SSD_DINNER + CONV_DIM
  o_q = o_dt + SSD_HEADS
  w_dt = wi[:, o_dt:o_q]
  wall = jnp.concatenate([
      wi[:, :o_dt],
      wi[:, o_q:],
      jnp.repeat(w_dt, SSD_HEADDIM, axis=1),
      jnp.pad(w_dt, ((0, 0), (0, LANES - SSD_HEADS))),
  ], axis=1).astype(BF16)
  prm = MixParams(
      conv_w=conv_w[0].astype(F32), conv_b=row(conv_b[0]), dtb_e=per_head(dt_bias[0]), dtb_c=head_lanes(dt_bias[0]),
      alog_e=per_head(a_log[0]), alog_c=head_lanes(a_log[0]), dskip_e=per_head(d_skip[0]), ssd_norm=row(ssd_norm[0]),
      lb_logits=hg_lb_logits.astype(F32), hg_norm=row(hg_norm[0]))
  nmix = row(norm_mix[0])
  tail_w = (w_out[0].astype(BF16), row(norm_ffn2[0]), w2u, w2d, row(norm_ple[0]), w_ple_gate[0].astype(BF16),
            w_ple_proj[0].astype(BF16), row(ple_post_norm[0]), row(norm_final))

  xp = x_prompt.reshape(bp * seq, D_MODEL)
  x1p = _ffn1(xp, row(norm_ffn1[0]), w1u, w1d)
  mixed_p, conv_p, ssm_p, hg_p = _mix_prompt(x1p.reshape(bp, seq, D_MODEL), nmix, wall, prm)
  y_prompt = _tail(x1p, mixed_p.reshape(bp * seq, D_MIX), p_prompt[0].reshape(bp * seq, PLE_DIM), *tail_w)

  xs = x_sample.reshape(bs * dec_seq, D_MODEL)
  x1s = _ffn1(xs, row(norm_ffn1[0]), w1u, w1d)
  x1s_pad = jnp.pad(x1s.reshape(bs, dec_seq, D_MODEL), ((0, 0), (0, SAMPLE_ROWS - dec_seq), (0, 0)))
  proj_s = _proj_sample(x1s_pad.reshape(bs * SAMPLE_ROWS, D_MODEL), nmix, wall)
  mixed_s, conv_s, ssm_s, hg_s = _rec_sample(proj_s.reshape(bs, SAMPLE_ROWS, PROJ_W), state_conv[0], state_ssm[0],
                                             state_hgrn[0], prm, dec_seq)
  mixed_s = mixed_s[:, :dec_seq].reshape(bs * dec_seq, D_MIX)
  y_sample = _tail(x1s, mixed_s, p_sample[0].reshape(bs * dec_seq, PLE_DIM), *tail_w)

  return (y_prompt.reshape(bp, seq, D_MODEL), y_sample.reshape(bs, dec_seq, D_MODEL), conv_p[None], ssm_p[None],
          hg_p[None], conv_s[None], ssm_s[None], hg_s[None])
```

```python
import functools
from typing import NamedTuple

import jax
import jax.numpy as jnp
from jax import lax
from jax.experimental import pallas as pl
from jax.experimental.pallas import tpu as pltpu

F32 = jnp.float32
BF16 = jnp.bfloat16

D_MODEL = 1024
D_FF = 2816
PLE_DIM = 256
EPS = 1e-6
SSD_HEADS = 16
SSD_HEADDIM = 64
SSD_DINNER = SSD_HEADS * SSD_HEADDIM
SSD_STATE = 128
SSD_GROUPS = 2
SSD_HEADS_PER_GROUP = SSD_HEADS // SSD_GROUPS
SSD_GROUP_WIDTH = SSD_DINNER // SSD_GROUPS
CONV_W = 4
CONV_DIM = SSD_DINNER + 2 * SSD_GROUPS * SSD_STATE
HG_HEADS = 8
HG_DK = 128
HG_DV = 128
HG_WIDTH = HG_HEADS * HG_DV
HG_BLOCK = 16
D_MIX = SSD_DINNER + HG_WIDTH

LANES = 128
SUBLANES = 8
VMEM_LIMIT_BYTES = 56 * 1024 * 1024

CHUNK = 128
SAMPLE_ROWS = 8
FFN_TILE = 512
FF_CHUNK = 256
N_FF_CHUNKS = D_FF // FF_CHUNK

OFF_Z = 0
OFF_XBC = OFF_Z + SSD_DINNER
OFF_Q = OFF_XBC + CONV_DIM
OFF_FR = OFF_Q + HG_WIDTH
OFF_IV = OFF_FR + HG_WIDTH
OFF_OG = OFF_IV + HG_WIDTH
OFF_DT = OFF_OG + HG_WIDTH
PROJ_W = OFF_DT + LANES
PROJ_BLOCK = 256


class MixParams(NamedTuple):
  conv_w: jax.Array
  conv_b: jax.Array
  dtb_c: jax.Array
  alog_c: jax.Array
  dskip_e: jax.Array
  ssd_norm: jax.Array
  lb_logits: jax.Array
  hg_norm: jax.Array


def _dot(a, b):
  return jnp.dot(a, b, preferred_element_type=F32)


def _dot_nt(a, b):
  return lax.dot_general(a, b, (((1,), (1,)), ((), ())), preferred_element_type=F32)


def _dot_tn(a, b):
  return lax.dot_general(a, b, (((0,), (0,)), ((), ())), preferred_element_type=F32)


NEG_LOG2_E = -1.4426950408889634


def _sigmoid(x):
  return 1.0 / (1.0 + jnp.exp2(x * NEG_LOG2_E))


def _silu(x):
  return x * _sigmoid(x)


def _softplus(x):
  return jnp.maximum(x, 0.0) + jnp.log1p(jnp.exp(-jnp.abs(x)))


def _rmsnorm(x, w):
  ms = jnp.mean(x * x, axis=-1, keepdims=True)
  return x * lax.rsqrt(ms + EPS) * w


def _cumsum_rows(x):
  c, n = x.shape
  g = c // SUBLANES
  x3 = x.reshape(g, SUBLANES, n)
  sub = lax.broadcasted_iota(jnp.int32, (1, SUBLANES, n), 1)
  s = 1
  while s < SUBLANES:
    x3 = x3 + jnp.where(sub >= s, pltpu.roll(x3, s, axis=1), 0.0)
    s *= 2
  if g > 1:
    tot = jnp.broadcast_to(x3[:, SUBLANES - 1:SUBLANES, :], (g, SUBLANES, n))
    offs = [jnp.zeros((1, SUBLANES, n), x.dtype)]
    for k in range(1, g):
      offs.append(offs[-1] + tot[k - 1:k])
    x3 = x3 + jnp.concatenate(offs, axis=0)
  return x3.reshape(c, n)


def _expand_heads(xc):
  c = xc.shape[0]
  first_head = lax.broadcasted_iota(jnp.int32, (1, LANES), 1) < SSD_HEADDIM
  parts = []
  for j in range(SSD_HEADS // 2):
    a = jnp.broadcast_to(xc[:, 2 * j:2 * j + 1], (c, LANES))
    b = jnp.broadcast_to(xc[:, 2 * j + 1:2 * j + 2], (c, LANES))
    parts.append(jnp.where(first_head, a, b))
  return jnp.concatenate(parts, axis=1)


def _col_matrix(row_vec, n_lanes):
  n = row_vec.shape[1]
  return jnp.transpose(jnp.broadcast_to(row_vec, (n_lanes, n)))


def _mix_chunk(proj, *, c, valid, p, xb_ref, ssm_ref, hg_ref, tick=lambda n=1: None):
  cs = max(c, LANES)

  def padr(x):
    if x.shape[0] == cs:
      return x
    return jnp.concatenate([x, jnp.zeros((cs - x.shape[0], x.shape[1]), x.dtype)], axis=0)

  row_valid = None
  if valid < c:
    row_valid = lax.broadcasted_iota(jnp.int32, (c, 1), 0) < valid

  def mask_rows(x):
    return x if row_valid is None else jnp.where(row_valid, x, 0.0)

  z = proj[:, OFF_Z:OFF_Z + SSD_DINNER]
  xbc = proj[:, OFF_XBC:OFF_XBC + CONV_DIM]
  q = proj[:, OFF_Q:OFF_Q + HG_WIDTH]
  fr = proj[:, OFF_FR:OFF_FR + HG_WIDTH]
  iv = proj[:, OFF_IV:OFF_IV + HG_WIDTH]
  og = proj[:, OFF_OG:OFF_OG + HG_WIDTH]
  dt_raw = proj[:, OFF_DT:OFF_DT + LANES]

  xb_ref[SUBLANES:SUBLANES + c, :] = xbc
  conv = p.conv_b + p.conv_w[CONV_W - 1:CONV_W] * xbc
  for j in range(CONV_W - 1):
    conv = conv + p.conv_w[j:j + 1] * xb_ref[SUBLANES - (CONV_W - 1) + j:SUBLANES - (CONV_W - 1) + j + c, :]
  tick(2)
  conv = _silu(conv)
  tick(2)
  xs = conv[:, :SSD_DINNER]
  bm = conv[:, SSD_DINNER:SSD_DINNER + SSD_GROUPS * SSD_STATE]
  cm = conv[:, SSD_DINNER + SSD_GROUPS * SSD_STATE:]

  head_lane = lax.broadcasted_iota(jnp.int32, (1, LANES), 1) < SSD_HEADS
  dt_c = mask_rows(jnp.where(head_lane, _softplus(dt_raw + p.dtb_c), 0.0))
  acs_c = _cumsum_rows(dt_c * -jnp.exp(p.alog_c))
  last_c = acs_c[c - 1:c, :]
  dec_in = _expand_heads(jnp.exp(acs_c))
  xd_end = xs * _expand_heads(dt_c * jnp.exp(last_c - acs_c))
  tick(2)
  acs_t = jnp.transpose(padr(acs_c))
  dt_t = jnp.transpose(padr(dt_c))
  chunk_decay_t = jnp.transpose(jnp.broadcast_to(jnp.exp(last_c), (LANES, LANES)))

  t_idx = lax.broadcasted_iota(jnp.int32, (c, cs), 0)
  s_idx = lax.broadcasted_iota(jnp.int32, (c, cs), 1)
  causal = s_idx <= t_idx
  lane = lax.broadcasted_iota(jnp.int32, (1, LANES), 1)
  first_head = lane < SSD_HEADDIM

  y_intra, y_inter = [], []
  for g in range(SSD_GROUPS):
    bg = bm[:, g * SSD_STATE:(g + 1) * SSD_STATE]
    cg = cm[:, g * SSD_STATE:(g + 1) * SSD_STATE].astype(BF16)
    bg_p = padr(bg).astype(BF16)
    cb = _dot_nt(cg, bg_p)
    h0 = ssm_ref[g * SSD_HEADS_PER_GROUP:(g + 1) * SSD_HEADS_PER_GROUP].reshape(SSD_GROUP_WIDTH, SSD_STATE)
    y_inter.append(_dot_nt(cg, h0.astype(BF16)))
    for j in range(SSD_HEADS_PER_GROUP // 2):
      h_first = g * SSD_HEADS_PER_GROUP + 2 * j
      scores = []
      for h in (h_first, h_first + 1):
        diff = acs_c[:, h:h + 1] - acs_t[h:h + 1, :]
        decay_dt = jnp.where(causal, jnp.exp(diff), 0.0) * dt_t[h:h + 1, :]
        scores.append((cb * decay_dt).astype(BF16))
      pair = padr(xs[:, h_first * SSD_HEADDIM:(h_first + 2) * SSD_HEADDIM])
      rhs = jnp.concatenate([jnp.where(first_head, pair, 0.0), jnp.where(first_head, 0.0, pair)], axis=0)
      y_intra.append(_dot(jnp.concatenate(scores, axis=1), rhs.astype(BF16)))
      tick()
    xd_g = padr(xd_end[:, g * SSD_GROUP_WIDTH:(g + 1) * SSD_GROUP_WIDTH]).astype(BF16)
    decay_cols = jnp.concatenate([
        jnp.broadcast_to(chunk_decay_t[h:h + 1, :], (SSD_HEADDIM, SSD_STATE))
        for h in range(g * SSD_HEADS_PER_GROUP, (g + 1) * SSD_HEADS_PER_GROUP)], axis=0)
    h1 = h0 * decay_cols + _dot_tn(xd_g, bg_p)
    ssm_ref[g * SSD_HEADS_PER_GROUP:(g + 1) * SSD_HEADS_PER_GROUP] = h1.reshape(
        SSD_HEADS_PER_GROUP, SSD_HEADDIM, SSD_STATE)
  y = jnp.concatenate(y_intra, axis=1) + jnp.concatenate(y_inter, axis=1) * dec_in + p.dskip_e * xs
  tick()
  yg = y * _silu(z)
  tick()
  y_norm = []
  for g in range(SSD_GROUPS):
    blk = yg[:, g * SSD_GROUP_WIDTH:(g + 1) * SSD_GROUP_WIDTH]
    y_norm.append(blk * lax.rsqrt(jnp.mean(blk * blk, axis=-1, keepdims=True) + EPS))
  y_ssd = jnp.concatenate(y_norm, axis=1) * p.ssd_norm

  lg = p.lb_logits
  lg_max = jnp.max(lg, axis=0, keepdims=True)
  lg_exp = jnp.exp(lg - lg_max)
  lb = lg_exp[0:1] / jnp.sum(lg_exp, axis=0, keepdims=True)
  f = lb + (1.0 - lb) * _sigmoid(fr)
  tick()
  logf = mask_rows(jnp.log(f))
  kk = mask_rows(1.0 - f)
  qq = _silu(q)
  tick()
  b = _cumsum_rows(logf)
  tick(2)
  b_last = b[c - 1:c, :]
  q_inter = (qq * jnp.exp(b)).astype(BF16)
  k_end = kk * jnp.exp(b_last - b)
  state_decay = jnp.exp(b_last)
  tick()

  levels = []
  m = c // 2
  while m >= HG_BLOCK:
    q_rows, k_rows = [], []
    zero_half = jnp.zeros((m, HG_WIDTH), BF16)
    for i in range(c // (2 * m)):
      lo = i * 2 * m
      b_ref_row = b[lo + m - 1:lo + m, :]
      k_rows += [(kk[lo:lo + m] * jnp.exp(b_ref_row - b[lo:lo + m])).astype(BF16), zero_half]
      q_rows += [zero_half, (qq[lo + m:lo + 2 * m] * jnp.exp(b[lo + m:lo + 2 * m] - b_ref_row)).astype(BF16)]
    shift = m.bit_length() - 1
    mask = ((t_idx >> (shift + 1)) == (s_idx >> (shift + 1))) & (((t_idx >> shift) & 1) == 1) & (
        ((s_idx >> shift) & 1) == 0)
    levels.append((jnp.concatenate(q_rows, axis=0), jnp.concatenate(k_rows, axis=0), mask))
    tick()
    m //= 2
  blk = min(HG_BLOCK, c)
  q_rows, k_rows = [], []
  for i in range(c // blk):
    lo = i * blk
    b_loc = b[lo:lo + blk] if i == 0 else b[lo:lo + blk] - b[lo - 1:lo, :]
    q_rows.append(qq[lo:lo + blk] * jnp.exp(b_loc))
    k_rows.append(kk[lo:lo + blk] * jnp.exp(-b_loc))
  shift = blk.bit_length() - 1
  levels.append((jnp.concatenate(q_rows, axis=0).astype(BF16), padr(jnp.concatenate(k_rows, axis=0)).astype(BF16),
                 ((t_idx >> shift) == (s_idx >> shift)) & causal))

  tick(2)
  o_parts = []
  for h in range(HG_HEADS):
    sl = slice(h * HG_DK, (h + 1) * HG_DK)
    att = jnp.zeros((c, cs), F32)
    for ql, kl, mk in levels:
      att = jnp.where(mk, _dot_nt(ql[:, sl], kl[:, sl]), att)
    v_h = padr(iv[:, sl]).astype(BF16)
    s0 = hg_ref[h]
    o_h = _dot(att.astype(BF16), v_h) + _dot(q_inter[:, sl], s0.astype(BF16))
    s1 = s0 * _col_matrix(state_decay[:, sl], HG_DV) + _dot_tn(padr(k_end[:, sl]).astype(BF16), v_h)
    hg_ref[h] = s1
    o_parts.append(o_h * lax.rsqrt(jnp.mean(o_h * o_h, axis=-1, keepdims=True) + EPS))
    tick()
  o = jnp.concatenate(o_parts, axis=1) * p.hg_norm * _silu(og)
  return jnp.concatenate([y_ssd, o], axis=1)


def _load_mix_params(refs):
  return MixParams(*[r[...] for r in refs])


def _swiglu(h, wup_ref, wdn_ref):
  acc = jnp.zeros((h.shape[0], D_MODEL), F32)
  for ci in range(N_FF_CHUNKS):
    lo = ci * FF_CHUNK
    g = _dot(h, wup_ref[:, lo:lo + FF_CHUNK])
    u = _dot(h, wup_ref[:, D_FF + lo:D_FF + lo + FF_CHUNK])
    acc = acc + _dot((_silu(g) * u).astype(BF16), wdn_ref[lo:lo + FF_CHUNK, :])
  return acc


def _ffn1_kernel(x_ref, nw_ref, wup_ref, wdn_ref, o_ref):
  x = x_ref[...]
  h = _rmsnorm(x, nw_ref[...]).astype(BF16)
  o_ref[...] = x + 0.5 * _swiglu(h, wup_ref, wdn_ref)


def _mix_prompt_kernel(x_ref, nmix_ref, wall_ref, *rest, chunks_per_seq):
  prm_refs = rest[:len(MixParams._fields)]
  mixed_ref, conv_ref, ssm_ref, hg_ref, xb_ref, proj_a, proj_b = rest[len(MixParams._fields):]
  i = pl.program_id(0)
  t_cur = lax.rem(jnp.maximum(i - 1, 0), chunks_per_seq)

  @pl.when(i == 0)
  def _():
    proj_b[...] = jnp.zeros(proj_b.shape, F32)

  @pl.when(t_cur == 0)
  def _():
    xb_ref[0:SUBLANES, :] = jnp.zeros((SUBLANES, CONV_DIM), F32)
    ssm_ref[...] = jnp.zeros(ssm_ref.shape, F32)
    hg_ref[...] = jnp.zeros(hg_ref.shape, F32)

  def step(cur_ref, next_ref):
    h = _rmsnorm(x_ref[0], nmix_ref[...]).astype(BF16)
    pending = list(range(0, PROJ_W, PROJ_BLOCK))

    def tick(n=1):
      for _ in range(min(n, len(pending))):
        lo = pending.pop(0)
        hi = min(lo + PROJ_BLOCK, PROJ_W)
        next_ref[:, lo:hi] = _dot(h, wall_ref[:, lo:hi])

    mixed = _mix_chunk(cur_ref[...], c=CHUNK, valid=CHUNK, p=_load_mix_params(prm_refs), xb_ref=xb_ref,
                       ssm_ref=ssm_ref.at[0], hg_ref=hg_ref.at[0], tick=tick)
    tick(len(pending))
    mixed_ref[0] = mixed.astype(BF16)
    tail = xb_ref[CHUNK:CHUNK + SUBLANES, :]
    xb_ref[0:SUBLANES, :] = tail

  parity = lax.rem(i, 2)
  pl.when(parity == 0)(functools.partial(step, proj_b, proj_a))
  pl.when(parity == 1)(functools.partial(step, proj_a, proj_b))

  @pl.when((t_cur == chunks_per_seq - 1) & (i > 0))
  def _():
    conv_ref[0] = xb_ref[SUBLANES + CHUNK - (CONV_W - 1):SUBLANES + CHUNK, :]


def _proj_sample_kernel(x_ref, nmix_ref, wall_ref, o_ref):
  h = _rmsnorm(x_ref[...], nmix_ref[...]).astype(BF16)
  o_ref[...] = _dot(h, wall_ref[...])


def _rec_sample_kernel(proj_ref, conv_in_ref, ssm_in_ref, hg_in_ref, *rest, valid):
  prm_refs = rest[:len(MixParams._fields)]
  mixed_ref, conv_ref, ssm_ref, hg_ref, xb_ref = rest[len(MixParams._fields):]
  xb_ref[0:SUBLANES, :] = jnp.zeros((SUBLANES, CONV_DIM), F32)
  xb_ref[SUBLANES - (CONV_W - 1):SUBLANES, :] = conv_in_ref[0]
  ssm_ref[...] = ssm_in_ref[...]
  hg_ref[...] = hg_in_ref[...]
  mixed = _mix_chunk(proj_ref[0], c=SAMPLE_ROWS, valid=valid, p=_load_mix_params(prm_refs), xb_ref=xb_ref,
                     ssm_ref=ssm_ref.at[0], hg_ref=hg_ref.at[0])
  mixed_ref[0] = mixed.astype(BF16)
  conv_ref[0] = xb_ref[SUBLANES + valid - (CONV_W - 1):SUBLANES + valid, :]


def _tail_kernel(x1_ref, mixed_ref, p_ref, wout_ref, nf2_ref, wup_ref, wdn_ref, nple_ref, wgate_ref, wproj_ref,
                 ppost_ref, nfin_ref, o_ref):
  x2 = x1_ref[...] + _dot(mixed_ref[...], wout_ref[...])
  h = _rmsnorm(x2, nf2_ref[...]).astype(BF16)
  x3 = x2 + 0.5 * _swiglu(h, wup_ref, wdn_ref)
  gate = _sigmoid(_dot(_rmsnorm(x3, nple_ref[...]).astype(BF16), wgate_ref[...]))
  e = _rmsnorm(_dot(p_ref[...].astype(BF16), wproj_ref[...]), ppost_ref[...])
  x4 = x3 + gate * e
  o_ref[...] = _rmsnorm(x4, nfin_ref[...])


def _resident(shape):
  nd = len(shape)
  return pl.BlockSpec(shape, lambda *_: (0,) * nd, pipeline_mode=pl.Buffered(1))


def _params(semantics):
  return pltpu.CompilerParams(dimension_semantics=semantics, vmem_limit_bytes=VMEM_LIMIT_BYTES)


def _ffn1(x, nw, wup, wdn):
  n = x.shape[0]
  return pl.pallas_call(
      _ffn1_kernel,
      grid=(n // FFN_TILE,),
      in_specs=[pl.BlockSpec((FFN_TILE, D_MODEL), lambda i: (i, 0)), _resident(nw.shape), _resident(wup.shape),
                _resident(wdn.shape)],
      out_specs=pl.BlockSpec((FFN_TILE, D_MODEL), lambda i: (i, 0)),
      out_shape=jax.ShapeDtypeStruct((n, D_MODEL), F32),
      compiler_params=_params(("arbitrary",)),
      name="ffn1",
  )(x, nw, wup, wdn)


def _mix_prompt(x1, nmix, wall, prm):
  bsz, seq, _ = x1.shape
  nt = seq // CHUNK
  n_chunks = bsz * nt
  out_shape = (
      jax.ShapeDtypeStruct((bsz, seq, D_MIX), BF16),
      jax.ShapeDtypeStruct((bsz, CONV_W - 1, CONV_DIM), F32),
      jax.ShapeDtypeStruct((bsz, SSD_HEADS, SSD_HEADDIM, SSD_STATE), F32),
      jax.ShapeDtypeStruct((bsz, HG_HEADS, HG_DK, HG_DV), F32),
  )
  proj_chunk = lambda i: jnp.minimum(i, n_chunks - 1)
  mix_chunk = lambda i: jnp.maximum(i - 1, 0)
  out_specs = (
      pl.BlockSpec((1, CHUNK, D_MIX), lambda i: (mix_chunk(i) // nt, mix_chunk(i) % nt, 0)),
      pl.BlockSpec((1, CONV_W - 1, CONV_DIM), lambda i: (mix_chunk(i) // nt, 0, 0)),
      pl.BlockSpec((1, SSD_HEADS, SSD_HEADDIM, SSD_STATE), lambda i: (mix_chunk(i) // nt, 0, 0, 0)),
      pl.BlockSpec((1, HG_HEADS, HG_DK, HG_DV), lambda i: (mix_chunk(i) // nt, 0, 0, 0)),
  )
  in_specs = [pl.BlockSpec((1, CHUNK, D_MODEL), lambda i: (proj_chunk(i) // nt, proj_chunk(i) % nt, 0)),
              _resident(nmix.shape), _resident(wall.shape)] + [_resident(a.shape) for a in prm]
  return pl.pallas_call(
      functools.partial(_mix_prompt_kernel, chunks_per_seq=nt),
      grid=(n_chunks + 1,),
      in_specs=in_specs,
      out_specs=out_specs,
      out_shape=out_shape,
      scratch_shapes=[pltpu.VMEM((SUBLANES + CHUNK, CONV_DIM), F32), pltpu.VMEM((CHUNK, PROJ_W), F32),
                      pltpu.VMEM((CHUNK, PROJ_W), F32)],
      compiler_params=_params(("arbitrary",)),
      name="mix_prompt",
  )(x1, nmix, wall, *prm)


def _proj_sample(x1p, nmix, wall):
  n = x1p.shape[0]
  tile = 256
  return pl.pallas_call(
      _proj_sample_kernel,
      grid=(n // tile,),
      in_specs=[pl.BlockSpec((tile, D_MODEL), lambda i: (i, 0)), _resident(nmix.shape), _resident(wall.shape)],
      out_specs=pl.BlockSpec((tile, PROJ_W), lambda i: (i, 0)),
      out_shape=jax.ShapeDtypeStruct((n, PROJ_W), F32),
      compiler_params=_params(("arbitrary",)),
      name="proj_sample",
  )(x1p, nmix, wall)


def _rec_sample(proj, conv0, ssm0, hg0, prm, valid):
  bsz = proj.shape[0]
  out_shape = (
      jax.ShapeDtypeStruct((bsz, SAMPLE_ROWS, D_MIX), BF16),
      jax.ShapeDtypeStruct((bsz, CONV_W - 1, CONV_DIM), F32),
      jax.ShapeDtypeStruct((bsz, SSD_HEADS, SSD_HEADDIM, SSD_STATE), F32),
      jax.ShapeDtypeStruct((bsz, HG_HEADS, HG_DK, HG_DV), F32),
  )
  state_specs = [
      pl.BlockSpec((1, CONV_W - 1, CONV_DIM), lambda b: (b, 0, 0)),
      pl.BlockSpec((1, SSD_HEADS, SSD_HEADDIM, SSD_STATE), lambda b: (b, 0, 0, 0)),
      pl.BlockSpec((1, HG_HEADS, HG_DK, HG_DV), lambda b: (b, 0, 0, 0)),
  ]
  in_specs = [pl.BlockSpec((1, SAMPLE_ROWS, PROJ_W), lambda b: (b, 0, 0))] + state_specs + [
      _resident(a.shape) for a in prm]
  out_specs = tuple([pl.BlockSpec((1, SAMPLE_ROWS, D_MIX), lambda b: (b, 0, 0))] + state_specs)
  return pl.pallas_call(
      functools.partial(_rec_sample_kernel, valid=valid),
      grid=(bsz,),
      in_specs=in_specs,
      out_specs=out_specs,
      out_shape=out_shape,
      scratch_shapes=[pltpu.VMEM((2 * SUBLANES, CONV_DIM), F32)],
      compiler_params=_params(("arbitrary",)),
      name="rec_sample",
  )(proj, conv0, ssm0, hg0, *prm)


def _tail(x1, mixed, p, wout, nf2, wup, wdn, nple, wgate, wproj, ppost, nfin):
  n = x1.shape[0]
  tile = FFN_TILE
  row = lambda w: pl.BlockSpec((tile, w), lambda i: (i, 0))
  weights = (wout, nf2, wup, wdn, nple, wgate, wproj, ppost, nfin)
  return pl.pallas_call(
      _tail_kernel,
      grid=(n // tile,),
      in_specs=[row(D_MODEL), row(D_MIX), row(PLE_DIM)] + [_resident(w.shape) for w in weights],
      out_specs=row(D_MODEL),
      out_shape=jax.ShapeDtypeStruct((n, D_MODEL), F32),
      compiler_params=_params(("arbitrary",)),
      name="tail",
  )(x1, mixed, p, *weights)


def kernel(x_prompt, x_sample, state_conv, state_ssm, state_hgrn, p_prompt, p_sample, norm_ffn1, w_ffn1_up, w_ffn1_down, norm_mix, w_in, conv_w, conv_b, dt_bias, a_log, d_skip, ssd_norm, hg_lb_logits, hg_norm, w_out, norm_ffn2, w_ffn2_up, w_ffn2_down, norm_ple, w_ple_gate, w_ple_proj, ple_post_norm, norm_final):
  bp, seq, _ = x_prompt.shape
  bs, dec_seq, _ = x_sample.shape
  row = lambda v: v.reshape(1, -1).astype(F32)
  per_head = lambda v: jnp.repeat(v.astype(F32), SSD_HEADDIM).reshape(1, SSD_DINNER)
  head_lanes = lambda v: jnp.pad(v.astype(F32), (0, LANES - SSD_HEADS)).reshape(1, LANES)

  w1u, w1d = w_ffn1_up[0].astype(BF16), w_ffn1_down[0].astype(BF16)
  w2u, w2d = w_ffn2_up[0].astype(BF16), w_ffn2_down[0].astype(BF16)
  wi = w_in[0].astype(BF16)
  o_dt = SSD_DINNER + CONV_DIM
  o_q = o_dt + SSD_HEADS
  wall = jnp.concatenate([wi[:, :o_dt], wi[:, o_q:], jnp.pad(wi[:, o_dt:o_q], ((0, 0), (0, LANES - SSD_HEADS)))],
                         axis=1)
  prm = MixParams(
      conv_w=conv_w[0].astype(F32), conv_b=row(conv_b[0]), dtb_c=head_lanes(dt_bias[0]),
      alog_c=head_lanes(a_log[0]), dskip_e=per_head(d_skip[0]), ssd_norm=row(ssd_norm[0]),
      lb_logits=hg_lb_logits.astype(F32), hg_norm=row(hg_norm[0]))
  nmix = row(norm_mix[0])
  tail_w = (w_out[0].astype(BF16), row(norm_ffn2[0]), w2u, w2d, row(norm_ple[0]), w_ple_gate[0].astype(BF16),
            w_ple_proj[0].astype(BF16), row(ple_post_norm[0]), row(norm_final))

  xp = x_prompt.reshape(bp * seq, D_MODEL)
  x1p = _ffn1(xp, row(norm_ffn1[0]), w1u, w1d)
  mixed_p, conv_p, ssm_p, hg_p = _mix_prompt(x1p.reshape(bp, seq, D_MODEL), nmix, wall, prm)
  y_prompt = _tail(x1p, mixed_p.reshape(bp * seq, D_MIX), p_prompt[0].reshape(bp * seq, PLE_DIM), *tail_w)

  xs = x_sample.reshape(bs * dec_seq, D_MODEL)
  x1s = _ffn1(xs, row(norm_ffn1[0]), w1u, w1d)
  x1s_pad = jnp.pad(x1s.reshape(bs, dec_seq, D_MODEL), ((0, 0), (0, SAMPLE_ROWS - dec_seq), (0, 0)))
  proj_s = _proj_sample(x1s_pad.reshape(bs * SAMPLE_ROWS, D_MODEL), nmix, wall)
  mixed_s, conv_s, ssm_s, hg_s = _rec_sample(proj_s.reshape(bs, SAMPLE_ROWS, PROJ_W), state_conv[0], state_ssm[0],
                                             state_hgrn[0], prm, dec_seq)
  mixed_s = mixed_s[:, :dec_seq].reshape(bs * dec_seq, D_MIX)
  y_sample = _tail(x1s, mixed_s, p_sample[0].reshape(bs * dec_seq, PLE_DIM), *tail_w)

  return (y_prompt.reshape(bp, seq, D_MODEL), y_sample.reshape(bs, dec_seq, D_MODEL), conv_p[None], ssm_p[None],
          hg_p[None], conv_s[None], ssm_s[None], hg_s[None])
```

```python
import functools
from typing import NamedTuple

import jax
import jax.numpy as jnp
from jax import lax
from jax.experimental import pallas as pl
from jax.experimental.pallas import tpu as pltpu

F32 = jnp.float32
BF16 = jnp.bfloat16

D_MODEL = 1024
D_FF = 2816
PLE_DIM = 256
EPS = 1e-6
SSD_HEADS = 16
SSD_HEADDIM = 64
SSD_DINNER = SSD_HEADS * SSD_HEADDIM
SSD_STATE = 128
SSD_GROUPS = 2
SSD_HEADS_PER_GROUP = SSD_HEADS // SSD_GROUPS
SSD_GROUP_WIDTH = SSD_DINNER // SSD_GROUPS
CONV_W = 4
CONV_DIM = SSD_DINNER + 2 * SSD_GROUPS * SSD_STATE
HG_HEADS = 8
HG_DK = 128
HG_DV = 128
HG_WIDTH = HG_HEADS * HG_DV
HG_BLOCK = 16
D_MIX = SSD_DINNER + HG_WIDTH

LANES = 128
SUBLANES = 8
VMEM_LIMIT_BYTES = 56 * 1024 * 1024

CHUNK = 128
SAMPLE_ROWS = 8
FFN_TILE = 512
FF_CHUNK = 256
N_FF_CHUNKS = D_FF // FF_CHUNK

OFF_Z = 0
OFF_XBC = OFF_Z + SSD_DINNER
OFF_Q = OFF_XBC + CONV_DIM
OFF_FR = OFF_Q + HG_WIDTH
OFF_IV = OFF_FR + HG_WIDTH
OFF_OG = OFF_IV + HG_WIDTH
OFF_DT = OFF_OG + HG_WIDTH
PROJ_W = OFF_DT + LANES
PROJ_BLOCK = 256


class MixParams(NamedTuple):
  conv_w: jax.Array
  conv_b: jax.Array
  dtb_c: jax.Array
  alog_c: jax.Array
  dskip_e: jax.Array
  ssd_norm: jax.Array
  lb_logits: jax.Array
  hg_norm: jax.Array


def _dot(a, b):
  return jnp.dot(a, b, preferred_element_type=F32)


def _dot_nt(a, b):
  return lax.dot_general(a, b, (((1,), (1,)), ((), ())), preferred_element_type=F32)


def _dot_tn(a, b):
  return lax.dot_general(a, b, (((0,), (0,)), ((), ())), preferred_element_type=F32)


NEG_LOG2_E = -1.4426950408889634


def _sigmoid(x):
  return 1.0 / (1.0 + jnp.exp2(x * NEG_LOG2_E))


def _silu(x):
  return x * _sigmoid(x)


def _softplus(x):
  return jnp.maximum(x, 0.0) + jnp.log1p(jnp.exp(-jnp.abs(x)))


def _rmsnorm(x, w):
  ms = jnp.mean(x * x, axis=-1, keepdims=True)
  return x * lax.rsqrt(ms + EPS) * w


def _cumsum_rows(x):
  c, n = x.shape
  g = c // SUBLANES
  x3 = x.reshape(g, SUBLANES, n)
  sub = lax.broadcasted_iota(jnp.int32, (1, SUBLANES, n), 1)
  s = 1
  while s < SUBLANES:
    x3 = x3 + jnp.where(sub >= s, pltpu.roll(x3, s, axis=1), 0.0)
    s *= 2
  if g > 1:
    tot = jnp.broadcast_to(x3[:, SUBLANES - 1:SUBLANES, :], (g, SUBLANES, n))
    offs = [jnp.zeros((1, SUBLANES, n), x.dtype)]
    for k in range(1, g):
      offs.append(offs[-1] + tot[k - 1:k])
    x3 = x3 + jnp.concatenate(offs, axis=0)
  return x3.reshape(c, n)


def _expand_heads(xc):
  c = xc.shape[0]
  first_head = lax.broadcasted_iota(jnp.int32, (1, LANES), 1) < SSD_HEADDIM
  parts = []
  for j in range(SSD_HEADS // 2):
    a = jnp.broadcast_to(xc[:, 2 * j:2 * j + 1], (c, LANES))
    b = jnp.broadcast_to(xc[:, 2 * j + 1:2 * j + 2], (c, LANES))
    parts.append(jnp.where(first_head, a, b))
  return jnp.concatenate(parts, axis=1)


def _col_matrix(row_vec, n_lanes):
  n = row_vec.shape[1]
  return jnp.transpose(jnp.broadcast_to(row_vec, (n_lanes, n)))


def _mix_chunk(proj_ref, *, c, valid, p, xb_ref, ssm_ref, hg_ref, tick=lambda n=1: None):
  cs = max(c, LANES)

  def padr(x):
    if x.shape[0] == cs:
      return x
    return jnp.concatenate([x, jnp.zeros((cs - x.shape[0], x.shape[1]), x.dtype)], axis=0)

  row_valid = None
  if valid < c:
    row_valid = lax.broadcasted_iota(jnp.int32, (c, 1), 0) < valid

  def mask_rows(x):
    return x if row_valid is None else jnp.where(row_valid, x, 0.0)

  cols = lambda off, width: proj_ref[:, off:off + width]

  xbc = cols(OFF_XBC, CONV_DIM)
  xb_ref[SUBLANES:SUBLANES + c, :] = xbc
  conv = p.conv_b + p.conv_w[CONV_W - 1:CONV_W] * xbc
  for j in range(CONV_W - 1):
    conv = conv + p.conv_w[j:j + 1] * xb_ref[SUBLANES - (CONV_W - 1) + j:SUBLANES - (CONV_W - 1) + j + c, :]
  tick(2)
  conv = _silu(conv)
  tick(2)
  xs = conv[:, :SSD_DINNER]
  bm = conv[:, SSD_DINNER:SSD_DINNER + SSD_GROUPS * SSD_STATE]
  cm = conv[:, SSD_DINNER + SSD_GROUPS * SSD_STATE:]

  head_lane = lax.broadcasted_iota(jnp.int32, (1, LANES), 1) < SSD_HEADS
  dt_c = mask_rows(jnp.where(head_lane, _softplus(cols(OFF_DT, LANES) + p.dtb_c), 0.0))
  acs_c = _cumsum_rows(dt_c * -jnp.exp(p.alog_c))
  last_c = acs_c[c - 1:c, :]
  dec_in = _expand_heads(jnp.exp(acs_c))
  xd_end = xs * _expand_heads(dt_c * jnp.exp(last_c - acs_c))
  tick(2)
  acs_t = jnp.transpose(padr(acs_c))
  dt_t = jnp.transpose(padr(dt_c))
  chunk_decay_t = jnp.transpose(jnp.broadcast_to(jnp.exp(last_c), (LANES, LANES)))

  t_idx = lax.broadcasted_iota(jnp.int32, (c, cs), 0)
  s_idx = lax.broadcasted_iota(jnp.int32, (c, cs), 1)
  causal = s_idx <= t_idx
  lane = lax.broadcasted_iota(jnp.int32, (1, LANES), 1)
  first_head = lane < SSD_HEADDIM

  group_heads = [range(g * SSD_HEADS_PER_GROUP, (g + 1) * SSD_HEADS_PER_GROUP) for g in range(SSD_GROUPS)]
  bg_p = [padr(bm[:, g * SSD_STATE:(g + 1) * SSD_STATE]).astype(BF16) for g in range(SSD_GROUPS)]
  cg = [cm[:, g * SSD_STATE:(g + 1) * SSD_STATE].astype(BF16) for g in range(SSD_GROUPS)]
  h0 = [ssm_ref[heads.start:heads.stop].reshape(SSD_GROUP_WIDTH, SSD_STATE) for heads in group_heads]
  cb = [_dot_nt(cg[g], bg_p[g]) for g in range(SSD_GROUPS)]
  y_inter = [_dot_nt(cg[g], h0[g].astype(BF16)) for g in range(SSD_GROUPS)]
  xs_bf = padr(xs).astype(BF16)
  zero_bf = jnp.zeros((cs, LANES), BF16)
  y_intra = []
  for g in range(SSD_GROUPS):
    for j in range(SSD_HEADS_PER_GROUP // 2):
      h_first = g * SSD_HEADS_PER_GROUP + 2 * j
      scores = []
      for h in (h_first, h_first + 1):
        diff = acs_c[:, h:h + 1] - acs_t[h:h + 1, :]
        decay_dt = jnp.where(causal, jnp.exp(diff), 0.0) * dt_t[h:h + 1, :]
        scores.append((cb[g] * decay_dt).astype(BF16))
      pair = xs_bf[:, h_first * SSD_HEADDIM:(h_first + 2) * SSD_HEADDIM]
      rhs = jnp.concatenate([jnp.where(first_head, pair, zero_bf), jnp.where(first_head, zero_bf, pair)], axis=0)
      y_intra.append(_dot(jnp.concatenate(scores, axis=1), rhs))
      tick()
  for g, heads in enumerate(group_heads):
    xd_g = padr(xd_end[:, g * SSD_GROUP_WIDTH:(g + 1) * SSD_GROUP_WIDTH]).astype(BF16)
    decay_cols = jnp.concatenate(
        [jnp.broadcast_to(chunk_decay_t[h:h + 1, :], (SSD_HEADDIM, SSD_STATE)) for h in heads], axis=0)
    h1 = h0[g] * decay_cols + _dot_tn(xd_g, bg_p[g])
    ssm_ref[heads.start:heads.stop] = h1.reshape(SSD_HEADS_PER_GROUP, SSD_HEADDIM, SSD_STATE)
  y = jnp.concatenate(y_intra, axis=1) + jnp.concatenate(y_inter, axis=1) * dec_in + p.dskip_e * xs
  tick()
  yg = y * _silu(cols(OFF_Z, SSD_DINNER))
  tick()
  y_norm = []
  for g in range(SSD_GROUPS):
    blk = yg[:, g * SSD_GROUP_WIDTH:(g + 1) * SSD_GROUP_WIDTH]
    y_norm.append(blk * lax.rsqrt(jnp.mean(blk * blk, axis=-1, keepdims=True) + EPS))
  y_ssd = jnp.concatenate(y_norm, axis=1) * p.ssd_norm

  lg = p.lb_logits
  lg_max = jnp.max(lg, axis=0, keepdims=True)
  lg_exp = jnp.exp(lg - lg_max)
  lb = lg_exp[0:1] / jnp.sum(lg_exp, axis=0, keepdims=True)
  f = lb + (1.0 - lb) * _sigmoid(cols(OFF_FR, HG_WIDTH))
  tick()
  logf = mask_rows(jnp.log(f))
  kk = mask_rows(1.0 - f)
  qq = _silu(cols(OFF_Q, HG_WIDTH))
  tick()
  b = _cumsum_rows(logf)
  tick(2)
  b_last = b[c - 1:c, :]
  q_inter = (qq * jnp.exp(b)).astype(BF16)
  k_end = kk * jnp.exp(b_last - b)
  state_decay = jnp.exp(b_last)
  tick()

  levels = []
  m = c // 2
  while m >= HG_BLOCK:
    q_rows, k_rows = [], []
    zero_half = jnp.zeros((m, HG_WIDTH), BF16)
    for i in range(c // (2 * m)):
      lo = i * 2 * m
      b_ref_row = b[lo + m - 1:lo + m, :]
      k_rows += [(kk[lo:lo + m] * jnp.exp(b_ref_row - b[lo:lo + m])).astype(BF16), zero_half]
      q_rows += [zero_half, (qq[lo + m:lo + 2 * m] * jnp.exp(b[lo + m:lo + 2 * m] - b_ref_row)).astype(BF16)]
    shift = m.bit_length() - 1
    mask = ((t_idx >> (shift + 1)) == (s_idx >> (shift + 1))) & (((t_idx >> shift) & 1) == 1) & (
        ((s_idx >> shift) & 1) == 0)
    levels.append((jnp.concatenate(q_rows, axis=0), jnp.concatenate(k_rows, axis=0), mask))
    tick()
    m //= 2
  blk = min(HG_BLOCK, c)
  q_rows, k_rows = [], []
  for i in range(c // blk):
    lo = i * blk
    b_loc = b[lo:lo + blk] if i == 0 else b[lo:lo + blk] - b[lo - 1:lo, :]
    q_rows.append(qq[lo:lo + blk] * jnp.exp(b_loc))
    k_rows.append(kk[lo:lo + blk] * jnp.exp(-b_loc))
  shift = blk.bit_length() - 1
  levels.append((jnp.concatenate(q_rows, axis=0).astype(BF16), padr(jnp.concatenate(k_rows, axis=0)).astype(BF16),
                 ((t_idx >> shift) == (s_idx >> shift)) & causal))

  tick(2)
  head_slices = [slice(h * HG_DK, (h + 1) * HG_DK) for h in range(HG_HEADS)]
  v_bf = padr(cols(OFF_IV, HG_WIDTH)).astype(BF16)
  k_end_bf = padr(k_end).astype(BF16)
  att_bf = []
  for sl in head_slices:
    att = jnp.zeros((c, cs), F32)
    for ql, kl, mk in levels:
      att = jnp.where(mk, _dot_nt(ql[:, sl], kl[:, sl]), att)
    att_bf.append(att.astype(BF16))
    tick()
  s0 = [hg_ref[h] for h in range(HG_HEADS)]
  o_heads = [_dot(att_bf[h], v_bf[:, sl]) + _dot(q_inter[:, sl], s0[h].astype(BF16))
             for h, sl in enumerate(head_slices)]
  for h, sl in enumerate(head_slices):
    hg_ref[h] = s0[h] * _col_matrix(state_decay[:, sl], HG_DV) + _dot_tn(k_end_bf[:, sl], v_bf[:, sl])
  o_parts = [o_h * lax.rsqrt(jnp.mean(o_h * o_h, axis=-1, keepdims=True) + EPS) for o_h in o_heads]
  o = jnp.concatenate(o_parts, axis=1) * p.hg_norm * _silu(cols(OFF_OG, HG_WIDTH))
  return jnp.concatenate([y_ssd, o], axis=1)


def _load_mix_params(refs):
  return MixParams(*[r[...] for r in refs])


def _swiglu(h, wup_ref, wdn_ref):
  acc = jnp.zeros((h.shape[0], D_MODEL), F32)
  for ci in range(N_FF_CHUNKS):
    lo = ci * FF_CHUNK
    g = _dot(h, wup_ref[:, lo:lo + FF_CHUNK])
    u = _dot(h, wup_ref[:, D_FF + lo:D_FF + lo + FF_CHUNK])
    acc = acc + _dot((_silu(g) * u).astype(BF16), wdn_ref[lo:lo + FF_CHUNK, :])
  return acc


def _ffn1_kernel(x_ref, nw_ref, wup_ref, wdn_ref, o_ref):
  x = x_ref[...]
  h = _rmsnorm(x, nw_ref[...]).astype(BF16)
  o_ref[...] = x + 0.5 * _swiglu(h, wup_ref, wdn_ref)


def _mix_prompt_kernel(x_ref, nmix_ref, wall_ref, *rest, chunks_per_seq):
  prm_refs = rest[:len(MixParams._fields)]
  mixed_ref, conv_ref, ssm_ref, hg_ref, xb_ref, proj_a, proj_b = rest[len(MixParams._fields):]
  i = pl.program_id(0)
  t_cur = lax.rem(jnp.maximum(i - 1, 0), chunks_per_seq)

  @pl.when(i == 0)
  def _():
    proj_b[...] = jnp.zeros(proj_b.shape, F32)

  @pl.when(t_cur == 0)
  def _():
    xb_ref[0:SUBLANES, :] = jnp.zeros((SUBLANES, CONV_DIM), F32)
    ssm_ref[...] = jnp.zeros(ssm_ref.shape, F32)
    hg_ref[...] = jnp.zeros(hg_ref.shape, F32)

  def step(cur_ref, next_ref):
    h = _rmsnorm(x_ref[0], nmix_ref[...]).astype(BF16)
    pending = list(range(0, PROJ_W, PROJ_BLOCK))

    def tick(n=1):
      for _ in range(min(n, len(pending))):
        lo = pending.pop(0)
        hi = min(lo + PROJ_BLOCK, PROJ_W)
        next_ref[:, lo:hi] = _dot(h, wall_ref[:, lo:hi])

    mixed = _mix_chunk(cur_ref, c=CHUNK, valid=CHUNK, p=_load_mix_params(prm_refs), xb_ref=xb_ref,
                       ssm_ref=ssm_ref.at[0], hg_ref=hg_ref.at[0], tick=tick)
    tick(len(pending))
    mixed_ref[0] = mixed.astype(BF16)
    tail = xb_ref[CHUNK:CHUNK + SUBLANES, :]
    xb_ref[0:SUBLANES, :] = tail

  parity = lax.rem(i, 2)
  pl.when(parity == 0)(functools.partial(step, proj_b, proj_a))
  pl.when(parity == 1)(functools.partial(step, proj_a, proj_b))

  @pl.when((t_cur == chunks_per_seq - 1) & (i > 0))
  def _():
    conv_ref[0] = xb_ref[SUBLANES + CHUNK - (CONV_W - 1):SUBLANES + CHUNK, :]


def _proj_sample_kernel(x_ref, nmix_ref, wall_ref, o_ref):
  h = _rmsnorm(x_ref[...], nmix_ref[...]).astype(BF16)
  o_ref[...] = _dot(h, wall_ref[...])


def _rec_sample_kernel(proj_ref, conv_in_ref, ssm_in_ref, hg_in_ref, *rest, valid):
  prm_refs = rest[:len(MixParams._fields)]
  mixed_ref, conv_ref, ssm_ref, hg_ref, xb_ref = rest[len(MixParams._fields):]
  xb_ref[0:SUBLANES, :] = jnp.zeros((SUBLANES, CONV_DIM), F32)
  xb_ref[SUBLANES - (CONV_W - 1):SUBLANES, :] = conv_in_ref[0]
  ssm_ref[...] = ssm_in_ref[...]
  hg_ref[...] = hg_in_ref[...]
  mixed = _mix_chunk(proj_ref.at[0], c=SAMPLE_ROWS, valid=valid, p=_load_mix_params(prm_refs), xb_ref=xb_ref,
                     ssm_ref=ssm_ref.at[0], hg_ref=hg_ref.at[0])
  mixed_ref[0] = mixed.astype(BF16)
  conv_ref[0] = xb_ref[SUBLANES + valid - (CONV_W - 1):SUBLANES + valid, :]


def _tail_kernel(x1_ref, mixed_ref, p_ref, wout_ref, nf2_ref, wup_ref, wdn_ref, nple_ref, wgate_ref, wproj_ref,
                 ppost_ref, nfin_ref, o_ref):
  x2 = x1_ref[...] + _dot(mixed_ref[...], wout_ref[...])
  h = _rmsnorm(x2, nf2_ref[...]).astype(BF16)
  x3 = x2 + 0.5 * _swiglu(h, wup_ref, wdn_ref)
  gate = _sigmoid(_dot(_rmsnorm(x3, nple_ref[...]).astype(BF16), wgate_ref[...]))
  e = _rmsnorm(_dot(p_ref[...].astype(BF16), wproj_ref[...]), ppost_ref[...])
  x4 = x3 + gate * e
  o_ref[...] = _rmsnorm(x4, nfin_ref[...])


def _resident(shape):
  nd = len(shape)
  return pl.BlockSpec(shape, lambda *_: (0,) * nd, pipeline_mode=pl.Buffered(1))


def _params(semantics, flags=None):
  return pltpu.CompilerParams(dimension_semantics=semantics, vmem_limit_bytes=VMEM_LIMIT_BYTES, flags=flags)


def _ffn1(x, nw, wup, wdn):
  n = x.shape[0]
  return pl.pallas_call(
      _ffn1_kernel,
      grid=(n // FFN_TILE,),
      in_specs=[pl.BlockSpec((FFN_TILE, D_MODEL), lambda i: (i, 0)), _resident(nw.shape), _resident(wup.shape),
                _resident(wdn.shape)],
      out_specs=pl.BlockSpec((FFN_TILE, D_MODEL), lambda i: (i, 0)),
      out_shape=jax.ShapeDtypeStruct((n, D_MODEL), F32),
      compiler_params=_params(("arbitrary",)),
      name="ffn1",
  )(x, nw, wup, wdn)


def _mix_prompt(x1, nmix, wall, prm):
  bsz, seq, _ = x1.shape
  nt = seq // CHUNK
  n_chunks = bsz * nt
  out_shape = (
      jax.ShapeDtypeStruct((bsz, seq, D_MIX), BF16),
      jax.ShapeDtypeStruct((bsz, CONV_W - 1, CONV_DIM), F32),
      jax.ShapeDtypeStruct((bsz, SSD_HEADS, SSD_HEADDIM, SSD_STATE), F32),
      jax.ShapeDtypeStruct((bsz, HG_HEADS, HG_DK, HG_DV), F32),
  )
  proj_chunk = lambda i: jnp.minimum(i, n_chunks - 1)
  mix_chunk = lambda i: jnp.maximum(i - 1, 0)
  out_specs = (
      pl.BlockSpec((1, CHUNK, D_MIX), lambda i: (mix_chunk(i) // nt, mix_chunk(i) % nt, 0)),
      pl.BlockSpec((1, CONV_W - 1, CONV_DIM), lambda i: (mix_chunk(i) // nt, 0, 0)),
      pl.BlockSpec((1, SSD_HEADS, SSD_HEADDIM, SSD_STATE), lambda i: (mix_chunk(i) // nt, 0, 0, 0)),
      pl.BlockSpec((1, HG_HEADS, HG_DK, HG_DV), lambda i: (mix_chunk(i) // nt, 0, 0, 0)),
  )
  in_specs = [pl.BlockSpec((1, CHUNK, D_MODEL), lambda i: (proj_chunk(i) // nt, proj_chunk(i) % nt, 0)),
              _resident(nmix.shape), _resident(wall.shape)] + [_resident(a.shape) for a in prm]
  return pl.pallas_call(
      functools.partial(_mix_prompt_kernel, chunks_per_seq=nt),
      grid=(n_chunks + 1,),
      in_specs=in_specs,
      out_specs=out_specs,
      out_shape=out_shape,
      scratch_shapes=[pltpu.VMEM((SUBLANES + CHUNK, CONV_DIM), F32), pltpu.VMEM((CHUNK, PROJ_W), F32),
                      pltpu.VMEM((CHUNK, PROJ_W), F32)],
      compiler_params=_params(("arbitrary",)),
      name="mix_prompt",
  )(x1, nmix, wall, *prm)


def _proj_sample(x1p, nmix, wall):
  n = x1p.shape[0]
  tile = 256
  return pl.pallas_call(
      _proj_sample_kernel,
      grid=(n // tile,),
      in_specs=[pl.BlockSpec((tile, D_MODEL), lambda i: (i, 0)), _resident(nmix.shape), _resident(wall.shape)],
      out_specs=pl.BlockSpec((tile, PROJ_W), lambda i: (i, 0)),
      out_shape=jax.ShapeDtypeStruct((n, PROJ_W), F32),
      compiler_params=_params(("arbitrary",)),
      name="proj_sample",
  )(x1p, nmix, wall)


def _rec_sample(proj, conv0, ssm0, hg0, prm, valid):
  bsz = proj.shape[0]
  out_shape = (
      jax.ShapeDtypeStruct((bsz, SAMPLE_ROWS, D_MIX), BF16),
      jax.ShapeDtypeStruct((bsz, CONV_W - 1, CONV_DIM), F32),
      jax.ShapeDtypeStruct((bsz, SSD_HEADS, SSD_HEADDIM, SSD_STATE), F32),
      jax.ShapeDtypeStruct((bsz, HG_HEADS, HG_DK, HG_DV), F32),
  )
  state_specs = [
      pl.BlockSpec((1, CONV_W - 1, CONV_DIM), lambda b: (b, 0, 0)),
      pl.BlockSpec((1, SSD_HEADS, SSD_HEADDIM, SSD_STATE), lambda b: (b, 0, 0, 0)),
      pl.BlockSpec((1, HG_HEADS, HG_DK, HG_DV), lambda b: (b, 0, 0, 0)),
  ]
  in_specs = [pl.BlockSpec((1, SAMPLE_ROWS, PROJ_W), lambda b: (b, 0, 0))] + state_specs + [
      _resident(a.shape) for a in prm]
  out_specs = tuple([pl.BlockSpec((1, SAMPLE_ROWS, D_MIX), lambda b: (b, 0, 0))] + state_specs)
  return pl.pallas_call(
      functools.partial(_rec_sample_kernel, valid=valid),
      grid=(bsz,),
      in_specs=in_specs,
      out_specs=out_specs,
      out_shape=out_shape,
      scratch_shapes=[pltpu.VMEM((2 * SUBLANES, CONV_DIM), F32)],
      compiler_params=_params(("arbitrary",)),
      name="rec_sample",
  )(proj, conv0, ssm0, hg0, *prm)


def _tail(x1, mixed, p, wout, nf2, wup, wdn, nple, wgate, wproj, ppost, nfin):
  n = x1.shape[0]
  tile = FFN_TILE
  row = lambda w: pl.BlockSpec((tile, w), lambda i: (i, 0))
  weights = (wout, nf2, wup, wdn, nple, wgate, wproj, ppost, nfin)
  return pl.pallas_call(
      _tail_kernel,
      grid=(n // tile,),
      in_specs=[row(D_MODEL), row(D_MIX), row(PLE_DIM)] + [_resident(w.shape) for w in weights],
      out_specs=row(D_MODEL),
      out_shape=jax.ShapeDtypeStruct((n, D_MODEL), F32),
      compiler_params=_params(("arbitrary",)),
      name="tail",
  )(x1, mixed, p, *weights)


def kernel(x_prompt, x_sample, state_conv, state_ssm, state_hgrn, p_prompt, p_sample, norm_ffn1, w_ffn1_up, w_ffn1_down, norm_mix, w_in, conv_w, conv_b, dt_bias, a_log, d_skip, ssd_norm, hg_lb_logits, hg_norm, w_out, norm_ffn2, w_ffn2_up, w_ffn2_down, norm_ple, w_ple_gate, w_ple_proj, ple_post_norm, norm_final):
  bp, seq, _ = x_prompt.shape
  bs, dec_seq, _ = x_sample.shape
  row = lambda v: v.reshape(1, -1).astype(F32)
  per_head = lambda v: jnp.repeat(v.astype(F32), SSD_HEADDIM).reshape(1, SSD_DINNER)
  head_lanes = lambda v: jnp.pad(v.astype(F32), (0, LANES - SSD_HEADS)).reshape(1, LANES)

  w1u, w1d = w_ffn1_up[0].astype(BF16), w_ffn1_down[0].astype(BF16)
  w2u, w2d = w_ffn2_up[0].astype(BF16), w_ffn2_down[0].astype(BF16)
  wi = w_in[0].astype(BF16)
  o_dt = SSD_DINNER + CONV_DIM
  o_q = o_dt + SSD_HEADS
  wall = jnp.concatenate([wi[:, :o_dt], wi[:, o_q:], jnp.pad(wi[:, o_dt:o_q], ((0, 0), (0, LANES - SSD_HEADS)))],
                         axis=1)
  prm = MixParams(
      conv_w=conv_w[0].astype(F32), conv_b=row(conv_b[0]), dtb_c=head_lanes(dt_bias[0]),
      alog_c=head_lanes(a_log[0]), dskip_e=per_head(d_skip[0]), ssd_norm=row(ssd_norm[0]),
      lb_logits=hg_lb_logits.astype(F32), hg_norm=row(hg_norm[0]))
  nmix = row(norm_mix[0])
  tail_w = (w_out[0].astype(BF16), row(norm_ffn2[0]), w2u, w2d, row(norm_ple[0]), w_ple_gate[0].astype(BF16),
            w_ple_proj[0].astype(BF16), row(ple_post_norm[0]), row(norm_final))

  xp = x_prompt.reshape(bp * seq, D_MODEL)
  x1p = _ffn1(xp, row(norm_ffn1[0]), w1u, w1d)
  mixed_p, conv_p, ssm_p, hg_p = _mix_prompt(x1p.reshape(bp, seq, D_MODEL), nmix, wall, prm)
  y_prompt = _tail(x1p, mixed_p.reshape(bp * seq, D_MIX), p_prompt[0].reshape(bp * seq, PLE_DIM), *tail_w)

  xs = x_sample.reshape(bs * dec_seq, D_MODEL)
  x1s = _ffn1(xs, row(norm_ffn1[0]), w1u, w1d)
  x1s_pad = jnp.pad(x1s.reshape(bs, dec_seq, D_MODEL), ((0, 0), (0, SAMPLE_ROWS - dec_seq), (0, 0)))
  proj_s = _proj_sample(x1s_pad.reshape(bs * SAMPLE_ROWS, D_MODEL), nmix, wall)
  mixed_s, conv_s, ssm_s, hg_s = _rec_sample(proj_s.reshape(bs, SAMPLE_ROWS, PROJ_W), state_conv[0], state_ssm[0],
                                             state_hgrn[0], prm, dec_seq)
  mixed_s = mixed_s[:, :dec_seq].reshape(bs * dec_seq, D_MIX)
  y_sample = _tail(x1s, mixed_s, p_sample[0].reshape(bs * dec_seq, PLE_DIM), *tail_w)

  return (y_prompt.reshape(bp, seq, D_MODEL), y_sample.reshape(bs, dec_seq, D_MODEL), conv_p[None], ssm_p[None],
          hg_p[None], conv_s[None], ssm_s[None], hg_s[None])
```

```python
import functools
from typing import NamedTuple

import jax
import jax.numpy as jnp
from jax import lax
from jax.experimental import pallas as pl
from jax.experimental.pallas import tpu as pltpu

F32 = jnp.float32
BF16 = jnp.bfloat16

D_MODEL = 1024
D_FF = 2816
PLE_DIM = 256
EPS = 1e-6
SSD_HEADS = 16
SSD_HEADDIM = 64
SSD_DINNER = SSD_HEADS * SSD_HEADDIM
SSD_STATE = 128
SSD_GROUPS = 2
SSD_HEADS_PER_GROUP = SSD_HEADS // SSD_GROUPS
SSD_GROUP_WIDTH = SSD_DINNER // SSD_GROUPS
CONV_W = 4
CONV_DIM = SSD_DINNER + 2 * SSD_GROUPS * SSD_STATE
HG_HEADS = 8
HG_DK = 128
HG_DV = 128
HG_WIDTH = HG_HEADS * HG_DV
HG_BLOCK = 16
D_MIX = SSD_DINNER + HG_WIDTH

LANES = 128
SUBLANES = 8
VMEM_LIMIT_BYTES = 56 * 1024 * 1024

CHUNK = 128
SAMPLE_ROWS = 8
SAMPLE_SEQS_PER_STEP = 4
FFN_TILE = 512
FF_CHUNK = 256
N_FF_CHUNKS = D_FF // FF_CHUNK

OFF_Z = 0
OFF_XBC = OFF_Z + SSD_DINNER
OFF_Q = OFF_XBC + CONV_DIM
OFF_FR = OFF_Q + HG_WIDTH
OFF_IV = OFF_FR + HG_WIDTH
OFF_OG = OFF_IV + HG_WIDTH
OFF_DT = OFF_OG + HG_WIDTH
PROJ_W = OFF_DT + LANES
PROJ_BLOCK = 256


class MixParams(NamedTuple):
  conv_w: jax.Array
  conv_b: jax.Array
  dtb_c: jax.Array
  alog_c: jax.Array
  dskip_e: jax.Array
  ssd_norm: jax.Array
  lb_logits: jax.Array
  hg_norm: jax.Array


def _dot(a, b):
  return jnp.dot(a, b, preferred_element_type=F32)


def _dot_nt(a, b):
  return lax.dot_general(a, b, (((1,), (1,)), ((), ())), preferred_element_type=F32)


def _dot_tn(a, b):
  return lax.dot_general(a, b, (((0,), (0,)), ((), ())), preferred_element_type=F32)


NEG_LOG2_E = -1.4426950408889634


def _sigmoid(x):
  return 1.0 / (1.0 + jnp.exp2(x * NEG_LOG2_E))


def _silu(x):
  return x * _sigmoid(x)


def _softplus(x):
  return jnp.maximum(x, 0.0) + jnp.log1p(jnp.exp(-jnp.abs(x)))


def _rmsnorm(x, w):
  ms = jnp.mean(x * x, axis=-1, keepdims=True)
  return x * lax.rsqrt(ms + EPS) * w


def _cumsum_rows(x):
  c, n = x.shape
  g = c // SUBLANES
  x3 = x.reshape(g, SUBLANES, n)
  sub = lax.broadcasted_iota(jnp.int32, (1, SUBLANES, n), 1)
  s = 1
  while s < SUBLANES:
    x3 = x3 + jnp.where(sub >= s, pltpu.roll(x3, s, axis=1), 0.0)
    s *= 2
  if g > 1:
    tot = jnp.broadcast_to(x3[:, SUBLANES - 1:SUBLANES, :], (g, SUBLANES, n))
    offs = [jnp.zeros((1, SUBLANES, n), x.dtype)]
    for k in range(1, g):
      offs.append(offs[-1] + tot[k - 1:k])
    x3 = x3 + jnp.concatenate(offs, axis=0)
  return x3.reshape(c, n)


def _expand_heads(xc):
  c = xc.shape[0]
  first_head = lax.broadcasted_iota(jnp.int32, (1, LANES), 1) < SSD_HEADDIM
  parts = []
  for j in range(SSD_HEADS // 2):
    a = jnp.broadcast_to(xc[:, 2 * j:2 * j + 1], (c, LANES))
    b = jnp.broadcast_to(xc[:, 2 * j + 1:2 * j + 2], (c, LANES))
    parts.append(jnp.where(first_head, a, b))
  return jnp.concatenate(parts, axis=1)


def _col_matrix(row_vec, n_lanes):
  n = row_vec.shape[1]
  return jnp.transpose(jnp.broadcast_to(row_vec, (n_lanes, n)))


def _mix_chunk(proj_ref, *, c, valid, p, xb_ref, ssm_ref, hg_ref):
  cs = max(c, LANES)

  def padr(x):
    if x.shape[0] == cs:
      return x
    return jnp.concatenate([x, jnp.zeros((cs - x.shape[0], x.shape[1]), x.dtype)], axis=0)

  row_valid = None
  if valid < c:
    row_valid = lax.broadcasted_iota(jnp.int32, (c, 1), 0) < valid

  def mask_rows(x):
    return x if row_valid is None else jnp.where(row_valid, x, 0.0)

  cols = lambda off, width: proj_ref[:, off:off + width]

  xbc = cols(OFF_XBC, CONV_DIM)
  xb_ref[SUBLANES:SUBLANES + c, :] = xbc
  conv = p.conv_b + p.conv_w[CONV_W - 1:CONV_W] * xbc
  for j in range(CONV_W - 1):
    conv = conv + p.conv_w[j:j + 1] * xb_ref[SUBLANES - (CONV_W - 1) + j:SUBLANES - (CONV_W - 1) + j + c, :]
  yield 2
  conv = _silu(conv)
  yield 2
  xs = conv[:, :SSD_DINNER]
  bm = conv[:, SSD_DINNER:SSD_DINNER + SSD_GROUPS * SSD_STATE]
  cm = conv[:, SSD_DINNER + SSD_GROUPS * SSD_STATE:]

  head_lane = lax.broadcasted_iota(jnp.int32, (1, LANES), 1) < SSD_HEADS
  dt_c = mask_rows(jnp.where(head_lane, _softplus(cols(OFF_DT, LANES) + p.dtb_c), 0.0))
  acs_c = _cumsum_rows(dt_c * -jnp.exp(p.alog_c))
  last_c = acs_c[c - 1:c, :]
  dec_in = _expand_heads(jnp.exp(acs_c))
  xd_end = xs * _expand_heads(dt_c * jnp.exp(last_c - acs_c))
  yield 2
  acs_t = jnp.transpose(padr(acs_c))
  dt_t = jnp.transpose(padr(dt_c))
  chunk_decay_t = jnp.transpose(jnp.broadcast_to(jnp.exp(last_c), (LANES, LANES)))

  t_idx = lax.broadcasted_iota(jnp.int32, (c, cs), 0)
  s_idx = lax.broadcasted_iota(jnp.int32, (c, cs), 1)
  causal = s_idx <= t_idx
  lane = lax.broadcasted_iota(jnp.int32, (1, LANES), 1)
  first_head = lane < SSD_HEADDIM

  group_heads = [range(g * SSD_HEADS_PER_GROUP, (g + 1) * SSD_HEADS_PER_GROUP) for g in range(SSD_GROUPS)]
  bg_p = [padr(bm[:, g * SSD_STATE:(g + 1) * SSD_STATE]).astype(BF16) for g in range(SSD_GROUPS)]
  cg = [cm[:, g * SSD_STATE:(g + 1) * SSD_STATE].astype(BF16) for g in range(SSD_GROUPS)]
  h0 = [ssm_ref[heads.start:heads.stop].reshape(SSD_GROUP_WIDTH, SSD_STATE) for heads in group_heads]
  cb = [_dot_nt(cg[g], bg_p[g]) for g in range(SSD_GROUPS)]
  y_inter = [_dot_nt(cg[g], h0[g].astype(BF16)) for g in range(SSD_GROUPS)]
  xs_bf = padr(xs).astype(BF16)
  zero_bf = jnp.zeros((cs, LANES), BF16)
  y_intra = []
  for g in range(SSD_GROUPS):
    for j in range(SSD_HEADS_PER_GROUP // 2):
      h_first = g * SSD_HEADS_PER_GROUP + 2 * j
      scores = []
      for h in (h_first, h_first + 1):
        diff = acs_c[:, h:h + 1] - acs_t[h:h + 1, :]
        decay_dt = jnp.where(causal, jnp.exp(diff), 0.0) * dt_t[h:h + 1, :]
        scores.append((cb[g] * decay_dt).astype(BF16))
      pair = xs_bf[:, h_first * SSD_HEADDIM:(h_first + 2) * SSD_HEADDIM]
      rhs = jnp.concatenate([jnp.where(first_head, pair, zero_bf), jnp.where(first_head, zero_bf, pair)], axis=0)
      y_intra.append(_dot(jnp.concatenate(scores, axis=1), rhs))
      yield 1
  for g, heads in enumerate(group_heads):
    xd_g = padr(xd_end[:, g * SSD_GROUP_WIDTH:(g + 1) * SSD_GROUP_WIDTH]).astype(BF16)
    decay_cols = jnp.concatenate(
        [jnp.broadcast_to(chunk_decay_t[h:h + 1, :], (SSD_HEADDIM, SSD_STATE)) for h in heads], axis=0)
    h1 = h0[g] * decay_cols + _dot_tn(xd_g, bg_p[g])
    ssm_ref[heads.start:heads.stop] = h1.reshape(SSD_HEADS_PER_GROUP, SSD_HEADDIM, SSD_STATE)
  y = jnp.concatenate(y_intra, axis=1) + jnp.concatenate(y_inter, axis=1) * dec_in + p.dskip_e * xs
  yield 1
  yg = y * _silu(cols(OFF_Z, SSD_DINNER))
  yield 1
  y_norm = []
  for g in range(SSD_GROUPS):
    blk = yg[:, g * SSD_GROUP_WIDTH:(g + 1) * SSD_GROUP_WIDTH]
    y_norm.append(blk * lax.rsqrt(jnp.mean(blk * blk, axis=-1, keepdims=True) + EPS))
  y_ssd = jnp.concatenate(y_norm, axis=1) * p.ssd_norm

  lg = p.lb_logits
  lg_max = jnp.max(lg, axis=0, keepdims=True)
  lg_exp = jnp.exp(lg - lg_max)
  lb = lg_exp[0:1] / jnp.sum(lg_exp, axis=0, keepdims=True)
  f = lb + (1.0 - lb) * _sigmoid(cols(OFF_FR, HG_WIDTH))
  yield 1
  logf = mask_rows(jnp.log(f))
  kk = mask_rows(1.0 - f)
  qq = _silu(cols(OFF_Q, HG_WIDTH))
  yield 1
  b = _cumsum_rows(logf)
  yield 2
  b_last = b[c - 1:c, :]
  q_inter = (qq * jnp.exp(b)).astype(BF16)
  k_end = kk * jnp.exp(b_last - b)
  state_decay = jnp.exp(b_last)
  yield 1

  levels = []
  m = c // 2
  while m >= HG_BLOCK:
    q_rows, k_rows = [], []
    zero_half = jnp.zeros((m, HG_WIDTH), BF16)
    for i in range(c // (2 * m)):
      lo = i * 2 * m
      b_ref_row = b[lo + m - 1:lo + m, :]
      k_rows += [(kk[lo:lo + m] * jnp.exp(b_ref_row - b[lo:lo + m])).astype(BF16), zero_half]
      q_rows += [zero_half, (qq[lo + m:lo + 2 * m] * jnp.exp(b[lo + m:lo + 2 * m] - b_ref_row)).astype(BF16)]
    shift = m.bit_length() - 1
    mask = ((t_idx >> (shift + 1)) == (s_idx >> (shift + 1))) & (((t_idx >> shift) & 1) == 1) & (
        ((s_idx >> shift) & 1) == 0)
    levels.append((jnp.concatenate(q_rows, axis=0), jnp.concatenate(k_rows, axis=0), mask))
    yield 1
    m //= 2
  blk = min(HG_BLOCK, c)
  q_rows, k_rows = [], []
  for i in range(c // blk):
    lo = i * blk
    b_loc = b[lo:lo + blk] if i == 0 else b[lo:lo + blk] - b[lo - 1:lo, :]
    q_rows.append(qq[lo:lo + blk] * jnp.exp(b_loc))
    k_rows.append(kk[lo:lo + blk] * jnp.exp(-b_loc))
  shift = blk.bit_length() - 1
  levels.append((jnp.concatenate(q_rows, axis=0).astype(BF16), padr(jnp.concatenate(k_rows, axis=0)).astype(BF16),
                 ((t_idx >> shift) == (s_idx >> shift)) & causal))

  yield 2
  head_slices = [slice(h * HG_DK, (h + 1) * HG_DK) for h in range(HG_HEADS)]
  v_bf = padr(cols(OFF_IV, HG_WIDTH)).astype(BF16)
  k_end_bf = padr(k_end).astype(BF16)
  att_bf = []
  for sl in head_slices:
    att = jnp.zeros((c, cs), F32)
    for ql, kl, mk in levels:
      att = jnp.where(mk, _dot_nt(ql[:, sl], kl[:, sl]), att)
    att_bf.append(att.astype(BF16))
    yield 1
  s0 = [hg_ref[h] for h in range(HG_HEADS)]
  o_heads = [_dot(att_bf[h], v_bf[:, sl]) + _dot(q_inter[:, sl], s0[h].astype(BF16))
             for h, sl in enumerate(head_slices)]
  for h, sl in enumerate(head_slices):
    hg_ref[h] = s0[h] * _col_matrix(state_decay[:, sl], HG_DV) + _dot_tn(k_end_bf[:, sl], v_bf[:, sl])
  o_parts = [o_h * lax.rsqrt(jnp.mean(o_h * o_h, axis=-1, keepdims=True) + EPS) for o_h in o_heads]
  o = jnp.concatenate(o_parts, axis=1) * p.hg_norm * _silu(cols(OFF_OG, HG_WIDTH))
  return jnp.concatenate([y_ssd, o], axis=1)


def _load_mix_params(refs):
  return MixParams(*[r[...] for r in refs])


def _drive(gen, between=lambda n: None):
  while True:
    try:
      between(next(gen))
    except StopIteration as done:
      return done.value


def _drive_round_robin(gens):
  results = [None] * len(gens)
  active = list(range(len(gens)))
  while active:
    for idx in list(active):
      try:
        next(gens[idx])
      except StopIteration as done:
        results[idx] = done.value
        active.remove(idx)
  return results


def _swiglu(h, wup_ref, wdn_ref):
  acc = jnp.zeros((h.shape[0], D_MODEL), F32)
  for ci in range(N_FF_CHUNKS):
    lo = ci * FF_CHUNK
    g = _dot(h, wup_ref[:, lo:lo + FF_CHUNK])
    u = _dot(h, wup_ref[:, D_FF + lo:D_FF + lo + FF_CHUNK])
    acc = acc + _dot((_silu(g) * u).astype(BF16), wdn_ref[lo:lo + FF_CHUNK, :])
  return acc


def _ffn1_kernel(x_ref, nw_ref, wup_ref, wdn_ref, o_ref):
  x = x_ref[...]
  h = _rmsnorm(x, nw_ref[...]).astype(BF16)
  o_ref[...] = x + 0.5 * _swiglu(h, wup_ref, wdn_ref)


def _mix_prompt_kernel(x_ref, nmix_ref, wall_ref, *rest, chunks_per_seq):
  prm_refs = rest[:len(MixParams._fields)]
  mixed_ref, conv_ref, ssm_ref, hg_ref, xb_ref, proj_a, proj_b = rest[len(MixParams._fields):]
  i = pl.program_id(0)
  t_cur = lax.rem(jnp.maximum(i - 1, 0), chunks_per_seq)

  @pl.when(i == 0)
  def _():
    proj_b[...] = jnp.zeros(proj_b.shape, F32)

  @pl.when(t_cur == 0)
  def _():
    xb_ref[0:SUBLANES, :] = jnp.zeros((SUBLANES, CONV_DIM), F32)
    ssm_ref[...] = jnp.zeros(ssm_ref.shape, F32)
    hg_ref[...] = jnp.zeros(hg_ref.shape, F32)

  def step(cur_ref, next_ref):
    h = _rmsnorm(x_ref[0], nmix_ref[...]).astype(BF16)
    pending = list(range(0, PROJ_W, PROJ_BLOCK))

    def project_blocks(n):
      for _ in range(min(n, len(pending))):
        lo = pending.pop(0)
        hi = min(lo + PROJ_BLOCK, PROJ_W)
        next_ref[:, lo:hi] = _dot(h, wall_ref[:, lo:hi])

    mixed = _drive(_mix_chunk(cur_ref, c=CHUNK, valid=CHUNK, p=_load_mix_params(prm_refs), xb_ref=xb_ref,
                              ssm_ref=ssm_ref.at[0], hg_ref=hg_ref.at[0]), project_blocks)
    project_blocks(len(pending))
    mixed_ref[0] = mixed.astype(BF16)
    tail = xb_ref[CHUNK:CHUNK + SUBLANES, :]
    xb_ref[0:SUBLANES, :] = tail

  parity = lax.rem(i, 2)
  pl.when(parity == 0)(functools.partial(step, proj_b, proj_a))
  pl.when(parity == 1)(functools.partial(step, proj_a, proj_b))

  @pl.when((t_cur == chunks_per_seq - 1) & (i > 0))
  def _():
    conv_ref[0] = xb_ref[SUBLANES + CHUNK - (CONV_W - 1):SUBLANES + CHUNK, :]


def _proj_sample_kernel(x_ref, nmix_ref, wall_ref, o_ref):
  h = _rmsnorm(x_ref[...], nmix_ref[...]).astype(BF16)
  o_ref[...] = _dot(h, wall_ref[...])


def _rec_sample_kernel(proj_ref, conv_in_ref, ssm_in_ref, hg_in_ref, *rest, valid):
  prm_refs = rest[:len(MixParams._fields)]
  mixed_ref, conv_ref, ssm_ref, hg_ref, xb_ref = rest[len(MixParams._fields):]
  n_seq = proj_ref.shape[0]
  prm = _load_mix_params(prm_refs)
  ssm_ref[...] = ssm_in_ref[...]
  hg_ref[...] = hg_in_ref[...]
  gens = []
  for s in range(n_seq):
    xb_ref[s, 0:SUBLANES, :] = jnp.zeros((SUBLANES, CONV_DIM), F32)
    xb_ref[s, SUBLANES - (CONV_W - 1):SUBLANES, :] = conv_in_ref[s]
    gens.append(_mix_chunk(proj_ref.at[s], c=SAMPLE_ROWS, valid=valid, p=prm, xb_ref=xb_ref.at[s],
                           ssm_ref=ssm_ref.at[s], hg_ref=hg_ref.at[s]))
  for s, mixed in enumerate(_drive_round_robin(gens)):
    mixed_ref[s] = mixed.astype(BF16)
    conv_ref[s] = xb_ref[s, SUBLANES + valid - (CONV_W - 1):SUBLANES + valid, :]


def _tail_kernel(x1_ref, mixed_ref, p_ref, wout_ref, nf2_ref, wup_ref, wdn_ref, nple_ref, wgate_ref, wproj_ref,
                 ppost_ref, nfin_ref, o_ref):
  x2 = x1_ref[...] + _dot(mixed_ref[...], wout_ref[...])
  h = _rmsnorm(x2, nf2_ref[...]).astype(BF16)
  x3 = x2 + 0.5 * _swiglu(h, wup_ref, wdn_ref)
  gate = _sigmoid(_dot(_rmsnorm(x3, nple_ref[...]).astype(BF16), wgate_ref[...]))
  e = _rmsnorm(_dot(p_ref[...].astype(BF16), wproj_ref[...]), ppost_ref[...])
  x4 = x3 + gate * e
  o_ref[...] = _rmsnorm(x4, nfin_ref[...])


def _resident(shape):
  nd = len(shape)
  return pl.BlockSpec(shape, lambda *_: (0,) * nd, pipeline_mode=pl.Buffered(1))


def _params(semantics, flags=None):
  return pltpu.CompilerParams(dimension_semantics=semantics, vmem_limit_bytes=VMEM_LIMIT_BYTES, flags=flags)


def _ffn1(x, nw, wup, wdn):
  n = x.shape[0]
  tile = min(FFN_TILE, n)
  rows = pl.BlockSpec((tile, D_MODEL), lambda i: (i, 0))
  return pl.pallas_call(
      _ffn1_kernel,
      grid=(n // tile,),
      in_specs=[rows, _resident(nw.shape), _resident(wup.shape), _resident(wdn.shape)],
      out_specs=rows,
      out_shape=jax.ShapeDtypeStruct((n, D_MODEL), F32),
      compiler_params=_params(("arbitrary",)),
      name="ffn1",
  )(x, nw, wup, wdn)


def _mix_prompt(x1, nmix, wall, prm):
  bsz, seq, _ = x1.shape
  nt = seq // CHUNK
  n_chunks = bsz * nt
  out_shape = (
      jax.ShapeDtypeStruct((bsz, seq, D_MIX), BF16),
      jax.ShapeDtypeStruct((bsz, CONV_W - 1, CONV_DIM), F32),
      jax.ShapeDtypeStruct((bsz, SSD_HEADS, SSD_HEADDIM, SSD_STATE), F32),
      jax.ShapeDtypeStruct((bsz, HG_HEADS, HG_DK, HG_DV), F32),
  )
  proj_chunk = lambda i: jnp.minimum(i, n_chunks - 1)
  mix_chunk = lambda i: jnp.maximum(i - 1, 0)
  out_specs = (
      pl.BlockSpec((1, CHUNK, D_MIX), lambda i: (mix_chunk(i) // nt, mix_chunk(i) % nt, 0)),
      pl.BlockSpec((1, CONV_W - 1, CONV_DIM), lambda i: (mix_chunk(i) // nt, 0, 0)),
      pl.BlockSpec((1, SSD_HEADS, SSD_HEADDIM, SSD_STATE), lambda i: (mix_chunk(i) // nt, 0, 0, 0)),
      pl.BlockSpec((1, HG_HEADS, HG_DK, HG_DV), lambda i: (mix_chunk(i) // nt, 0, 0, 0)),
  )
  in_specs = [pl.BlockSpec((1, CHUNK, D_MODEL), lambda i: (proj_chunk(i) // nt, proj_chunk(i) % nt, 0)),
              _resident(nmix.shape), _resident(wall.shape)] + [_resident(a.shape) for a in prm]
  return pl.pallas_call(
      functools.partial(_mix_prompt_kernel, chunks_per_seq=nt),
      grid=(n_chunks + 1,),
      in_specs=in_specs,
      out_specs=out_specs,
      out_shape=out_shape,
      scratch_shapes=[pltpu.VMEM((SUBLANES + CHUNK, CONV_DIM), F32), pltpu.VMEM((CHUNK, PROJ_W), F32),
                      pltpu.VMEM((CHUNK, PROJ_W), F32)],
      compiler_params=_params(("arbitrary",)),
      name="mix_prompt",
  )(x1, nmix, wall, *prm)


def _proj_sample(x1p, nmix, wall):
  n = x1p.shape[0]
  tile = 256
  return pl.pallas_call(
      _proj_sample_kernel,
      grid=(n // tile,),
      in_specs=[pl.BlockSpec((tile, D_MODEL), lambda i: (i, 0)), _resident(nmix.shape), _resident(wall.shape)],
      out_specs=pl.BlockSpec((tile, PROJ_W), lambda i: (i, 0)),
      out_shape=jax.ShapeDtypeStruct((n, PROJ_W), F32),
      compiler_params=_params(("arbitrary",)),
      name="proj_sample",
  )(x1p, nmix, wall)


def _rec_sample(proj, conv0, ssm0, hg0, prm, valid):
  bsz = proj.shape[0]
  out_shape = (
      jax.ShapeDtypeStruct((bsz, SAMPLE_ROWS, D_MIX), BF16),
      jax.ShapeDtypeStruct((bsz, CONV_W - 1, CONV_DIM), F32),
      jax.ShapeDtypeStruct((bsz, SSD_HEADS, SSD_HEADDIM, SSD_STATE), F32),
      jax.ShapeDtypeStruct((bsz, HG_HEADS, HG_DK, HG_DV), F32),
  )
  g = SAMPLE_SEQS_PER_STEP
  state_specs = [
      pl.BlockSpec((g, CONV_W - 1, CONV_DIM), lambda b: (b, 0, 0)),
      pl.BlockSpec((g, SSD_HEADS, SSD_HEADDIM, SSD_STATE), lambda b: (b, 0, 0, 0)),
      pl.BlockSpec((g, HG_HEADS, HG_DK, HG_DV), lambda b: (b, 0, 0, 0)),
  ]
  in_specs = [pl.BlockSpec((g, SAMPLE_ROWS, PROJ_W), lambda b: (b, 0, 0))] + state_specs + [
      _resident(a.shape) for a in prm]
  out_specs = tuple([pl.BlockSpec((g, SAMPLE_ROWS, D_MIX), lambda b: (b, 0, 0))] + state_specs)
  return pl.pallas_call(
      functools.partial(_rec_sample_kernel, valid=valid),
      grid=(bsz // g,),
      in_specs=in_specs,
      out_specs=out_specs,
      out_shape=out_shape,
      scratch_shapes=[pltpu.VMEM((g, 2 * SUBLANES, CONV_DIM), F32)],
      compiler_params=_params(("arbitrary",)),
      name="rec_sample",
  )(proj, conv0, ssm0, hg0, *prm)


def _tail(x1, mixed, p, wout, nf2, wup, wdn, nple, wgate, wproj, ppost, nfin):
  n = x1.shape[0]
  tile = min(FFN_TILE, n)
  row = lambda w: pl.BlockSpec((tile, w), lambda i: (i, 0))
  weights = (wout, nf2, wup, wdn, nple, wgate, wproj, ppost, nfin)
  return pl.pallas_call(
      _tail_kernel,
      grid=(n // tile,),
      in_specs=[row(D_MODEL), row(D_MIX), row(PLE_DIM)] + [_resident(w.shape) for w in weights],
      out_specs=row(D_MODEL),
      out_shape=jax.ShapeDtypeStruct((n, D_MODEL), F32),
      compiler_params=_params(("arbitrary",)),
      name="tail",
  )(x1, mixed, p, *weights)


def kernel(x_prompt, x_sample, state_conv, state_ssm, state_hgrn, p_prompt, p_sample, norm_ffn1, w_ffn1_up, w_ffn1_down, norm_mix, w_in, conv_w, conv_b, dt_bias, a_log, d_skip, ssd_norm, hg_lb_logits, hg_norm, w_out, norm_ffn2, w_ffn2_up, w_ffn2_down, norm_ple, w_ple_gate, w_ple_proj, ple_post_norm, norm_final):
  bp, seq, _ = x_prompt.shape
  bs, dec_seq, _ = x_sample.shape
  row = lambda v: v.reshape(1, -1).astype(F32)
  per_head = lambda v: jnp.repeat(v.astype(F32), SSD_HEADDIM).reshape(1, SSD_DINNER)
  head_lanes = lambda v: jnp.pad(v.astype(F32), (0, LANES - SSD_HEADS)).reshape(1, LANES)

  w1u, w1d = w_ffn1_up[0].astype(BF16), w_ffn1_down[0].astype(BF16)
  w2u, w2d = w_ffn2_up[0].astype(BF16), w_ffn2_down[0].astype(BF16)
  wi = w_in[0].astype(BF16)
  o_dt = SSD_DINNER + CONV_DIM
  o_q = o_dt + SSD_HEADS
  wall = jnp.concatenate([wi[:, :o_dt], wi[:, o_q:], jnp.pad(wi[:, o_dt:o_q], ((0, 0), (0, LANES - SSD_HEADS)))],
                         axis=1)
  prm = MixParams(
      conv_w=conv_w[0].astype(F32), conv_b=row(conv_b[0]), dtb_c=head_lanes(dt_bias[0]),
      alog_c=head_lanes(a_log[0]), dskip_e=per_head(d_skip[0]), ssd_norm=row(ssd_norm[0]),
      lb_logits=hg_lb_logits.astype(F32), hg_norm=row(hg_norm[0]))
  nmix = row(norm_mix[0])
  tail_w = (w_out[0].astype(BF16), row(norm_ffn2[0]), w2u, w2d, row(norm_ple[0]), w_ple_gate[0].astype(BF16),
            w_ple_proj[0].astype(BF16), row(ple_post_norm[0]), row(norm_final))

  xp = x_prompt.reshape(bp * seq, D_MODEL)
  x1p = _ffn1(xp, row(norm_ffn1[0]), w1u, w1d)
  mixed_p, conv_p, ssm_p, hg_p = _mix_prompt(x1p.reshape(bp, seq, D_MODEL), nmix, wall, prm)
  y_prompt = _tail(x1p, mixed_p.reshape(bp * seq, D_MIX), p_prompt[0].reshape(bp * seq, PLE_DIM), *tail_w)

  xs = x_sample.reshape(bs * dec_seq, D_MODEL)
  x1s = _ffn1(xs, row(norm_ffn1[0]), w1u, w1d)
  x1s_pad = jnp.pad(x1s.reshape(bs, dec_seq, D_MODEL), ((0, 0), (0, SAMPLE_ROWS - dec_seq), (0, 0)))
  proj_s = _proj_sample(x1s_pad.reshape(bs * SAMPLE_ROWS, D_MODEL), nmix, wall)
  mixed_s, conv_s, ssm_s, hg_s = _rec_sample(proj_s.reshape(bs, SAMPLE_ROWS, PROJ_W), state_conv[0], state_ssm[0],
                                             state_hgrn[0], prm, dec_seq)
  mixed_s = mixed_s[:, :dec_seq].reshape(bs * dec_seq, D_MIX)
  y_sample = _tail(x1s, mixed_s, p_sample[0].reshape(bs * dec_seq, PLE_DIM), *tail_w)

  return (y_prompt.reshape(bp, seq, D_MODEL), y_sample.reshape(bs, dec_seq, D_MODEL), conv_p[None], ssm_p[None],
          hg_p[None], conv_s[None], ssm_s[None], hg_s[None])
```

```python
import functools
from typing import NamedTuple

import jax
import jax.numpy as jnp
from jax import lax
from jax.experimental import pallas as pl
from jax.experimental.pallas import tpu as pltpu

F32 = jnp.float32
BF16 = jnp.bfloat16

D_MODEL = 1024
D_FF = 2816
PLE_DIM = 256
EPS = 1e-6
SSD_HEADS = 16
SSD_HEADDIM = 64
SSD_DINNER = SSD_HEADS * SSD_HEADDIM
SSD_STATE = 128
SSD_GROUPS = 2
SSD_HEADS_PER_GROUP = SSD_HEADS // SSD_GROUPS
SSD_GROUP_WIDTH = SSD_DINNER // SSD_GROUPS
CONV_W = 4
CONV_DIM = SSD_DINNER + 2 * SSD_GROUPS * SSD_STATE
HG_HEADS = 8
HG_DK = 128
HG_DV = 128
HG_WIDTH = HG_HEADS * HG_DV
HG_BLOCK = 16
D_MIX = SSD_DINNER + HG_WIDTH

LANES = 128
SUBLANES = 8
VMEM_LIMIT_BYTES = 56 * 1024 * 1024

CHUNK = 128
SAMPLE_ROWS = 8
SAMPLE_SEQS_PER_STEP = 4
FFN_TILE = 512
FF_CHUNK = 256
N_FF_CHUNKS = D_FF // FF_CHUNK

OFF_Z = 0
OFF_XBC = OFF_Z + SSD_DINNER
OFF_Q = OFF_XBC + CONV_DIM
OFF_FR = OFF_Q + HG_WIDTH
OFF_IV = OFF_FR + HG_WIDTH
OFF_OG = OFF_IV + HG_WIDTH
OFF_DT = OFF_OG + HG_WIDTH
PROJ_W = OFF_DT + LANES
PROJ_BLOCK = 256
MIX_PHASE_WEIGHT = 34


class MixParams(NamedTuple):
  conv_w: jax.Array
  conv_b: jax.Array
  dtb_c: jax.Array
  alog_c: jax.Array
  dskip_e: jax.Array
  ssd_norm: jax.Array
  lb_logits: jax.Array
  hg_norm: jax.Array


def _dot(a, b):
  return jnp.dot(a, b, preferred_element_type=F32)


def _dot_nt(a, b):
  return lax.dot_general(a, b, (((1,), (1,)), ((), ())), preferred_element_type=F32)


def _dot_tn(a, b):
  return lax.dot_general(a, b, (((0,), (0,)), ((), ())), preferred_element_type=F32)


NEG_LOG2_E = -1.4426950408889634


def _sigmoid(x):
  return 1.0 / (1.0 + jnp.exp2(x * NEG_LOG2_E))


def _silu(x):
  return x * _sigmoid(x)


def _softplus(x):
  return jnp.maximum(x, 0.0) + jnp.log1p(jnp.exp(-jnp.abs(x)))


def _rmsnorm(x, w):
  ms = jnp.mean(x * x, axis=-1, keepdims=True)
  return x * lax.rsqrt(ms + EPS) * w


def _cumsum_rows(x):
  c, n = x.shape
  g = c // SUBLANES
  x3 = x.reshape(g, SUBLANES, n)
  sub = lax.broadcasted_iota(jnp.int32, (1, SUBLANES, n), 1)
  s = 1
  while s < SUBLANES:
    x3 = x3 + jnp.where(sub >= s, pltpu.roll(x3, s, axis=1), 0.0)
    s *= 2
  if g > 1:
    tot = jnp.broadcast_to(x3[:, SUBLANES - 1:SUBLANES, :], (g, SUBLANES, n))
    offs = [jnp.zeros((1, SUBLANES, n), x.dtype)]
    for k in range(1, g):
      offs.append(offs[-1] + tot[k - 1:k])
    x3 = x3 + jnp.concatenate(offs, axis=0)
  return x3.reshape(c, n)


def _expand_heads(xc):
  c = xc.shape[0]
  first_head = lax.broadcasted_iota(jnp.int32, (1, LANES), 1) < SSD_HEADDIM
  parts = []
  for j in range(SSD_HEADS // 2):
    a = jnp.broadcast_to(xc[:, 2 * j:2 * j + 1], (c, LANES))
    b = jnp.broadcast_to(xc[:, 2 * j + 1:2 * j + 2], (c, LANES))
    parts.append(jnp.where(first_head, a, b))
  return jnp.concatenate(parts, axis=1)


def _col_matrix(row_vec, n_lanes):
  n = row_vec.shape[1]
  return jnp.transpose(jnp.broadcast_to(row_vec, (n_lanes, n)))


def _mix_chunk(proj_ref, *, c, valid, p, xb_ref, ssm_ref, hg_ref):
  cs = max(c, LANES)

  def padr(x):
    if x.shape[0] == cs:
      return x
    return jnp.concatenate([x, jnp.zeros((cs - x.shape[0], x.shape[1]), x.dtype)], axis=0)

  row_valid = None
  if valid < c:
    row_valid = lax.broadcasted_iota(jnp.int32, (c, 1), 0) < valid

  def mask_rows(x):
    return x if row_valid is None else jnp.where(row_valid, x, 0.0)

  cols = lambda off, width: proj_ref[:, off:off + width]

  xbc = cols(OFF_XBC, CONV_DIM)
  xb_ref[SUBLANES:SUBLANES + c, :] = xbc
  conv = p.conv_b + p.conv_w[CONV_W - 1:CONV_W] * xbc
  for j in range(CONV_W - 1):
    conv = conv + p.conv_w[j:j + 1] * xb_ref[SUBLANES - (CONV_W - 1) + j:SUBLANES - (CONV_W - 1) + j + c, :]
  yield 2
  conv = _silu(conv)
  yield 2
  xs = conv[:, :SSD_DINNER]
  bm = conv[:, SSD_DINNER:SSD_DINNER + SSD_GROUPS * SSD_STATE]
  cm = conv[:, SSD_DINNER + SSD_GROUPS * SSD_STATE:]

  head_lane = lax.broadcasted_iota(jnp.int32, (1, LANES), 1) < SSD_HEADS
  dt_c = mask_rows(jnp.where(head_lane, _softplus(cols(OFF_DT, LANES) + p.dtb_c), 0.0))
  acs_c = _cumsum_rows(dt_c * -jnp.exp(p.alog_c))
  last_c = acs_c[c - 1:c, :]
  dec_in = _expand_heads(jnp.exp(acs_c))
  xd_end = xs * _expand_heads(dt_c * jnp.exp(last_c - acs_c))
  yield 2
  acs_t = jnp.transpose(padr(acs_c))
  dt_t = jnp.transpose(padr(dt_c))
  chunk_decay_t = jnp.transpose(jnp.broadcast_to(jnp.exp(last_c), (LANES, LANES)))

  t_idx = lax.broadcasted_iota(jnp.int32, (c, cs), 0)
  s_idx = lax.broadcasted_iota(jnp.int32, (c, cs), 1)
  causal = s_idx <= t_idx
  lane = lax.broadcasted_iota(jnp.int32, (1, LANES), 1)
  first_head = lane < SSD_HEADDIM

  group_heads = [range(g * SSD_HEADS_PER_GROUP, (g + 1) * SSD_HEADS_PER_GROUP) for g in range(SSD_GROUPS)]
  bg_p = [padr(bm[:, g * SSD_STATE:(g + 1) * SSD_STATE]).astype(BF16) for g in range(SSD_GROUPS)]
  cg = [cm[:, g * SSD_STATE:(g + 1) * SSD_STATE].astype(BF16) for g in range(SSD_GROUPS)]
  h0 = [ssm_ref[heads.start:heads.stop].reshape(SSD_GROUP_WIDTH, SSD_STATE) for heads in group_heads]
  cb = [_dot_nt(cg[g], bg_p[g]) for g in range(SSD_GROUPS)]
  y_inter = [_dot_nt(cg[g], h0[g].astype(BF16)) for g in range(SSD_GROUPS)]
  xs_bf = padr(xs).astype(BF16)
  zero_bf = jnp.zeros((cs, LANES), BF16)
  y_intra = []
  for g in range(SSD_GROUPS):
    for j in range(SSD_HEADS_PER_GROUP // 2):
      h_first = g * SSD_HEADS_PER_GROUP + 2 * j
      scores = []
      for h in (h_first, h_first + 1):
        diff = acs_c[:, h:h + 1] - acs_t[h:h + 1, :]
        decay_dt = jnp.where(causal, jnp.exp(diff), 0.0) * dt_t[h:h + 1, :]
        scores.append((cb[g] * decay_dt).astype(BF16))
      pair = xs_bf[:, h_first * SSD_HEADDIM:(h_first + 2) * SSD_HEADDIM]
      rhs = jnp.concatenate([jnp.where(first_head, pair, zero_bf), jnp.where(first_head, zero_bf, pair)], axis=0)
      y_intra.append(_dot(jnp.concatenate(scores, axis=1), rhs))
      yield 1
  for g, heads in enumerate(group_heads):
    xd_g = padr(xd_end[:, g * SSD_GROUP_WIDTH:(g + 1) * SSD_GROUP_WIDTH]).astype(BF16)
    decay_cols = jnp.concatenate(
        [jnp.broadcast_to(chunk_decay_t[h:h + 1, :], (SSD_HEADDIM, SSD_STATE)) for h in heads], axis=0)
    h1 = h0[g] * decay_cols + _dot_tn(xd_g, bg_p[g])
    ssm_ref[heads.start:heads.stop] = h1.reshape(SSD_HEADS_PER_GROUP, SSD_HEADDIM, SSD_STATE)
  y = jnp.concatenate(y_intra, axis=1) + jnp.concatenate(y_inter, axis=1) * dec_in + p.dskip_e * xs
  yield 1
  yg = y * _silu(cols(OFF_Z, SSD_DINNER))
  yield 1
  y_norm = []
  for g in range(SSD_GROUPS):
    blk = yg[:, g * SSD_GROUP_WIDTH:(g + 1) * SSD_GROUP_WIDTH]
    y_norm.append(blk * lax.rsqrt(jnp.mean(blk * blk, axis=-1, keepdims=True) + EPS))
  y_ssd = jnp.concatenate(y_norm, axis=1) * p.ssd_norm

  lg = p.lb_logits
  lg_max = jnp.max(lg, axis=0, keepdims=True)
  lg_exp = jnp.exp(lg - lg_max)
  lb = lg_exp[0:1] / jnp.sum(lg_exp, axis=0, keepdims=True)
  f = lb + (1.0 - lb) * _sigmoid(cols(OFF_FR, HG_WIDTH))
  yield 1
  logf = mask_rows(jnp.log(f))
  kk = mask_rows(1.0 - f)
  qq = _silu(cols(OFF_Q, HG_WIDTH))
  yield 1
  b = _cumsum_rows(logf)
  yield 2
  b_last = b[c - 1:c, :]
  q_inter = (qq * jnp.exp(b)).astype(BF16)
  k_end = kk * jnp.exp(b_last - b)
  state_decay = jnp.exp(b_last)
  yield 1

  levels = []
  m = c // 2
  while m >= HG_BLOCK:
    q_rows, k_rows = [], []
    zero_half = jnp.zeros((m, HG_WIDTH), BF16)
    for i in range(c // (2 * m)):
      lo = i * 2 * m
      b_ref_row = b[lo + m - 1:lo + m, :]
      k_rows += [(kk[lo:lo + m] * jnp.exp(b_ref_row - b[lo:lo + m])).astype(BF16), zero_half]
      q_rows += [zero_half, (qq[lo + m:lo + 2 * m] * jnp.exp(b[lo + m:lo + 2 * m] - b_ref_row)).astype(BF16)]
    shift = m.bit_length() - 1
    mask = ((t_idx >> (shift + 1)) == (s_idx >> (shift + 1))) & (((t_idx >> shift) & 1) == 1) & (
        ((s_idx >> shift) & 1) == 0)
    levels.append((jnp.concatenate(q_rows, axis=0), jnp.concatenate(k_rows, axis=0), mask))
    yield 1
    m //= 2
  blk = min(HG_BLOCK, c)
  q_rows, k_rows = [], []
  for i in range(c // blk):
    lo = i * blk
    b_loc = b[lo:lo + blk] if i == 0 else b[lo:lo + blk] - b[lo - 1:lo, :]
    q_rows.append(qq[lo:lo + blk] * jnp.exp(b_loc))
    k_rows.append(kk[lo:lo + blk] * jnp.exp(-b_loc))
  shift = blk.bit_length() - 1
  levels.append((jnp.concatenate(q_rows, axis=0).astype(BF16), padr(jnp.concatenate(k_rows, axis=0)).astype(BF16),
                 ((t_idx >> shift) == (s_idx >> shift)) & causal))

  yield 2
  head_slices = [slice(h * HG_DK, (h + 1) * HG_DK) for h in range(HG_HEADS)]
  v_bf = padr(cols(OFF_IV, HG_WIDTH)).astype(BF16)
  k_end_bf = padr(k_end).astype(BF16)
  att_bf = []
  for sl in head_slices:
    att = jnp.zeros((c, cs), F32)
    for ql, kl, mk in levels:
      att = jnp.where(mk, _dot_nt(ql[:, sl], kl[:, sl]), att)
    att_bf.append(att.astype(BF16))
    yield 1
  s0 = [hg_ref[h] for h in range(HG_HEADS)]
  o_heads = [_dot(att_bf[h], v_bf[:, sl]) + _dot(q_inter[:, sl], s0[h].astype(BF16))
             for h, sl in enumerate(head_slices)]
  for h, sl in enumerate(head_slices):
    hg_ref[h] = s0[h] * _col_matrix(state_decay[:, sl], HG_DV) + _dot_tn(k_end_bf[:, sl], v_bf[:, sl])
  o_parts = [o_h * lax.rsqrt(jnp.mean(o_h * o_h, axis=-1, keepdims=True) + EPS) for o_h in o_heads]
  o = jnp.concatenate(o_parts, axis=1) * p.hg_norm * _silu(cols(OFF_OG, HG_WIDTH))
  return jnp.concatenate([y_ssd, o], axis=1)


def _load_mix_params(refs):
  return MixParams(*[r[...] for r in refs])


def _drive(gen, between=lambda n: None):
  while True:
    try:
      between(next(gen))
    except StopIteration as done:
      return done.value


def _drive_round_robin(gens):
  results = [None] * len(gens)
  active = list(range(len(gens)))
  while active:
    for idx in list(active):
      try:
        next(gens[idx])
      except StopIteration as done:
        results[idx] = done.value
        active.remove(idx)
  return results


def _swiglu(h, wup_ref, wdn_ref):
  acc = jnp.zeros((h.shape[0], D_MODEL), F32)
  for ci in range(N_FF_CHUNKS):
    lo = ci * FF_CHUNK
    g = _dot(h, wup_ref[:, lo:lo + FF_CHUNK])
    u = _dot(h, wup_ref[:, D_FF + lo:D_FF + lo + FF_CHUNK])
    acc = acc + _dot((_silu(g) * u).astype(BF16), wdn_ref[lo:lo + FF_CHUNK, :])
  return acc


def _first_step_select(first_ref, rest_ref):
  return jnp.where(pl.program_id(0) == 0, first_ref[...], rest_ref[...])


def _ffn1_kernel(xs_ref, xp_ref, nw_ref, wup_ref, wdn_ref, os_ref, op_ref):
  x = _first_step_select(xs_ref, xp_ref)
  h = _rmsnorm(x, nw_ref[...]).astype(BF16)
  op_ref[...] = x + 0.5 * _swiglu(h, wup_ref, wdn_ref)

  @pl.when(pl.program_id(0) == 0)
  def _():
    os_ref[...] = op_ref[...]


def _mix_prompt_kernel(x_ref, nmix_ref, wall_ref, *rest, chunks_per_seq):
  prm_refs = rest[:len(MixParams._fields)]
  mixed_ref, conv_ref, ssm_ref, hg_ref, xb_ref, proj_a, proj_b = rest[len(MixParams._fields):]
  i = pl.program_id(0)
  t_cur = lax.rem(jnp.maximum(i - 1, 0), chunks_per_seq)

  @pl.when(i == 0)
  def _():
    proj_b[...] = jnp.zeros(proj_b.shape, F32)

  @pl.when(t_cur == 0)
  def _():
    xb_ref[0:SUBLANES, :] = jnp.zeros((SUBLANES, CONV_DIM), F32)
    ssm_ref[...] = jnp.zeros(ssm_ref.shape, F32)
    hg_ref[...] = jnp.zeros(hg_ref.shape, F32)

  def step(cur_ref, next_ref):
    h = _rmsnorm(x_ref[0], nmix_ref[...]).astype(BF16)
    pending = list(range(0, PROJ_W, PROJ_BLOCK))

    n_blocks = len(pending)
    credit = [0.0]

    def project_blocks(n):
      credit[0] += n * n_blocks / MIX_PHASE_WEIGHT
      while pending and credit[0] >= 1.0:
        credit[0] -= 1.0
        lo = pending.pop(0)
        hi = min(lo + PROJ_BLOCK, PROJ_W)
        next_ref[:, lo:hi] = _dot(h, wall_ref[:, lo:hi])

    mixed = _drive(_mix_chunk(cur_ref, c=CHUNK, valid=CHUNK, p=_load_mix_params(prm_refs), xb_ref=xb_ref,
                              ssm_ref=ssm_ref.at[0], hg_ref=hg_ref.at[0]), project_blocks)
    project_blocks(MIX_PHASE_WEIGHT)
    mixed_ref[0] = mixed.astype(BF16)
    tail = xb_ref[CHUNK:CHUNK + SUBLANES, :]
    xb_ref[0:SUBLANES, :] = tail

  parity = lax.rem(i, 2)
  pl.when(parity == 0)(functools.partial(step, proj_b, proj_a))
  pl.when(parity == 1)(functools.partial(step, proj_a, proj_b))

  @pl.when((t_cur == chunks_per_seq - 1) & (i > 0))
  def _():
    conv_ref[0] = xb_ref[SUBLANES + CHUNK - (CONV_W - 1):SUBLANES + CHUNK, :]


def _proj_sample_kernel(x_ref, nmix_ref, wall_ref, o_ref):
  h = _rmsnorm(x_ref[...], nmix_ref[...]).astype(BF16)
  o_ref[...] = _dot(h, wall_ref[...])


def _rec_sample_kernel(proj_ref, conv_in_ref, ssm_in_ref, hg_in_ref, *rest, valid):
  prm_refs = rest[:len(MixParams._fields)]
  mixed_ref, conv_ref, ssm_ref, hg_ref, xb_ref = rest[len(MixParams._fields):]
  n_seq = proj_ref.shape[0]
  prm = _load_mix_params(prm_refs)
  ssm_ref[...] = ssm_in_ref[...]
  hg_ref[...] = hg_in_ref[...]
  gens = []
  for s in range(n_seq):
    xb_ref[s, 0:SUBLANES, :] = jnp.zeros((SUBLANES, CONV_DIM), F32)
    xb_ref[s, SUBLANES - (CONV_W - 1):SUBLANES, :] = conv_in_ref[s]
    gens.append(_mix_chunk(proj_ref.at[s], c=SAMPLE_ROWS, valid=valid, p=prm, xb_ref=xb_ref.at[s],
                           ssm_ref=ssm_ref.at[s], hg_ref=hg_ref.at[s]))
  for s, mixed in enumerate(_drive_round_robin(gens)):
    mixed_ref[s] = mixed.astype(BF16)
    conv_ref[s] = xb_ref[s, SUBLANES + valid - (CONV_W - 1):SUBLANES + valid, :]


def _tail_kernel(x1s_ref, x1p_ref, mixs_ref, mixp_ref, ps_ref, pp_ref, wout_ref, nf2_ref, wup_ref, wdn_ref, nple_ref,
                 wgate_ref, wproj_ref, ppost_ref, nfin_ref, os_ref, op_ref):
  x2 = _first_step_select(x1s_ref, x1p_ref) + _dot(_first_step_select(mixs_ref, mixp_ref), wout_ref[...])
  h = _rmsnorm(x2, nf2_ref[...]).astype(BF16)
  x3 = x2 + 0.5 * _swiglu(h, wup_ref, wdn_ref)
  gate = _sigmoid(_dot(_rmsnorm(x3, nple_ref[...]).astype(BF16), wgate_ref[...]))
  e = _rmsnorm(_dot(_first_step_select(ps_ref, pp_ref).astype(BF16), wproj_ref[...]), ppost_ref[...])
  x4 = x3 + gate * e
  op_ref[...] = _rmsnorm(x4, nfin_ref[...])

  @pl.when(pl.program_id(0) == 0)
  def _():
    os_ref[...] = op_ref[...]


def _resident(shape):
  nd = len(shape)
  return pl.BlockSpec(shape, lambda *_: (0,) * nd, pipeline_mode=pl.Buffered(1))


def _params(semantics, flags=None):
  return pltpu.CompilerParams(dimension_semantics=semantics, vmem_limit_bytes=VMEM_LIMIT_BYTES, flags=flags)


def _sample_then_prompt_specs(width):
  sample = pl.BlockSpec((FFN_TILE, width), lambda i: (0, 0))
  prompt = pl.BlockSpec((FFN_TILE, width), lambda i: (jnp.maximum(i - 1, 0), 0))
  return sample, prompt


def _ffn1(xs, xp, nw, wup, wdn):
  assert xs.shape[0] == FFN_TILE and xp.shape[0] % FFN_TILE == 0
  specs = _sample_then_prompt_specs(D_MODEL)
  return pl.pallas_call(
      _ffn1_kernel,
      grid=(xp.shape[0] // FFN_TILE + 1,),
      in_specs=[*specs, _resident(nw.shape), _resident(wup.shape), _resident(wdn.shape)],
      out_specs=specs,
      out_shape=(jax.ShapeDtypeStruct(xs.shape, F32), jax.ShapeDtypeStruct(xp.shape, F32)),
      compiler_params=_params(("arbitrary",)),
      name="ffn1",
  )(xs, xp, nw, wup, wdn)


def _mix_prompt(x1, nmix, wall, prm):
  bsz, seq, _ = x1.shape
  nt = seq // CHUNK
  n_chunks = bsz * nt
  out_shape = (
      jax.ShapeDtypeStruct((bsz, seq, D_MIX), BF16),
      jax.ShapeDtypeStruct((bsz, CONV_W - 1, CONV_DIM), F32),
      jax.ShapeDtypeStruct((bsz, SSD_HEADS, SSD_HEADDIM, SSD_STATE), F32),
      jax.ShapeDtypeStruct((bsz, HG_HEADS, HG_DK, HG_DV), F32),
  )
  proj_chunk = lambda i: jnp.minimum(i, n_chunks - 1)
  mix_chunk = lambda i: jnp.maximum(i - 1, 0)
  out_specs = (
      pl.BlockSpec((1, CHUNK, D_MIX), lambda i: (mix_chunk(i) // nt, mix_chunk(i) % nt, 0)),
      pl.BlockSpec((1, CONV_W - 1, CONV_DIM), lambda i: (mix_chunk(i) // nt, 0, 0)),
      pl.BlockSpec((1, SSD_HEADS, SSD_HEADDIM, SSD_STATE), lambda i: (mix_chunk(i) // nt, 0, 0, 0)),
      pl.BlockSpec((1, HG_HEADS, HG_DK, HG_DV), lambda i: (mix_chunk(i) // nt, 0, 0, 0)),
  )
  in_specs = [pl.BlockSpec((1, CHUNK, D_MODEL), lambda i: (proj_chunk(i) // nt, proj_chunk(i) % nt, 0)),
              _resident(nmix.shape), _resident(wall.shape)] + [_resident(a.shape) for a in prm]
  return pl.pallas_call(
      functools.partial(_mix_prompt_kernel, chunks_per_seq=nt),
      grid=(n_chunks + 1,),
      in_specs=in_specs,
      out_specs=out_specs,
      out_shape=out_shape,
      scratch_shapes=[pltpu.VMEM((SUBLANES + CHUNK, CONV_DIM), F32), pltpu.VMEM((CHUNK, PROJ_W), F32),
                      pltpu.VMEM((CHUNK, PROJ_W), F32)],
      compiler_params=_params(("arbitrary",)),
      name="mix_prompt",
  )(x1, nmix, wall, *prm)


def _proj_sample(x1p, nmix, wall):
  n = x1p.shape[0]
  tile = 256
  return pl.pallas_call(
      _proj_sample_kernel,
      grid=(n // tile,),
      in_specs=[pl.BlockSpec((tile, D_MODEL), lambda i: (i, 0)), _resident(nmix.shape), _resident(wall.shape)],
      out_specs=pl.BlockSpec((tile, PROJ_W), lambda i: (i, 0)),
      out_shape=jax.ShapeDtypeStruct((n, PROJ_W), F32),
      compiler_params=_params(("arbitrary",)),
      name="proj_sample",
  )(x1p, nmix, wall)


def _rec_sample(proj, conv0, ssm0, hg0, prm, valid):
  bsz = proj.shape[0]
  out_shape = (
      jax.ShapeDtypeStruct((bsz, SAMPLE_ROWS, D_MIX), BF16),
      jax.ShapeDtypeStruct((bsz, CONV_W - 1, CONV_DIM), F32),
      jax.ShapeDtypeStruct((bsz, SSD_HEADS, SSD_HEADDIM, SSD_STATE), F32),
      jax.ShapeDtypeStruct((bsz, HG_HEADS, HG_DK, HG_DV), F32),
  )
  g = SAMPLE_SEQS_PER_STEP
  state_specs = [
      pl.BlockSpec((g, CONV_W - 1, CONV_DIM), lambda b: (b, 0, 0)),
      pl.BlockSpec((g, SSD_HEADS, SSD_HEADDIM, SSD_STATE), lambda b: (b, 0, 0, 0)),
      pl.BlockSpec((g, HG_HEADS, HG_DK, HG_DV), lambda b: (b, 0, 0, 0)),
  ]
  in_specs = [pl.BlockSpec((g, SAMPLE_ROWS, PROJ_W), lambda b: (b, 0, 0))] + state_specs + [
      _resident(a.shape) for a in prm]
  out_specs = tuple([pl.BlockSpec((g, SAMPLE_ROWS, D_MIX), lambda b: (b, 0, 0))] + state_specs)
  return pl.pallas_call(
      functools.partial(_rec_sample_kernel, valid=valid),
      grid=(bsz // g,),
      in_specs=in_specs,
      out_specs=out_specs,
      out_shape=out_shape,
      scratch_shapes=[pltpu.VMEM((g, 2 * SUBLANES, CONV_DIM), F32)],
      compiler_params=_params(("arbitrary",)),
      name="rec_sample",
  )(proj, conv0, ssm0, hg0, *prm)


def _tail(x1s, x1p, mixs, mixp, ps, pp, wout, nf2, wup, wdn, nple, wgate, wproj, ppost, nfin):
  assert x1s.shape[0] == FFN_TILE and x1p.shape[0] % FFN_TILE == 0
  weights = (wout, nf2, wup, wdn, nple, wgate, wproj, ppost, nfin)
  out_specs = _sample_then_prompt_specs(D_MODEL)
  return pl.pallas_call(
      _tail_kernel,
      grid=(x1p.shape[0] // FFN_TILE + 1,),
      in_specs=[*out_specs, *_sample_then_prompt_specs(D_MIX), *_sample_then_prompt_specs(PLE_DIM)] + [
          _resident(w.shape) for w in weights],
      out_specs=out_specs,
      out_shape=(jax.ShapeDtypeStruct(x1s.shape, F32), jax.ShapeDtypeStruct(x1p.shape, F32)),
      compiler_params=_params(("arbitrary",)),
      name="tail",
  )(x1s, x1p, mixs, mixp, ps, pp, *weights)


def kernel(x_prompt, x_sample, state_conv, state_ssm, state_hgrn, p_prompt, p_sample, norm_ffn1, w_ffn1_up, w_ffn1_down, norm_mix, w_in, conv_w, conv_b, dt_bias, a_log, d_skip, ssd_norm, hg_lb_logits, hg_norm, w_out, norm_ffn2, w_ffn2_up, w_ffn2_down, norm_ple, w_ple_gate, w_ple_proj, ple_post_norm, norm_final):
  bp, seq, _ = x_prompt.shape
  bs, dec_seq, _ = x_sample.shape
  row = lambda v: v.reshape(1, -1).astype(F32)
  per_head = lambda v: jnp.repeat(v.astype(F32), SSD_HEADDIM).reshape(1, SSD_DINNER)
  head_lanes = lambda v: jnp.pad(v.astype(F32), (0, LANES - SSD_HEADS)).reshape(1, LANES)

  w1u, w1d = w_ffn1_up[0].astype(BF16), w_ffn1_down[0].astype(BF16)
  w2u, w2d = w_ffn2_up[0].astype(BF16), w_ffn2_down[0].astype(BF16)
  wi = w_in[0]
  o_dt = SSD_DINNER + CONV_DIM
  o_q = o_dt + SSD_HEADS
  wall = jnp.concatenate([wi[:, :o_dt], wi[:, o_q:], jnp.pad(wi[:, o_dt:o_q], ((0, 0), (0, LANES - SSD_HEADS)))],
                         axis=1).astype(BF16)
  prm = MixParams(
      conv_w=conv_w[0].astype(F32), conv_b=row(conv_b[0]), dtb_c=head_lanes(dt_bias[0]),
      alog_c=head_lanes(a_log[0]), dskip_e=per_head(d_skip[0]), ssd_norm=row(ssd_norm[0]),
      lb_logits=hg_lb_logits.astype(F32), hg_norm=row(hg_norm[0]))
  nmix = row(norm_mix[0])
  tail_w = (w_out[0].astype(BF16), row(norm_ffn2[0]), w2u, w2d, row(norm_ple[0]), w_ple_gate[0].astype(BF16),
            w_ple_proj[0].astype(BF16), row(ple_post_norm[0]), row(norm_final))

  xp = x_prompt.reshape(bp * seq, D_MODEL)
  xs = x_sample.reshape(bs * dec_seq, D_MODEL)
  x1s, x1p = _ffn1(xs, xp, row(norm_ffn1[0]), w1u, w1d)

  mixed_p, conv_p, ssm_p, hg_p = _mix_prompt(x1p.reshape(bp, seq, D_MODEL), nmix, wall, prm)
  x1s_pad = jnp.pad(x1s.reshape(bs, dec_seq, D_MODEL), ((0, 0), (0, SAMPLE_ROWS - dec_seq), (0, 0)))
  proj_s = _proj_sample(x1s_pad.reshape(bs * SAMPLE_ROWS, D_MODEL), nmix, wall)
  mixed_s, conv_s, ssm_s, hg_s = _rec_sample(proj_s.reshape(bs, SAMPLE_ROWS, PROJ_W), state_conv[0], state_ssm[0],
                                             state_hgrn[0], prm, dec_seq)
  mixed_s = mixed_s[:, :dec_seq].reshape(bs * dec_seq, D_MIX)

  y_sample, y_prompt = _tail(x1s, x1p, mixed_s, mixed_p.reshape(bp * seq, D_MIX),
                             p_sample[0].reshape(bs * dec_seq, PLE_DIM), p_prompt[0].reshape(bp * seq, PLE_DIM), *tail_w)

  return (y_prompt.reshape(bp, seq, D_MODEL), y_sample.reshape(bs, dec_seq, D_MODEL), conv_p[None], ssm_p[None],
          hg_p[None], conv_s[None], ssm_s[None], hg_s[None])
```

```python
import functools
from typing import NamedTuple

import jax
import jax.numpy as jnp
from jax import lax
from jax.experimental import pallas as pl
from jax.experimental.pallas import tpu as pltpu

F32 = jnp.float32
BF16 = jnp.bfloat16

D_MODEL = 1024
D_FF = 2816
PLE_DIM = 256
EPS = 1e-6
SSD_HEADS = 16
SSD_HEADDIM = 64
SSD_DINNER = SSD_HEADS * SSD_HEADDIM
SSD_STATE = 128
SSD_GROUPS = 2
SSD_HEADS_PER_GROUP = SSD_HEADS // SSD_GROUPS
SSD_GROUP_WIDTH = SSD_DINNER // SSD_GROUPS
CONV_W = 4
CONV_DIM = SSD_DINNER + 2 * SSD_GROUPS * SSD_STATE
HG_HEADS = 8
HG_DK = 128
HG_DV = 128
HG_WIDTH = HG_HEADS * HG_DV
HG_BLOCK = 16
D_MIX = SSD_DINNER + HG_WIDTH

LANES = 128
SUBLANES = 8
VMEM_LIMIT_BYTES = 56 * 1024 * 1024

CHUNK = 128
SAMPLE_ROWS = 8
SAMPLE_SEQS_PER_STEP = 4
FFN_TILE = 512
FF_CHUNK = 256
N_FF_CHUNKS = D_FF // FF_CHUNK

OFF_Z = 0
OFF_XBC = OFF_Z + SSD_DINNER
OFF_Q = OFF_XBC + CONV_DIM
OFF_FR = OFF_Q + HG_WIDTH
OFF_IV = OFF_FR + HG_WIDTH
OFF_OG = OFF_IV + HG_WIDTH
OFF_DT = OFF_OG + HG_WIDTH
PROJ_W = OFF_DT + LANES
PROJ_BLOCK = 256
MIX_PHASE_WEIGHT = 34


class MixParams(NamedTuple):
  conv_w: jax.Array
  conv_b: jax.Array
  dtb_c: jax.Array
  alog_c: jax.Array
  dskip_e: jax.Array
  ssd_norm: jax.Array
  lb_logits: jax.Array
  hg_norm: jax.Array


def _dot(a, b):
  return jnp.dot(a, b, preferred_element_type=F32)


def _dot_nt(a, b):
  return lax.dot_general(a, b, (((1,), (1,)), ((), ())), preferred_element_type=F32)


def _dot_tn(a, b):
  return lax.dot_general(a, b, (((0,), (0,)), ((), ())), preferred_element_type=F32)


NEG_LOG2_E = -1.4426950408889634


def _sigmoid(x):
  return 1.0 / (1.0 + jnp.exp2(x * NEG_LOG2_E))


def _silu(x):
  return x * _sigmoid(x)


def _softplus(x):
  return jnp.maximum(x, 0.0) + jnp.log1p(jnp.exp(-jnp.abs(x)))


def _rmsnorm(x, w):
  ms = jnp.mean(x * x, axis=-1, keepdims=True)
  return x * lax.rsqrt(ms + EPS) * w


def _cumsum_rows(x):
  c, n = x.shape
  g = c // SUBLANES
  x3 = x.reshape(g, SUBLANES, n)
  sub = lax.broadcasted_iota(jnp.int32, (1, SUBLANES, n), 1)
  s = 1
  while s < SUBLANES:
    x3 = x3 + jnp.where(sub >= s, pltpu.roll(x3, s, axis=1), 0.0)
    s *= 2
  if g > 1:
    tot = jnp.broadcast_to(x3[:, SUBLANES - 1:SUBLANES, :], (g, SUBLANES, n))
    offs = [jnp.zeros((1, SUBLANES, n), x.dtype)]
    for k in range(1, g):
      offs.append(offs[-1] + tot[k - 1:k])
    x3 = x3 + jnp.concatenate(offs, axis=0)
  return x3.reshape(c, n)


def _expand_heads(xc):
  c = xc.shape[0]
  first_head = lax.broadcasted_iota(jnp.int32, (1, LANES), 1) < SSD_HEADDIM
  parts = []
  for j in range(SSD_HEADS // 2):
    a = jnp.broadcast_to(xc[:, 2 * j:2 * j + 1], (c, LANES))
    b = jnp.broadcast_to(xc[:, 2 * j + 1:2 * j + 2], (c, LANES))
    parts.append(jnp.where(first_head, a, b))
  return jnp.concatenate(parts, axis=1)


def _col_matrix(row_vec, n_lanes):
  n = row_vec.shape[1]
  return jnp.transpose(jnp.broadcast_to(row_vec, (n_lanes, n)))


def _mix_chunk(proj_ref, *, c, valid, p, xb_ref, ssm_ref, hg_ref):
  cs = max(c, LANES)

  def padr(x):
    if x.shape[0] == cs:
      return x
    return jnp.concatenate([x, jnp.zeros((cs - x.shape[0], x.shape[1]), x.dtype)], axis=0)

  row_valid = None
  if valid < c:
    row_valid = lax.broadcasted_iota(jnp.int32, (c, 1), 0) < valid

  def mask_rows(x):
    return x if row_valid is None else jnp.where(row_valid, x, 0.0)

  cols = lambda off, width: proj_ref[:, off:off + width]

  xbc = cols(OFF_XBC, CONV_DIM)
  xb_ref[SUBLANES:SUBLANES + c, :] = xbc
  conv = p.conv_b + p.conv_w[CONV_W - 1:CONV_W] * xbc
  for j in range(CONV_W - 1):
    conv = conv + p.conv_w[j:j + 1] * xb_ref[SUBLANES - (CONV_W - 1) + j:SUBLANES - (CONV_W - 1) + j + c, :]
  yield 2
  conv = _silu(conv)
  yield 2
  xs = conv[:, :SSD_DINNER]
  bm = conv[:, SSD_DINNER:SSD_DINNER + SSD_GROUPS * SSD_STATE]
  cm = conv[:, SSD_DINNER + SSD_GROUPS * SSD_STATE:]

  head_lane = lax.broadcasted_iota(jnp.int32, (1, LANES), 1) < SSD_HEADS
  dt_c = mask_rows(jnp.where(head_lane, _softplus(cols(OFF_DT, LANES) + p.dtb_c), 0.0))
  acs_c = _cumsum_rows(dt_c * -jnp.exp(p.alog_c))
  last_c = acs_c[c - 1:c, :]
  dec_in = _expand_heads(jnp.exp(acs_c))
  xd_end = xs * _expand_heads(dt_c * jnp.exp(last_c - acs_c))
  yield 2
  acs_t = jnp.transpose(padr(acs_c))
  dt_t = jnp.transpose(padr(dt_c))
  chunk_decay_t = jnp.transpose(jnp.broadcast_to(jnp.exp(last_c), (LANES, LANES)))

  t_idx = lax.broadcasted_iota(jnp.int32, (c, cs), 0)
  s_idx = lax.broadcasted_iota(jnp.int32, (c, cs), 1)
  causal = s_idx <= t_idx
  lane = lax.broadcasted_iota(jnp.int32, (1, LANES), 1)
  first_head = lane < SSD_HEADDIM

  group_heads = [range(g * SSD_HEADS_PER_GROUP, (g + 1) * SSD_HEADS_PER_GROUP) for g in range(SSD_GROUPS)]
  bg_p = [padr(bm[:, g * SSD_STATE:(g + 1) * SSD_STATE]).astype(BF16) for g in range(SSD_GROUPS)]
  cg = [cm[:, g * SSD_STATE:(g + 1) * SSD_STATE].astype(BF16) for g in range(SSD_GROUPS)]
  h0 = [ssm_ref[heads.start:heads.stop].reshape(SSD_GROUP_WIDTH, SSD_STATE) for heads in group_heads]
  cb = [_dot_nt(cg[g], bg_p[g]) for g in range(SSD_GROUPS)]
  y_inter = [_dot_nt(cg[g], h0[g].astype(BF16)) for g in range(SSD_GROUPS)]
  xs_bf = padr(xs).astype(BF16)
  zero_bf = jnp.zeros((cs, LANES), BF16)
  y_intra = []
  for g in range(SSD_GROUPS):
    for j in range(SSD_HEADS_PER_GROUP // 2):
      h_first = g * SSD_HEADS_PER_GROUP + 2 * j
      scores = []
      for h in (h_first, h_first + 1):
        diff = acs_c[:, h:h + 1] - acs_t[h:h + 1, :]
        decay_dt = jnp.where(causal, jnp.exp(diff), 0.0) * dt_t[h:h + 1, :]
        scores.append((cb[g] * decay_dt).astype(BF16))
      pair = xs_bf[:, h_first * SSD_HEADDIM:(h_first + 2) * SSD_HEADDIM]
      rhs = jnp.concatenate([jnp.where(first_head, pair, zero_bf), jnp.where(first_head, zero_bf, pair)], axis=0)
      y_intra.append(_dot(jnp.concatenate(scores, axis=1), rhs))
      yield 1
  for g, heads in enumerate(group_heads):
    xd_g = padr(xd_end[:, g * SSD_GROUP_WIDTH:(g + 1) * SSD_GROUP_WIDTH]).astype(BF16)
    decay_cols = jnp.concatenate(
        [jnp.broadcast_to(chunk_decay_t[h:h + 1, :], (SSD_HEADDIM, SSD_STATE)) for h in heads], axis=0)
    h1 = h0[g] * decay_cols + _dot_tn(xd_g, bg_p[g])
    ssm_ref[heads.start:heads.stop] = h1.reshape(SSD_HEADS_PER_GROUP, SSD_HEADDIM, SSD_STATE)
  y = jnp.concatenate(y_intra, axis=1) + jnp.concatenate(y_inter, axis=1) * dec_in + p.dskip_e * xs
  yield 1
  yg = y * _silu(cols(OFF_Z, SSD_DINNER))
  yield 1
  y_norm = []
  for g in range(SSD_GROUPS):
    blk = yg[:, g * SSD_GROUP_WIDTH:(g + 1) * SSD_GROUP_WIDTH]
    y_norm.append(blk * lax.rsqrt(jnp.mean(blk * blk, axis=-1, keepdims=True) + EPS))
  y_ssd = jnp.concatenate(y_norm, axis=1) * p.ssd_norm

  lg = p.lb_logits
  lg_max = jnp.max(lg, axis=0, keepdims=True)
  lg_exp = jnp.exp(lg - lg_max)
  lb = lg_exp[0:1] / jnp.sum(lg_exp, axis=0, keepdims=True)
  f = lb + (1.0 - lb) * _sigmoid(cols(OFF_FR, HG_WIDTH))
  yield 1
  logf = mask_rows(jnp.log(f))
  kk = mask_rows(1.0 - f)
  qq = _silu(cols(OFF_Q, HG_WIDTH))
  yield 1
  b = _cumsum_rows(logf)
  yield 2
  b_last = b[c - 1:c, :]
  q_inter = (qq * jnp.exp(b)).astype(BF16)
  k_end = kk * jnp.exp(b_last - b)
  state_decay = jnp.exp(b_last)
  yield 1

  levels = []
  m = c // 2
  while m >= HG_BLOCK:
    q_rows, k_rows = [], []
    zero_half = jnp.zeros((m, HG_WIDTH), BF16)
    for i in range(c // (2 * m)):
      lo = i * 2 * m
      b_ref_row = b[lo + m - 1:lo + m, :]
      k_rows += [(kk[lo:lo + m] * jnp.exp(b_ref_row - b[lo:lo + m])).astype(BF16), zero_half]
      q_rows += [zero_half, (qq[lo + m:lo + 2 * m] * jnp.exp(b[lo + m:lo + 2 * m] - b_ref_row)).astype(BF16)]
    shift = m.bit_length() - 1
    mask = ((t_idx >> (shift + 1)) == (s_idx >> (shift + 1))) & (((t_idx >> shift) & 1) == 1) & (
        ((s_idx >> shift) & 1) == 0)
    levels.append((jnp.concatenate(q_rows, axis=0), jnp.concatenate(k_rows, axis=0), mask))
    yield 1
    m //= 2
  blk = min(HG_BLOCK, c)
  q_rows, k_rows = [], []
  for i in range(c // blk):
    lo = i * blk
    b_loc = b[lo:lo + blk] if i == 0 else b[lo:lo + blk] - b[lo - 1:lo, :]
    q_rows.append(qq[lo:lo + blk] * jnp.exp(b_loc))
    k_rows.append(kk[lo:lo + blk] * jnp.exp(-b_loc))
  shift = blk.bit_length() - 1
  levels.append((jnp.concatenate(q_rows, axis=0).astype(BF16), padr(jnp.concatenate(k_rows, axis=0)).astype(BF16),
                 ((t_idx >> shift) == (s_idx >> shift)) & causal))

  yield 2
  head_slices = [slice(h * HG_DK, (h + 1) * HG_DK) for h in range(HG_HEADS)]
  v_bf = padr(cols(OFF_IV, HG_WIDTH)).astype(BF16)
  k_end_bf = padr(k_end).astype(BF16)
  att_bf = []
  for sl in head_slices:
    att = jnp.zeros((c, cs), F32)
    for ql, kl, mk in levels:
      att = jnp.where(mk, _dot_nt(ql[:, sl], kl[:, sl]), att)
    att_bf.append(att.astype(BF16))
    yield 1
  s0 = [hg_ref[h] for h in range(HG_HEADS)]
  o_heads = [_dot(att_bf[h], v_bf[:, sl]) + _dot(q_inter[:, sl], s0[h].astype(BF16))
             for h, sl in enumerate(head_slices)]
  for h, sl in enumerate(head_slices):
    hg_ref[h] = s0[h] * _col_matrix(state_decay[:, sl], HG_DV) + _dot_tn(k_end_bf[:, sl], v_bf[:, sl])
  o_parts = [o_h * lax.rsqrt(jnp.mean(o_h * o_h, axis=-1, keepdims=True) + EPS) for o_h in o_heads]
  o = jnp.concatenate(o_parts, axis=1) * p.hg_norm * _silu(cols(OFF_OG, HG_WIDTH))
  return jnp.concatenate([y_ssd, o], axis=1)


def _load_mix_params(refs):
  return MixParams(*[r[...] for r in refs])


def _drive(gen, between=lambda n: None):
  while True:
    try:
      between(next(gen))
    except StopIteration as done:
      return done.value


def _drive_round_robin(gens):
  results = [None] * len(gens)
  active = list(range(len(gens)))
  while active:
    for idx in list(active):
      try:
        next(gens[idx])
      except StopIteration as done:
        results[idx] = done.value
        active.remove(idx)
  return results


def _swiglu(h, wup_ref, wdn_ref):
  acc = jnp.zeros((h.shape[0], D_MODEL), F32)
  for ci in range(N_FF_CHUNKS):
    lo = ci * FF_CHUNK
    g = _dot(h, wup_ref[:, lo:lo + FF_CHUNK])
    u = _dot(h, wup_ref[:, D_FF + lo:D_FF + lo + FF_CHUNK])
    acc = acc + _dot((_silu(g) * u).astype(BF16), wdn_ref[lo:lo + FF_CHUNK, :])
  return acc


def _first_step_select(first_ref, rest_ref):
  return jnp.where(pl.program_id(0) == 0, first_ref[...], rest_ref[...])


def _ffn1_kernel(xs_ref, xp_ref, nw_ref, wup_ref, wdn_ref, os_ref, op_ref):
  x = _first_step_select(xs_ref, xp_ref)
  h = _rmsnorm(x, nw_ref[...]).astype(BF16)
  op_ref[...] = x + 0.5 * _swiglu(h, wup_ref, wdn_ref)

  @pl.when(pl.program_id(0) == 0)
  def _():
    os_ref[...] = op_ref[...]


def _split_w_in_kernel(w_ref, wa_ref, wb_ref, wdt_ref):
  rows = w_ref.shape[0]
  o_dt = SSD_DINNER + CONV_DIM
  o_q = o_dt + SSD_HEADS
  wa_ref[...] = w_ref[:, :o_dt].astype(BF16)
  wb_ref[...] = w_ref[:, o_q:].astype(BF16)
  wdt_ref[...] = jnp.concatenate([w_ref[:, o_dt:o_q], jnp.zeros((rows, LANES - SSD_HEADS), F32)], axis=1).astype(BF16)


def _proj_blocks(w_refs):
  blocks = []
  dst = 0
  for w in w_refs:
    width = w.shape[1]
    blocks += [(w, lo, min(lo + PROJ_BLOCK, width), dst + lo) for lo in range(0, width, PROJ_BLOCK)]
    dst += width
  assert dst == PROJ_W
  return blocks


def _mix_prompt_kernel(x_ref, nmix_ref, wa_ref, wb_ref, wdt_ref, *rest, chunks_per_seq):
  prm_refs = rest[:len(MixParams._fields)]
  mixed_ref, conv_ref, ssm_ref, hg_ref, xb_ref, proj_a, proj_b = rest[len(MixParams._fields):]
  i = pl.program_id(0)
  t_cur = lax.rem(jnp.maximum(i - 1, 0), chunks_per_seq)

  @pl.when(i == 0)
  def _():
    proj_b[...] = jnp.zeros(proj_b.shape, F32)

  @pl.when(t_cur == 0)
  def _():
    xb_ref[0:SUBLANES, :] = jnp.zeros((SUBLANES, CONV_DIM), F32)
    ssm_ref[...] = jnp.zeros(ssm_ref.shape, F32)
    hg_ref[...] = jnp.zeros(hg_ref.shape, F32)

  def step(cur_ref, next_ref):
    h = _rmsnorm(x_ref[0], nmix_ref[...]).astype(BF16)
    pending = _proj_blocks((wa_ref, wb_ref, wdt_ref))
    n_blocks = len(pending)
    credit = [0.0]

    def project_blocks(n):
      credit[0] += n * n_blocks / MIX_PHASE_WEIGHT
      while pending and credit[0] >= 1.0:
        credit[0] -= 1.0
        w_ref, lo, hi, dst = pending.pop(0)
        next_ref[:, dst:dst + hi - lo] = _dot(h, w_ref[:, lo:hi])

    mixed = _drive(_mix_chunk(cur_ref, c=CHUNK, valid=CHUNK, p=_load_mix_params(prm_refs), xb_ref=xb_ref,
                              ssm_ref=ssm_ref.at[0], hg_ref=hg_ref.at[0]), project_blocks)
    project_blocks(MIX_PHASE_WEIGHT)
    mixed_ref[0] = mixed.astype(BF16)
    tail = xb_ref[CHUNK:CHUNK + SUBLANES, :]
    xb_ref[0:SUBLANES, :] = tail

  parity = lax.rem(i, 2)
  pl.when(parity == 0)(functools.partial(step, proj_b, proj_a))
  pl.when(parity == 1)(functools.partial(step, proj_a, proj_b))

  @pl.when((t_cur == chunks_per_seq - 1) & (i > 0))
  def _():
    conv_ref[0] = xb_ref[SUBLANES + CHUNK - (CONV_W - 1):SUBLANES + CHUNK, :]


def _proj_sample_kernel(x_ref, nmix_ref, wa_ref, wb_ref, wdt_ref, o_ref):
  h = _rmsnorm(x_ref[...], nmix_ref[...]).astype(BF16)
  for w_ref, lo, hi, dst in _proj_blocks((wa_ref, wb_ref, wdt_ref)):
    o_ref[:, dst:dst + hi - lo] = _dot(h, w_ref[:, lo:hi])


def _rec_sample_kernel(proj_ref, conv_in_ref, ssm_in_ref, hg_in_ref, *rest, valid):
  prm_refs = rest[:len(MixParams._fields)]
  mixed_ref, conv_ref, ssm_ref, hg_ref, xb_ref = rest[len(MixParams._fields):]
  n_seq = proj_ref.shape[0]
  prm = _load_mix_params(prm_refs)
  ssm_ref[...] = ssm_in_ref[...]
  hg_ref[...] = hg_in_ref[...]
  gens = []
  for s in range(n_seq):
    xb_ref[s, 0:SUBLANES, :] = jnp.zeros((SUBLANES, CONV_DIM), F32)
    xb_ref[s, SUBLANES - (CONV_W - 1):SUBLANES, :] = conv_in_ref[s]
    gens.append(_mix_chunk(proj_ref.at[s], c=SAMPLE_ROWS, valid=valid, p=prm, xb_ref=xb_ref.at[s],
                           ssm_ref=ssm_ref.at[s], hg_ref=hg_ref.at[s]))
  for s, mixed in enumerate(_drive_round_robin(gens)):
    mixed_ref[s] = mixed.astype(BF16)
    conv_ref[s] = xb_ref[s, SUBLANES + valid - (CONV_W - 1):SUBLANES + valid, :]


def _tail_kernel(x1s_ref, x1p_ref, mixs_ref, mixp_ref, ps_ref, pp_ref, wout_ref, nf2_ref, wup_ref, wdn_ref, nple_ref,
                 wgate_ref, wproj_ref, ppost_ref, nfin_ref, os_ref, op_ref):
  x2 = _first_step_select(x1s_ref, x1p_ref) + _dot(_first_step_select(mixs_ref, mixp_ref), wout_ref[...])
  h = _rmsnorm(x2, nf2_ref[...]).astype(BF16)
  e = _rmsnorm(_dot(_first_step_select(ps_ref, pp_ref).astype(BF16), wproj_ref[...]), ppost_ref[...])
  x3 = x2 + 0.5 * _swiglu(h, wup_ref, wdn_ref)
  gate = _sigmoid(_dot(_rmsnorm(x3, nple_ref[...]).astype(BF16), wgate_ref[...]))
  x4 = x3 + gate * e
  op_ref[...] = _rmsnorm(x4, nfin_ref[...])

  @pl.when(pl.program_id(0) == 0)
  def _():
    os_ref[...] = op_ref[...]


def _resident(shape):
  nd = len(shape)
  return pl.BlockSpec(shape, lambda *_: (0,) * nd, pipeline_mode=pl.Buffered(1))


def _params(semantics, flags=None):
  return pltpu.CompilerParams(dimension_semantics=semantics, vmem_limit_bytes=VMEM_LIMIT_BYTES, flags=flags)


def _sample_then_prompt_specs(width):
  sample = pl.BlockSpec((FFN_TILE, width), lambda i: (0, 0))
  prompt = pl.BlockSpec((FFN_TILE, width), lambda i: (jnp.maximum(i - 1, 0), 0))
  return sample, prompt


def _ffn1(xs, xp, nw, wup, wdn):
  assert xs.shape[0] == FFN_TILE and xp.shape[0] % FFN_TILE == 0
  specs = _sample_then_prompt_specs(D_MODEL)
  return pl.pallas_call(
      _ffn1_kernel,
      grid=(xp.shape[0] // FFN_TILE + 1,),
      in_specs=[*specs, _resident(nw.shape), _resident(wup.shape), _resident(wdn.shape)],
      out_specs=specs,
      out_shape=(jax.ShapeDtypeStruct(xs.shape, F32), jax.ShapeDtypeStruct(xp.shape, F32)),
      compiler_params=_params(("arbitrary",)),
      name="ffn1",
  )(xs, xp, nw, wup, wdn)


def _split_w_in(w):
  rows = 128
  width_a = SSD_DINNER + CONV_DIM
  width_b = 4 * HG_WIDTH
  assert w.shape == (D_MODEL, width_a + SSD_HEADS + width_b)
  out = lambda width: pl.BlockSpec((rows, width), lambda i: (i, 0))
  return pl.pallas_call(
      _split_w_in_kernel,
      grid=(D_MODEL // rows,),
      in_specs=[pl.BlockSpec((rows, w.shape[1]), lambda i: (i, 0))],
      out_specs=(out(width_a), out(width_b), out(LANES)),
      out_shape=(jax.ShapeDtypeStruct((D_MODEL, width_a), BF16), jax.ShapeDtypeStruct((D_MODEL, width_b), BF16),
                 jax.ShapeDtypeStruct((D_MODEL, LANES), BF16)),
      compiler_params=_params(("arbitrary",)),
      name="split_w_in",
  )(w)


def _mix_prompt(x1, nmix, wall, prm):
  bsz, seq, _ = x1.shape
  nt = seq // CHUNK
  n_chunks = bsz * nt
  out_shape = (
      jax.ShapeDtypeStruct((bsz, seq, D_MIX), BF16),
      jax.ShapeDtypeStruct((bsz, CONV_W - 1, CONV_DIM), F32),
      jax.ShapeDtypeStruct((bsz, SSD_HEADS, SSD_HEADDIM, SSD_STATE), F32),
      jax.ShapeDtypeStruct((bsz, HG_HEADS, HG_DK, HG_DV), F32),
  )
  proj_chunk = lambda i: jnp.minimum(i, n_chunks - 1)
  mix_chunk = lambda i: jnp.maximum(i - 1, 0)
  out_specs = (
      pl.BlockSpec((1, CHUNK, D_MIX), lambda i: (mix_chunk(i) // nt, mix_chunk(i) % nt, 0)),
      pl.BlockSpec((1, CONV_W - 1, CONV_DIM), lambda i: (mix_chunk(i) // nt, 0, 0)),
      pl.BlockSpec((1, SSD_HEADS, SSD_HEADDIM, SSD_STATE), lambda i: (mix_chunk(i) // nt, 0, 0, 0)),
      pl.BlockSpec((1, HG_HEADS, HG_DK, HG_DV), lambda i: (mix_chunk(i) // nt, 0, 0, 0)),
  )
  in_specs = [pl.BlockSpec((1, CHUNK, D_MODEL), lambda i: (proj_chunk(i) // nt, proj_chunk(i) % nt, 0)),
              _resident(nmix.shape)] + [_resident(a.shape) for a in (*wall, *prm)]
  return pl.pallas_call(
      functools.partial(_mix_prompt_kernel, chunks_per_seq=nt),
      grid=(n_chunks + 1,),
      in_specs=in_specs,
      out_specs=out_specs,
      out_shape=out_shape,
      scratch_shapes=[pltpu.VMEM((SUBLANES + CHUNK, CONV_DIM), F32), pltpu.VMEM((CHUNK, PROJ_W), F32),
                      pltpu.VMEM((CHUNK, PROJ_W), F32)],
      compiler_params=_params(("arbitrary",)),
      name="mix_prompt",
  )(x1, nmix, *wall, *prm)


def _proj_sample(x1p, nmix, wall):
  n = x1p.shape[0]
  tile = 256
  return pl.pallas_call(
      _proj_sample_kernel,
      grid=(n // tile,),
      in_specs=[pl.BlockSpec((tile, D_MODEL), lambda i: (i, 0)), _resident(nmix.shape)] + [
          _resident(w.shape) for w in wall],
      out_specs=pl.BlockSpec((tile, PROJ_W), lambda i: (i, 0)),
      out_shape=jax.ShapeDtypeStruct((n, PROJ_W), F32),
      compiler_params=_params(("arbitrary",)),
      name="proj_sample",
  )(x1p, nmix, *wall)


def _rec_sample(proj, conv0, ssm0, hg0, prm, valid):
  bsz = proj.shape[0]
  out_shape = (
      jax.ShapeDtypeStruct((bsz, SAMPLE_ROWS, D_MIX), BF16),
      jax.ShapeDtypeStruct((bsz, CONV_W - 1, CONV_DIM), F32),
      jax.ShapeDtypeStruct((bsz, SSD_HEADS, SSD_HEADDIM, SSD_STATE), F32),
      jax.ShapeDtypeStruct((bsz, HG_HEADS, HG_DK, HG_DV), F32),
  )
  g = SAMPLE_SEQS_PER_STEP
  state_specs = [
      pl.BlockSpec((g, CONV_W - 1, CONV_DIM), lambda b: (b, 0, 0)),
      pl.BlockSpec((g, SSD_HEADS, SSD_HEADDIM, SSD_STATE), lambda b: (b, 0, 0, 0)),
      pl.BlockSpec((g, HG_HEADS, HG_DK, HG_DV), lambda b: (b, 0, 0, 0)),
  ]
  in_specs = [pl.BlockSpec((g, SAMPLE_ROWS, PROJ_W), lambda b: (b, 0, 0))] + state_specs + [
      _resident(a.shape) for a in prm]
  out_specs = tuple([pl.BlockSpec((g, SAMPLE_ROWS, D_MIX), lambda b: (b, 0, 0))] + state_specs)
  return pl.pallas_call(
      functools.partial(_rec_sample_kernel, valid=valid),
      grid=(bsz // g,),
      in_specs=in_specs,
      out_specs=out_specs,
      out_shape=out_shape,
      scratch_shapes=[pltpu.VMEM((g, 2 * SUBLANES, CONV_DIM), F32)],
      compiler_params=_params(("arbitrary",)),
      name="rec_sample",
  )(proj, conv0, ssm0, hg0, *prm)


def _tail(x1s, x1p, mixs, mixp, ps, pp, wout, nf2, wup, wdn, nple, wgate, wproj, ppost, nfin):
  assert x1s.shape[0] == FFN_TILE and x1p.shape[0] % FFN_TILE == 0
  weights = (wout, nf2, wup, wdn, nple, wgate, wproj, ppost, nfin)
  out_specs = _sample_then_prompt_specs(D_MODEL)
  return pl.pallas_call(
      _tail_kernel,
      grid=(x1p.shape[0] // FFN_TILE + 1,),
      in_specs=[*out_specs, *_sample_then_prompt_specs(D_MIX), *_sample_then_prompt_specs(PLE_DIM)] + [
          _resident(w.shape) for w in weights],
      out_specs=out_specs,
      out_shape=(jax.ShapeDtypeStruct(x1s.shape, F32), jax.ShapeDtypeStruct(x1p.shape, F32)),
      compiler_params=_params(("arbitrary",)),
      name="tail",
  )(x1s, x1p, mixs, mixp, ps, pp, *weights)


def kernel(x_prompt, x_sample, state_conv, state_ssm, state_hgrn, p_prompt, p_sample, norm_ffn1, w_ffn1_up, w_ffn1_down, norm_mix, w_in, conv_w, conv_b, dt_bias, a_log, d_skip, ssd_norm, hg_lb_logits, hg_norm, w_out, norm_ffn2, w_ffn2_up, w_ffn2_down, norm_ple, w_ple_gate, w_ple_proj, ple_post_norm, norm_final):
  bp, seq, _ = x_prompt.shape
  bs, dec_seq, _ = x_sample.shape
  row = lambda v: v.reshape(1, -1).astype(F32)
  per_head = lambda v: jnp.repeat(v.astype(F32), SSD_HEADDIM).reshape(1, SSD_DINNER)
  head_lanes = lambda v: jnp.pad(v.astype(F32), (0, LANES - SSD_HEADS)).reshape(1, LANES)

  w1u, w1d = w_ffn1_up[0].astype(BF16), w_ffn1_down[0].astype(BF16)
  w2u, w2d = w_ffn2_up[0].astype(BF16), w_ffn2_down[0].astype(BF16)
  wall = _split_w_in(w_in[0])
  prm = MixParams(
      conv_w=conv_w[0].astype(F32), conv_b=row(conv_b[0]), dtb_c=head_lanes(dt_bias[0]),
      alog_c=head_lanes(a_log[0]), dskip_e=per_head(d_skip[0]), ssd_norm=row(ssd_norm[0]),
      lb_logits=hg_lb_logits.astype(F32), hg_norm=row(hg_norm[0]))
  nmix = row(norm_mix[0])
  tail_w = (w_out[0].astype(BF16), row(norm_ffn2[0]), w2u, w2d, row(norm_ple[0]), w_ple_gate[0].astype(BF16),
            w_ple_proj[0].astype(BF16), row(ple_post_norm[0]), row(norm_final))

  xp = x_prompt.reshape(bp * seq, D_MODEL)
  xs = x_sample.reshape(bs * dec_seq, D_MODEL)
  x1s, x1p = _ffn1(xs, xp, row(norm_ffn1[0]), w1u, w1d)

  mixed_p, conv_p, ssm_p, hg_p = _mix_prompt(x1p.reshape(bp, seq, D_MODEL), nmix, wall, prm)
  x1s_pad = jnp.pad(x1s.reshape(bs, dec_seq, D_MODEL), ((0, 0), (0, SAMPLE_ROWS - dec_seq), (0, 0)))
  proj_s = _proj_sample(x1s_pad.reshape(bs * SAMPLE_ROWS, D_MODEL), nmix, wall)
  mixed_s, conv_s, ssm_s, hg_s = _rec_sample(proj_s.reshape(bs, SAMPLE_ROWS, PROJ_W), state_conv[0], state_ssm[0],
                                             state_hgrn[0], prm, dec_seq)
  mixed_s = mixed_s[:, :dec_seq].reshape(bs * dec_seq, D_MIX)

  y_sample, y_prompt = _tail(x1s, x1p, mixed_s, mixed_p.reshape(bp * seq, D_MIX),
                             p_sample[0].reshape(bs * dec_seq, PLE_DIM), p_prompt[0].reshape(bp * seq, PLE_DIM), *tail_w)

  return (y_prompt.reshape(bp, seq, D_MODEL), y_sample.reshape(bs, dec_seq, D_MODEL), conv_p[None], ssm_p[None],
          hg_p[None], conv_s[None], ssm_s[None], hg_s[None])
```

```python
import functools
from typing import NamedTuple

import jax
import jax.numpy as jnp
from jax import lax
from jax.experimental import pallas as pl
from jax.experimental.pallas import tpu as pltpu

F32 = jnp.float32
BF16 = jnp.bfloat16

D_MODEL = 1024
D_FF = 2816
PLE_DIM = 256
EPS = 1e-6
SSD_HEADS = 16
SSD_HEADDIM = 64
SSD_DINNER = SSD_HEADS * SSD_HEADDIM
SSD_STATE = 128
SSD_GROUPS = 2
SSD_HEADS_PER_GROUP = SSD_HEADS // SSD_GROUPS
SSD_GROUP_WIDTH = SSD_DINNER // SSD_GROUPS
CONV_W = 4
CONV_DIM = SSD_DINNER + 2 * SSD_GROUPS * SSD_STATE
HG_HEADS = 8
HG_DK = 128
HG_DV = 128
HG_WIDTH = HG_HEADS * HG_DV
HG_BLOCK = 16
D_MIX = SSD_DINNER + HG_WIDTH

LANES = 128
SUBLANES = 8
VMEM_LIMIT_BYTES = 56 * 1024 * 1024

CHUNK = 128
SAMPLE_ROWS = 8
SAMPLE_SEQS_PER_STEP = 8
FFN_TILE = 512
FF_CHUNK = 256
N_FF_CHUNKS = D_FF // FF_CHUNK

OFF_Z = 0
OFF_XBC = OFF_Z + SSD_DINNER
OFF_Q = OFF_XBC + CONV_DIM
OFF_FR = OFF_Q + HG_WIDTH
OFF_IV = OFF_FR + HG_WIDTH
OFF_OG = OFF_IV + HG_WIDTH
OFF_DT = OFF_OG + HG_WIDTH
PROJ_W = OFF_DT + LANES
PROJ_BLOCK = 256
MIX_PHASE_WEIGHT = 34


class MixParams(NamedTuple):
  conv_w: jax.Array
  conv_b: jax.Array
  dtb_c: jax.Array
  alog_c: jax.Array
  dskip_e: jax.Array
  ssd_norm: jax.Array
  lb_logits: jax.Array
  hg_norm: jax.Array


def _dot(a, b):
  return jnp.dot(a, b, preferred_element_type=F32)


def _dot_nt(a, b):
  return lax.dot_general(a, b, (((1,), (1,)), ((), ())), preferred_element_type=F32)


def _dot_tn(a, b):
  return lax.dot_general(a, b, (((0,), (0,)), ((), ())), preferred_element_type=F32)


NEG_LOG2_E = -1.4426950408889634


def _sigmoid(x):
  return 1.0 / (1.0 + jnp.exp2(x * NEG_LOG2_E))


def _silu(x):
  return x * _sigmoid(x)


def _softplus(x):
  return jnp.maximum(x, 0.0) + jnp.log1p(jnp.exp(-jnp.abs(x)))


def _rmsnorm(x, w):
  ms = jnp.mean(x * x, axis=-1, keepdims=True)
  return x * lax.rsqrt(ms + EPS) * w


def _cumsum_rows(x):
  c, n = x.shape
  g = c // SUBLANES
  x3 = x.reshape(g, SUBLANES, n)
  sub = lax.broadcasted_iota(jnp.int32, (1, SUBLANES, n), 1)
  s = 1
  while s < SUBLANES:
    x3 = x3 + jnp.where(sub >= s, pltpu.roll(x3, s, axis=1), 0.0)
    s *= 2
  if g > 1:
    tot = jnp.broadcast_to(x3[:, SUBLANES - 1:SUBLANES, :], (g, SUBLANES, n))
    offs = [jnp.zeros((1, SUBLANES, n), x.dtype)]
    for k in range(1, g):
      offs.append(offs[-1] + tot[k - 1:k])
    x3 = x3 + jnp.concatenate(offs, axis=0)
  return x3.reshape(c, n)


def _expand_heads(xc):
  c = xc.shape[0]
  first_head = lax.broadcasted_iota(jnp.int32, (1, LANES), 1) < SSD_HEADDIM
  parts = []
  for j in range(SSD_HEADS // 2):
    a = jnp.broadcast_to(xc[:, 2 * j:2 * j + 1], (c, LANES))
    b = jnp.broadcast_to(xc[:, 2 * j + 1:2 * j + 2], (c, LANES))
    parts.append(jnp.where(first_head, a, b))
  return jnp.concatenate(parts, axis=1)


def _col_matrix(row_vec, n_lanes):
  n = row_vec.shape[1]
  return jnp.transpose(jnp.broadcast_to(row_vec, (n_lanes, n)))


def _mix_chunk(proj_ref, *, c, valid, p, xb_ref, ssm_ref, hg_ref):
  cs = max(c, LANES)

  def padr(x):
    if x.shape[0] == cs:
      return x
    return jnp.concatenate([x, jnp.zeros((cs - x.shape[0], x.shape[1]), x.dtype)], axis=0)

  row_valid = None
  if valid < c:
    row_valid = lax.broadcasted_iota(jnp.int32, (c, 1), 0) < valid

  def mask_rows(x):
    return x if row_valid is None else jnp.where(row_valid, x, 0.0)

  cols = lambda off, width: proj_ref[:, off:off + width]

  xbc = cols(OFF_XBC, CONV_DIM)
  xb_ref[SUBLANES:SUBLANES + c, :] = xbc
  conv = p.conv_b + p.conv_w[CONV_W - 1:CONV_W] * xbc
  for j in range(CONV_W - 1):
    conv = conv + p.conv_w[j:j + 1] * xb_ref[SUBLANES - (CONV_W - 1) + j:SUBLANES - (CONV_W - 1) + j + c, :]
  yield 2
  conv = _silu(conv)
  yield 2
  xs = conv[:, :SSD_DINNER]
  bm = conv[:, SSD_DINNER:SSD_DINNER + SSD_GROUPS * SSD_STATE]
  cm = conv[:, SSD_DINNER + SSD_GROUPS * SSD_STATE:]

  head_lane = lax.broadcasted_iota(jnp.int32, (1, LANES), 1) < SSD_HEADS
  dt_c = mask_rows(jnp.where(head_lane, _softplus(cols(OFF_DT, LANES) + p.dtb_c), 0.0))
  acs_c = _cumsum_rows(dt_c * -jnp.exp(p.alog_c))
  last_c = acs_c[c - 1:c, :]
  dec_in = _expand_heads(jnp.exp(acs_c))
  xd_end = xs * _expand_heads(dt_c * jnp.exp(last_c - acs_c))
  yield 2
  acs_t = jnp.transpose(padr(acs_c))
  dt_t = jnp.transpose(padr(dt_c))
  chunk_decay_t = jnp.transpose(jnp.broadcast_to(jnp.exp(last_c), (LANES, LANES)))

  t_idx = lax.broadcasted_iota(jnp.int32, (c, cs), 0)
  s_idx = lax.broadcasted_iota(jnp.int32, (c, cs), 1)
  causal = s_idx <= t_idx
  lane = lax.broadcasted_iota(jnp.int32, (1, LANES), 1)
  first_head = lane < SSD_HEADDIM

  group_heads = [range(g * SSD_HEADS_PER_GROUP, (g + 1) * SSD_HEADS_PER_GROUP) for g in range(SSD_GROUPS)]
  bg_p = [padr(bm[:, g * SSD_STATE:(g + 1) * SSD_STATE]).astype(BF16) for g in range(SSD_GROUPS)]
  cg = [cm[:, g * SSD_STATE:(g + 1) * SSD_STATE].astype(BF16) for g in range(SSD_GROUPS)]
  h0 = [ssm_ref[heads.start:heads.stop].reshape(SSD_GROUP_WIDTH, SSD_STATE) for heads in group_heads]
  cb = [_dot_nt(cg[g], bg_p[g]) for g in range(SSD_GROUPS)]
  y_inter = [_dot_nt(cg[g], h0[g].astype(BF16)) for g in range(SSD_GROUPS)]
  xs_bf = padr(xs).astype(BF16)
  zero_bf = jnp.zeros((cs, LANES), BF16)
  y_intra = []
  for g in range(SSD_GROUPS):
    for j in range(SSD_HEADS_PER_GROUP // 2):
      h_first = g * SSD_HEADS_PER_GROUP + 2 * j
      scores = []
      for h in (h_first, h_first + 1):
        diff = acs_c[:, h:h + 1] - acs_t[h:h + 1, :]
        decay_dt = jnp.where(causal, jnp.exp(diff), 0.0) * dt_t[h:h + 1, :]
        scores.append((cb[g] * decay_dt).astype(BF16))
      pair = xs_bf[:, h_first * SSD_HEADDIM:(h_first + 2) * SSD_HEADDIM]
      rhs = jnp.concatenate([jnp.where(first_head, pair, zero_bf), jnp.where(first_head, zero_bf, pair)], axis=0)
      y_intra.append(_dot(jnp.concatenate(scores, axis=1), rhs))
      yield 1
  for g, heads in enumerate(group_heads):
    xd_g = padr(xd_end[:, g * SSD_GROUP_WIDTH:(g + 1) * SSD_GROUP_WIDTH]).astype(BF16)
    decay_cols = jnp.concatenate(
        [jnp.broadcast_to(chunk_decay_t[h:h + 1, :], (SSD_HEADDIM, SSD_STATE)) for h in heads], axis=0)
    h1 = h0[g] * decay_cols + _dot_tn(xd_g, bg_p[g])
    ssm_ref[heads.start:heads.stop] = h1.reshape(SSD_HEADS_PER_GROUP, SSD_HEADDIM, SSD_STATE)
  y = jnp.concatenate(y_intra, axis=1) + jnp.concatenate(y_inter, axis=1) * dec_in + p.dskip_e * xs
  yield 1
  yg = y * _silu(cols(OFF_Z, SSD_DINNER))
  yield 1
  y_norm = []
  for g in range(SSD_GROUPS):
    blk = yg[:, g * SSD_GROUP_WIDTH:(g + 1) * SSD_GROUP_WIDTH]
    y_norm.append(blk * lax.rsqrt(jnp.mean(blk * blk, axis=-1, keepdims=True) + EPS))
  y_ssd = jnp.concatenate(y_norm, axis=1) * p.ssd_norm

  lg = p.lb_logits
  lg_max = jnp.max(lg, axis=0, keepdims=True)
  lg_exp = jnp.exp(lg - lg_max)
  lb = lg_exp[0:1] / jnp.sum(lg_exp, axis=0, keepdims=True)
  f = lb + (1.0 - lb) * _sigmoid(cols(OFF_FR, HG_WIDTH))
  yield 1
  logf = mask_rows(jnp.log(f))
  kk = mask_rows(1.0 - f)
  qq = _silu(cols(OFF_Q, HG_WIDTH))
  yield 1
  b = _cumsum_rows(logf)
  yield 2
  b_last = b[c - 1:c, :]
  q_inter = (qq * jnp.exp(b)).astype(BF16)
  k_end = kk * jnp.exp(b_last - b)
  state_decay = jnp.exp(b_last)
  yield 1

  levels = []
  m = c // 2
  while m >= HG_BLOCK:
    q_rows, k_rows = [], []
    zero_half = jnp.zeros((m, HG_WIDTH), BF16)
    for i in range(c // (2 * m)):
      lo = i * 2 * m
      b_ref_row = b[lo + m - 1:lo + m, :]
      k_rows += [(kk[lo:lo + m] * jnp.exp(b_ref_row - b[lo:lo + m])).astype(BF16), zero_half]
      q_rows += [zero_half, (qq[lo + m:lo + 2 * m] * jnp.exp(b[lo + m:lo + 2 * m] - b_ref_row)).astype(BF16)]
    shift = m.bit_length() - 1
    mask = ((t_idx >> (shift + 1)) == (s_idx >> (shift + 1))) & (((t_idx >> shift) & 1) == 1) & (
        ((s_idx >> shift) & 1) == 0)
    levels.append((jnp.concatenate(q_rows, axis=0), jnp.concatenate(k_rows, axis=0), mask))
    yield 1
    m //= 2
  blk = min(HG_BLOCK, c)
  q_rows, k_rows = [], []
  for i in range(c // blk):
    lo = i * blk
    b_loc = b[lo:lo + blk] if i == 0 else b[lo:lo + blk] - b[lo - 1:lo, :]
    q_rows.append(qq[lo:lo + blk] * jnp.exp(b_loc))
    k_rows.append(kk[lo:lo + blk] * jnp.exp(-b_loc))
  shift = blk.bit_length() - 1
  levels.append((jnp.concatenate(q_rows, axis=0).astype(BF16), padr(jnp.concatenate(k_rows, axis=0)).astype(BF16),
                 ((t_idx >> shift) == (s_idx >> shift)) & causal))

  yield 2
  head_slices = [slice(h * HG_DK, (h + 1) * HG_DK) for h in range(HG_HEADS)]
  v_bf = padr(cols(OFF_IV, HG_WIDTH)).astype(BF16)
  k_end_bf = padr(k_end).astype(BF16)
  att_bf = []
  for sl in head_slices:
    att = jnp.zeros((c, cs), F32)
    for ql, kl, mk in levels:
      att = jnp.where(mk, _dot_nt(ql[:, sl], kl[:, sl]), att)
    att_bf.append(att.astype(BF16))
    yield 1
  s0 = [hg_ref[h] for h in range(HG_HEADS)]
  o_heads = [_dot(att_bf[h], v_bf[:, sl]) + _dot(q_inter[:, sl], s0[h].astype(BF16))
             for h, sl in enumerate(head_slices)]
  for h, sl in enumerate(head_slices):
    hg_ref[h] = s0[h] * _col_matrix(state_decay[:, sl], HG_DV) + _dot_tn(k_end_bf[:, sl], v_bf[:, sl])
  o_parts = [o_h * lax.rsqrt(jnp.mean(o_h * o_h, axis=-1, keepdims=True) + EPS) for o_h in o_heads]
  o = jnp.concatenate(o_parts, axis=1) * p.hg_norm * _silu(cols(OFF_OG, HG_WIDTH))
  return jnp.concatenate([y_ssd, o], axis=1)


def _load_mix_params(refs):
  return MixParams(*[r[...] for r in refs])


def _drive(gen, between=lambda n: None):
  while True:
    try:
      between(next(gen))
    except StopIteration as done:
      return done.value


def _drive_round_robin(gens):
  results = [None] * len(gens)
  active = list(range(len(gens)))
  while active:
    for idx in list(active):
      try:
        next(gens[idx])
      except StopIteration as done:
        results[idx] = done.value
        active.remove(idx)
  return results


def _swiglu(h, wup_ref, wdn_ref):
  acc = jnp.zeros((h.shape[0], D_MODEL), F32)
  for ci in range(N_FF_CHUNKS):
    lo = ci * FF_CHUNK
    g = _dot(h, wup_ref[:, lo:lo + FF_CHUNK])
    u = _dot(h, wup_ref[:, D_FF + lo:D_FF + lo + FF_CHUNK])
    acc = acc + _dot((_silu(g) * u).astype(BF16), wdn_ref[lo:lo + FF_CHUNK, :])
  return acc


def _first_step_select(first_ref, rest_ref):
  return jnp.where(pl.program_id(0) == 0, first_ref[...], rest_ref[...])


def _ffn1_kernel(xs_ref, xp_ref, nw_ref, wup_ref, wdn_ref, os_ref, op_ref):
  x = _first_step_select(xs_ref, xp_ref)
  h = _rmsnorm(x, nw_ref[...]).astype(BF16)
  op_ref[...] = x + 0.5 * _swiglu(h, wup_ref, wdn_ref)

  @pl.when(pl.program_id(0) == 0)
  def _():
    os_ref[...] = op_ref[...]


def _split_w_in_kernel(wt_ref, wa_ref, wb_ref, wdt_ref):
  o_dt = SSD_DINNER + CONV_DIM
  o_q = o_dt + SSD_HEADS
  wa_ref[...] = jnp.transpose(wt_ref[0:o_dt, :]).astype(BF16)
  wb_ref[...] = jnp.transpose(wt_ref[o_q:, :]).astype(BF16)
  dt_rows = jnp.concatenate([wt_ref[o_dt:o_q, :], jnp.zeros((LANES - SSD_HEADS, LANES), F32)], axis=0)
  wdt_ref[...] = jnp.transpose(dt_rows).astype(BF16)


def _proj_blocks(w_refs):
  blocks = []
  dst = 0
  for w in w_refs:
    width = w.shape[1]
    blocks += [(w, lo, min(lo + PROJ_BLOCK, width), dst + lo) for lo in range(0, width, PROJ_BLOCK)]
    dst += width
  assert dst == PROJ_W
  return blocks


def _mix_prompt_kernel(x_ref, nmix_ref, wa_ref, wb_ref, wdt_ref, *rest, chunks_per_seq):
  prm_refs = rest[:len(MixParams._fields)]
  mixed_ref, conv_ref, ssm_ref, hg_ref, xb_ref, proj_a, proj_b = rest[len(MixParams._fields):]
  i = pl.program_id(0)
  t_cur = lax.rem(jnp.maximum(i - 1, 0), chunks_per_seq)

  @pl.when(i == 0)
  def _():
    proj_b[...] = jnp.zeros(proj_b.shape, F32)

  @pl.when(t_cur == 0)
  def _():
    xb_ref[0:SUBLANES, :] = jnp.zeros((SUBLANES, CONV_DIM), F32)
    ssm_ref[...] = jnp.zeros(ssm_ref.shape, F32)
    hg_ref[...] = jnp.zeros(hg_ref.shape, F32)

  def step(cur_ref, next_ref):
    h = _rmsnorm(x_ref[0], nmix_ref[...]).astype(BF16)
    pending = _proj_blocks((wa_ref, wb_ref, wdt_ref))
    n_blocks = len(pending)
    credit = [0.0]

    def project_blocks(n):
      credit[0] += n * n_blocks / MIX_PHASE_WEIGHT
      while pending and credit[0] >= 1.0:
        credit[0] -= 1.0
        w_ref, lo, hi, dst = pending.pop(0)
        next_ref[:, dst:dst + hi - lo] = _dot(h, w_ref[:, lo:hi])

    mixed = _drive(_mix_chunk(cur_ref, c=CHUNK, valid=CHUNK, p=_load_mix_params(prm_refs), xb_ref=xb_ref,
                              ssm_ref=ssm_ref.at[0], hg_ref=hg_ref.at[0]), project_blocks)
    project_blocks(MIX_PHASE_WEIGHT)
    mixed_ref[0] = mixed.astype(BF16)
    tail = xb_ref[CHUNK:CHUNK + SUBLANES, :]
    xb_ref[0:SUBLANES, :] = tail

  parity = lax.rem(i, 2)
  pl.when(parity == 0)(functools.partial(step, proj_b, proj_a))
  pl.when(parity == 1)(functools.partial(step, proj_a, proj_b))

  @pl.when((t_cur == chunks_per_seq - 1) & (i > 0))
  def _():
    conv_ref[0] = xb_ref[SUBLANES + CHUNK - (CONV_W - 1):SUBLANES + CHUNK, :]


def _proj_sample_kernel(x_ref, nmix_ref, wa_ref, wb_ref, wdt_ref, o_ref):
  h = _rmsnorm(x_ref[...], nmix_ref[...]).astype(BF16)
  for w_ref, lo, hi, dst in _proj_blocks((wa_ref, wb_ref, wdt_ref)):
    o_ref[:, dst:dst + hi - lo] = _dot(h, w_ref[:, lo:hi])


def _rec_sample_kernel(proj_ref, conv_in_ref, ssm_in_ref, hg_in_ref, *rest, valid):
  prm_refs = rest[:len(MixParams._fields)]
  mixed_ref, conv_ref, ssm_ref, hg_ref, xb_ref = rest[len(MixParams._fields):]
  n_seq = proj_ref.shape[0]
  prm = _load_mix_params(prm_refs)
  ssm_ref[...] = ssm_in_ref[...]
  hg_ref[...] = hg_in_ref[...]
  gens = []
  for s in range(n_seq):
    xb_ref[s, 0:SUBLANES, :] = jnp.zeros((SUBLANES, CONV_DIM), F32)
    xb_ref[s, SUBLANES - (CONV_W - 1):SUBLANES, :] = conv_in_ref[s]
    gens.append(_mix_chunk(proj_ref.at[s], c=SAMPLE_ROWS, valid=valid, p=prm, xb_ref=xb_ref.at[s],
                           ssm_ref=ssm_ref.at[s], hg_ref=hg_ref.at[s]))
  for s, mixed in enumerate(_drive_round_robin(gens)):
    mixed_ref[s] = mixed.astype(BF16)
    conv_ref[s] = xb_ref[s, SUBLANES + valid - (CONV_W - 1):SUBLANES + valid, :]


def _tail_kernel(x1s_ref, x1p_ref, mixs_ref, mixp_ref, ps_ref, pp_ref, wout_ref, nf2_ref, wup_ref, wdn_ref, nple_ref,
                 wgate_ref, wproj_ref, ppost_ref, nfin_ref, os_ref, op_ref):
  x2 = _first_step_select(x1s_ref, x1p_ref) + _dot(_first_step_select(mixs_ref, mixp_ref), wout_ref[...])
  h = _rmsnorm(x2, nf2_ref[...]).astype(BF16)
  e = _rmsnorm(_dot(_first_step_select(ps_ref, pp_ref).astype(BF16), wproj_ref[...]), ppost_ref[...])
  x3 = x2 + 0.5 * _swiglu(h, wup_ref, wdn_ref)
  gate = _sigmoid(_dot(_rmsnorm(x3, nple_ref[...]).astype(BF16), wgate_ref[...]))
  x4 = x3 + gate * e
  op_ref[...] = _rmsnorm(x4, nfin_ref[...])

  @pl.when(pl.program_id(0) == 0)
  def _():
    os_ref[...] = op_ref[...]


def _resident(shape):
  nd = len(shape)
  return pl.BlockSpec(shape, lambda *_: (0,) * nd, pipeline_mode=pl.Buffered(1))


def _params(semantics, flags=None):
  return pltpu.CompilerParams(dimension_semantics=semantics, vmem_limit_bytes=VMEM_LIMIT_BYTES, flags=flags)


def _sample_then_prompt_specs(width):
  sample = pl.BlockSpec((FFN_TILE, width), lambda i: (0, 0))
  prompt = pl.BlockSpec((FFN_TILE, width), lambda i: (jnp.maximum(i - 1, 0), 0))
  return sample, prompt


def _ffn1(xs, xp, nw, wup, wdn):
  assert xs.shape[0] == FFN_TILE and xp.shape[0] % FFN_TILE == 0
  specs = _sample_then_prompt_specs(D_MODEL)
  return pl.pallas_call(
      _ffn1_kernel,
      grid=(xp.shape[0] // FFN_TILE + 1,),
      in_specs=[*specs, _resident(nw.shape), _resident(wup.shape), _resident(wdn.shape)],
      out_specs=specs,
      out_shape=(jax.ShapeDtypeStruct(xs.shape, F32), jax.ShapeDtypeStruct(xp.shape, F32)),
      compiler_params=_params(("arbitrary",)),
      name="ffn1",
  )(xs, xp, nw, wup, wdn)


def _split_w_in(wt):
  rows = LANES
  width_a = SSD_DINNER + CONV_DIM
  width_b = 4 * HG_WIDTH
  assert wt.shape == (width_a + SSD_HEADS + width_b, D_MODEL)
  out = lambda width: pl.BlockSpec((rows, width), lambda i: (i, 0))
  return pl.pallas_call(
      _split_w_in_kernel,
      grid=(D_MODEL // rows,),
      in_specs=[pl.BlockSpec((wt.shape[0], rows), lambda i: (0, i))],
      out_specs=(out(width_a), out(width_b), out(LANES)),
      out_shape=(jax.ShapeDtypeStruct((D_MODEL, width_a), BF16), jax.ShapeDtypeStruct((D_MODEL, width_b), BF16),
                 jax.ShapeDtypeStruct((D_MODEL, LANES), BF16)),
      compiler_params=_params(("arbitrary",)),
      name="split_w_in",
  )(wt)


def _mix_prompt(x1, nmix, wall, prm):
  bsz, seq, _ = x1.shape
  nt = seq // CHUNK
  n_chunks = bsz * nt
  out_shape = (
      jax.ShapeDtypeStruct((bsz, seq, D_MIX), BF16),
      jax.ShapeDtypeStruct((bsz, CONV_W - 1, CONV_DIM), F32),
      jax.ShapeDtypeStruct((bsz, SSD_HEADS, SSD_HEADDIM, SSD_STATE), F32),
      jax.ShapeDtypeStruct((bsz, HG_HEADS, HG_DK, HG_DV), F32),
  )
  proj_chunk = lambda i: jnp.minimum(i, n_chunks - 1)
  mix_chunk = lambda i: jnp.maximum(i - 1, 0)
  out_specs = (
      pl.BlockSpec((1, CHUNK, D_MIX), lambda i: (mix_chunk(i) // nt, mix_chunk(i) % nt, 0)),
      pl.BlockSpec((1, CONV_W - 1, CONV_DIM), lambda i: (mix_chunk(i) // nt, 0, 0)),
      pl.BlockSpec((1, SSD_HEADS, SSD_HEADDIM, SSD_STATE), lambda i: (mix_chunk(i) // nt, 0, 0, 0)),
      pl.BlockSpec((1, HG_HEADS, HG_DK, HG_DV), lambda i: (mix_chunk(i) // nt, 0, 0, 0)),
  )
  in_specs = [pl.BlockSpec((1, CHUNK, D_MODEL), lambda i: (proj_chunk(i) // nt, proj_chunk(i) % nt, 0)),
              _resident(nmix.shape)] + [_resident(a.shape) for a in (*wall, *prm)]
  return pl.pallas_call(
      functools.partial(_mix_prompt_kernel, chunks_per_seq=nt),
      grid=(n_chunks + 1,),
      in_specs=in_specs,
      out_specs=out_specs,
      out_shape=out_shape,
      scratch_shapes=[pltpu.VMEM((SUBLANES + CHUNK, CONV_DIM), F32), pltpu.VMEM((CHUNK, PROJ_W), F32),
                      pltpu.VMEM((CHUNK, PROJ_W), F32)],
      compiler_params=_params(("arbitrary",)),
      name="mix_prompt",
  )(x1, nmix, *wall, *prm)


def _proj_sample(x1p, nmix, wall):
  n = x1p.shape[0]
  tile = 256
  return pl.pallas_call(
      _proj_sample_kernel,
      grid=(n // tile,),
      in_specs=[pl.BlockSpec((tile, D_MODEL), lambda i: (i, 0)), _resident(nmix.shape)] + [
          _resident(w.shape) for w in wall],
      out_specs=pl.BlockSpec((tile, PROJ_W), lambda i: (i, 0)),
      out_shape=jax.ShapeDtypeStruct((n, PROJ_W), F32),
      compiler_params=_params(("arbitrary",)),
      name="proj_sample",
  )(x1p, nmix, *wall)


def _rec_sample(proj, conv0, ssm0, hg0, prm, valid):
  bsz = proj.shape[0]
  out_shape = (
      jax.ShapeDtypeStruct((bsz, SAMPLE_ROWS, D_MIX), BF16),
      jax.ShapeDtypeStruct((bsz, CONV_W - 1, CONV_DIM), F32),
      jax.ShapeDtypeStruct((bsz, SSD_HEADS, SSD_HEADDIM, SSD_STATE), F32),
      jax.ShapeDtypeStruct((bsz, HG_HEADS, HG_DK, HG_DV), F32),
  )
  g = SAMPLE_SEQS_PER_STEP
  state_specs = [
      pl.BlockSpec((g, CONV_W - 1, CONV_DIM), lambda b: (b, 0, 0)),
      pl.BlockSpec((g, SSD_HEADS, SSD_HEADDIM, SSD_STATE), lambda b: (b, 0, 0, 0)),
      pl.BlockSpec((g, HG_HEADS, HG_DK, HG_DV), lambda b: (b, 0, 0, 0)),
  ]
  in_specs = [pl.BlockSpec((g, SAMPLE_ROWS, PROJ_W), lambda b: (b, 0, 0))] + state_specs + [
      _resident(a.shape) for a in prm]
  out_specs = tuple([pl.BlockSpec((g, SAMPLE_ROWS, D_MIX), lambda b: (b, 0, 0))] + state_specs)
  return pl.pallas_call(
      functools.partial(_rec_sample_kernel, valid=valid),
      grid=(bsz // g,),
      in_specs=in_specs,
      out_specs=out_specs,
      out_shape=out_shape,
      scratch_shapes=[pltpu.VMEM((g, 2 * SUBLANES, CONV_DIM), F32)],
      compiler_params=_params(("arbitrary",)),
      name="rec_sample",
  )(proj, conv0, ssm0, hg0, *prm)


def _tail(x1s, x1p, mixs, mixp, ps, pp, wout, nf2, wup, wdn, nple, wgate, wproj, ppost, nfin):
  assert x1s.shape[0] == FFN_TILE and x1p.shape[0] % FFN_TILE == 0
  weights = (wout, nf2, wup, wdn, nple, wgate, wproj, ppost, nfin)
  out_specs = _sample_then_prompt_specs(D_MODEL)
  return pl.pallas_call(
      _tail_kernel,
      grid=(x1p.shape[0] // FFN_TILE + 1,),
      in_specs=[*out_specs, *_sample_then_prompt_specs(D_MIX), *_sample_then_prompt_specs(PLE_DIM)] + [
          _resident(w.shape) for w in weights],
      out_specs=out_specs,
      out_shape=(jax.ShapeDtypeStruct(x1s.shape, F32), jax.ShapeDtypeStruct(x1p.shape, F32)),
      compiler_params=_params(("arbitrary",)),
      name="tail",
  )(x1s, x1p, mixs, mixp, ps, pp, *weights)


def kernel(x_prompt, x_sample, state_conv, state_ssm, state_hgrn, p_prompt, p_sample, norm_ffn1, w_ffn1_up, w_ffn1_down, norm_mix, w_in, conv_w, conv_b, dt_bias, a_log, d_skip, ssd_norm, hg_lb_logits, hg_norm, w_out, norm_ffn2, w_ffn2_up, w_ffn2_down, norm_ple, w_ple_gate, w_ple_proj, ple_post_norm, norm_final):
  bp, seq, _ = x_prompt.shape
  bs, dec_seq, _ = x_sample.shape
  row = lambda v: v.reshape(1, -1).astype(F32)
  per_head = lambda v: jnp.repeat(v.astype(F32), SSD_HEADDIM).reshape(1, SSD_DINNER)
  head_lanes = lambda v: jnp.pad(v.astype(F32), (0, LANES - SSD_HEADS)).reshape(1, LANES)

  w1u, w1d = w_ffn1_up[0].astype(BF16), w_ffn1_down[0].astype(BF16)
  w2u, w2d = w_ffn2_up[0].astype(BF16), w_ffn2_down[0].astype(BF16)
  wall = _split_w_in(jnp.swapaxes(w_in[0], 0, 1))
  prm = MixParams(
      conv_w=conv_w[0].astype(F32), conv_b=row(conv_b[0]), dtb_c=head_lanes(dt_bias[0]),
      alog_c=head_lanes(a_log[0]), dskip_e=per_head(d_skip[0]), ssd_norm=row(ssd_norm[0]),
      lb_logits=hg_lb_logits.astype(F32), hg_norm=row(hg_norm[0]))
  nmix = row(norm_mix[0])
  tail_w = (w_out[0].astype(BF16), row(norm_ffn2[0]), w2u, w2d, row(norm_ple[0]), w_ple_gate[0].astype(BF16),
            w_ple_proj[0].astype(BF16), row(ple_post_norm[0]), row(norm_final))

  xp = x_prompt.reshape(bp * seq, D_MODEL)
  xs = x_sample.reshape(bs * dec_seq, D_MODEL)
  x1s, x1p = _ffn1(xs, xp, row(norm_ffn1[0]), w1u, w1d)

  mixed_p, conv_p, ssm_p, hg_p = _mix_prompt(x1p.reshape(bp, seq, D_MODEL), nmix, wall, prm)
  x1s_pad = jnp.pad(x1s.reshape(bs, dec_seq, D_MODEL), ((0, 0), (0, SAMPLE_ROWS - dec_seq), (0, 0)))
  proj_s = _proj_sample(x1s_pad.reshape(bs * SAMPLE_ROWS, D_MODEL), nmix, wall)
  mixed_s, conv_s, ssm_s, hg_s = _rec_sample(proj_s.reshape(bs, SAMPLE_ROWS, PROJ_W), state_conv[0], state_ssm[0],
                                             state_hgrn[0], prm, dec_seq)
  mixed_s = mixed_s[:, :dec_seq].reshape(bs * dec_seq, D_MIX)

  y_sample, y_prompt = _tail(x1s, x1p, mixed_s, mixed_p.reshape(bp * seq, D_MIX),
                             p_sample[0].reshape(bs * dec_seq, PLE_DIM), p_prompt[0].reshape(bp * seq, PLE_DIM), *tail_w)

  return (y_prompt.reshape(bp, seq, D_MODEL), y_sample.reshape(bs, dec_seq, D_MODEL), conv_p[None], ssm_p[None],
          hg_p[None], conv_s[None], ssm_s[None], hg_s[None])
```

```python
import functools
from typing import NamedTuple

import jax
import jax.numpy as jnp
from jax import lax
from jax.experimental import pallas as pl
from jax.experimental.pallas import tpu as pltpu

F32 = jnp.float32
BF16 = jnp.bfloat16

D_MODEL = 1024
D_FF = 2816
PLE_DIM = 256
EPS = 1e-6
SSD_HEADS = 16
SSD_HEADDIM = 64
SSD_DINNER = SSD_HEADS * SSD_HEADDIM
SSD_STATE = 128
SSD_GROUPS = 2
SSD_HEADS_PER_GROUP = SSD_HEADS // SSD_GROUPS
SSD_GROUP_WIDTH = SSD_DINNER // SSD_GROUPS
CONV_W = 4
CONV_DIM = SSD_DINNER + 2 * SSD_GROUPS * SSD_STATE
HG_HEADS = 8
HG_DK = 128
HG_DV = 128
HG_WIDTH = HG_HEADS * HG_DV
HG_BLOCK = 16
D_MIX = SSD_DINNER + HG_WIDTH

LANES = 128
SUBLANES = 8
VMEM_LIMIT_BYTES = 56 * 1024 * 1024

CHUNK = 128
SAMPLE_ROWS = 8
SAMPLE_SEQS_PER_STEP = 8
FFN_TILE = 512
OVERLAP_ROWS = 128
FF_CHUNK = 256
N_FF_CHUNKS = D_FF // FF_CHUNK

OFF_Z = 0
OFF_XBC = OFF_Z + SSD_DINNER
OFF_Q = OFF_XBC + CONV_DIM
OFF_FR = OFF_Q + HG_WIDTH
OFF_IV = OFF_FR + HG_WIDTH
OFF_OG = OFF_IV + HG_WIDTH
OFF_DT = OFF_OG + HG_WIDTH
PROJ_W = OFF_DT + LANES
PROJ_BLOCK = 256
MIX_PHASE_WEIGHT = 34


class MixParams(NamedTuple):
  conv_w: jax.Array
  conv_b: jax.Array
  dtb_c: jax.Array
  alog_c: jax.Array
  dskip_e: jax.Array
  ssd_norm: jax.Array
  lb_logits: jax.Array
  hg_norm: jax.Array


def _dot(a, b):
  return jnp.dot(a, b, preferred_element_type=F32)


def _dot_nt(a, b):
  return lax.dot_general(a, b, (((1,), (1,)), ((), ())), preferred_element_type=F32)


def _dot_tn(a, b):
  return lax.dot_general(a, b, (((0,), (0,)), ((), ())), preferred_element_type=F32)


NEG_LOG2_E = -1.4426950408889634


def _sigmoid(x):
  return 1.0 / (1.0 + jnp.exp2(x * NEG_LOG2_E))


def _silu(x):
  return x * _sigmoid(x)


def _softplus(x):
  return jnp.maximum(x, 0.0) + jnp.log1p(jnp.exp(-jnp.abs(x)))


def _rmsnorm(x, w):
  ms = jnp.mean(x * x, axis=-1, keepdims=True)
  return x * lax.rsqrt(ms + EPS) * w


def _cumsum_rows(x):
  c, n = x.shape
  g = c // SUBLANES
  x3 = x.reshape(g, SUBLANES, n)
  sub = lax.broadcasted_iota(jnp.int32, (1, SUBLANES, n), 1)
  s = 1
  while s < SUBLANES:
    x3 = x3 + jnp.where(sub >= s, pltpu.roll(x3, s, axis=1), 0.0)
    s *= 2
  if g > 1:
    tot = jnp.broadcast_to(x3[:, SUBLANES - 1:SUBLANES, :], (g, SUBLANES, n))
    offs = [jnp.zeros((1, SUBLANES, n), x.dtype)]
    for k in range(1, g):
      offs.append(offs[-1] + tot[k - 1:k])
    x3 = x3 + jnp.concatenate(offs, axis=0)
  return x3.reshape(c, n)


def _expand_heads(xc):
  c = xc.shape[0]
  first_head = lax.broadcasted_iota(jnp.int32, (1, LANES), 1) < SSD_HEADDIM
  parts = []
  for j in range(SSD_HEADS // 2):
    a = jnp.broadcast_to(xc[:, 2 * j:2 * j + 1], (c, LANES))
    b = jnp.broadcast_to(xc[:, 2 * j + 1:2 * j + 2], (c, LANES))
    parts.append(jnp.where(first_head, a, b))
  return jnp.concatenate(parts, axis=1)


def _col_matrix(row_vec, n_lanes):
  n = row_vec.shape[1]
  return jnp.transpose(jnp.broadcast_to(row_vec, (n_lanes, n)))


def _mix_chunk(proj_ref, *, c, valid, p, xb_ref, ssm_ref, hg_ref):
  cs = max(c, LANES)

  def padr(x):
    if x.shape[0] == cs:
      return x
    return jnp.concatenate([x, jnp.zeros((cs - x.shape[0], x.shape[1]), x.dtype)], axis=0)

  row_valid = None
  if valid < c:
    row_valid = lax.broadcasted_iota(jnp.int32, (c, 1), 0) < valid

  def mask_rows(x):
    return x if row_valid is None else jnp.where(row_valid, x, 0.0)

  cols = lambda off, width: proj_ref[:, off:off + width]

  xbc = cols(OFF_XBC, CONV_DIM)
  xb_ref[SUBLANES:SUBLANES + c, :] = xbc
  conv = p.conv_b + p.conv_w[CONV_W - 1:CONV_W] * xbc
  for j in range(CONV_W - 1):
    conv = conv + p.conv_w[j:j + 1] * xb_ref[SUBLANES - (CONV_W - 1) + j:SUBLANES - (CONV_W - 1) + j + c, :]
  yield 2
  conv = _silu(conv)
  yield 2
  xs = conv[:, :SSD_DINNER]
  bm = conv[:, SSD_DINNER:SSD_DINNER + SSD_GROUPS * SSD_STATE]
  cm = conv[:, SSD_DINNER + SSD_GROUPS * SSD_STATE:]

  head_lane = lax.broadcasted_iota(jnp.int32, (1, LANES), 1) < SSD_HEADS
  dt_c = mask_rows(jnp.where(head_lane, _softplus(cols(OFF_DT, LANES) + p.dtb_c), 0.0))
  acs_c = _cumsum_rows(dt_c * -jnp.exp(p.alog_c))
  last_c = acs_c[c - 1:c, :]
  dec_in = _expand_heads(jnp.exp(acs_c))
  xd_end = xs * _expand_heads(dt_c * jnp.exp(last_c - acs_c))
  yield 2
  acs_t = jnp.transpose(padr(acs_c))
  dt_t = jnp.transpose(padr(dt_c))
  chunk_decay_t = jnp.transpose(jnp.broadcast_to(jnp.exp(last_c), (LANES, LANES)))

  t_idx = lax.broadcasted_iota(jnp.int32, (c, cs), 0)
  s_idx = lax.broadcasted_iota(jnp.int32, (c, cs), 1)
  causal = s_idx <= t_idx
  lane = lax.broadcasted_iota(jnp.int32, (1, LANES), 1)
  first_head = lane < SSD_HEADDIM

  group_heads = [range(g * SSD_HEADS_PER_GROUP, (g + 1) * SSD_HEADS_PER_GROUP) for g in range(SSD_GROUPS)]
  bg_p = [padr(bm[:, g * SSD_STATE:(g + 1) * SSD_STATE]).astype(BF16) for g in range(SSD_GROUPS)]
  cg = [cm[:, g * SSD_STATE:(g + 1) * SSD_STATE].astype(BF16) for g in range(SSD_GROUPS)]
  h0 = [ssm_ref[heads.start:heads.stop].reshape(SSD_GROUP_WIDTH, SSD_STATE) for heads in group_heads]
  cb = [_dot_nt(cg[g], bg_p[g]) for g in range(SSD_GROUPS)]
  y_inter = [_dot_nt(cg[g], h0[g].astype(BF16)) for g in range(SSD_GROUPS)]
  xs_bf = padr(xs).astype(BF16)
  zero_bf = jnp.zeros((cs, LANES), BF16)
  y_intra = []
  for g in range(SSD_GROUPS):
    for j in range(SSD_HEADS_PER_GROUP // 2):
      h_first = g * SSD_HEADS_PER_GROUP + 2 * j
      scores = []
      for h in (h_first, h_first + 1):
        diff = acs_c[:, h:h + 1] - acs_t[h:h + 1, :]
        decay_dt = jnp.where(causal, jnp.exp(diff), 0.0) * dt_t[h:h + 1, :]
        scores.append((cb[g] * decay_dt).astype(BF16))
      pair = xs_bf[:, h_first * SSD_HEADDIM:(h_first + 2) * SSD_HEADDIM]
      rhs = jnp.concatenate([jnp.where(first_head, pair, zero_bf), jnp.where(first_head, zero_bf, pair)], axis=0)
      y_intra.append(_dot(jnp.concatenate(scores, axis=1), rhs))
      yield 1
  for g, heads in enumerate(group_heads):
    xd_g = padr(xd_end[:, g * SSD_GROUP_WIDTH:(g + 1) * SSD_GROUP_WIDTH]).astype(BF16)
    decay_cols = jnp.concatenate(
        [jnp.broadcast_to(chunk_decay_t[h:h + 1, :], (SSD_HEADDIM, SSD_STATE)) for h in heads], axis=0)
    h1 = h0[g] * decay_cols + _dot_tn(xd_g, bg_p[g])
    ssm_ref[heads.start:heads.stop] = h1.reshape(SSD_HEADS_PER_GROUP, SSD_HEADDIM, SSD_STATE)
  y = jnp.concatenate(y_intra, axis=1) + jnp.concatenate(y_inter, axis=1) * dec_in + p.dskip_e * xs
  yield 1
  yg = y * _silu(cols(OFF_Z, SSD_DINNER))
  yield 1
  y_norm = []
  for g in range(SSD_GROUPS):
    blk = yg[:, g * SSD_GROUP_WIDTH:(g + 1) * SSD_GROUP_WIDTH]
    y_norm.append(blk * lax.rsqrt(jnp.mean(blk * blk, axis=-1, keepdims=True) + EPS))
  y_ssd = jnp.concatenate(y_norm, axis=1) * p.ssd_norm

  lg = p.lb_logits
  lg_max = jnp.max(lg, axis=0, keepdims=True)
  lg_exp = jnp.exp(lg - lg_max)
  lb = lg_exp[0:1] / jnp.sum(lg_exp, axis=0, keepdims=True)
  f = lb + (1.0 - lb) * _sigmoid(cols(OFF_FR, HG_WIDTH))
  yield 1
  logf = mask_rows(jnp.log(f))
  kk = mask_rows(1.0 - f)
  qq = _silu(cols(OFF_Q, HG_WIDTH))
  yield 1
  b = _cumsum_rows(logf)
  yield 2
  b_last = b[c - 1:c, :]
  q_inter = (qq * jnp.exp(b)).astype(BF16)
  k_end = kk * jnp.exp(b_last - b)
  state_decay = jnp.exp(b_last)
  yield 1

  levels = []
  m = c // 2
  while m >= HG_BLOCK:
    q_rows, k_rows = [], []
    zero_half = jnp.zeros((m, HG_WIDTH), BF16)
    for i in range(c // (2 * m)):
      lo = i * 2 * m
      b_ref_row = b[lo + m - 1:lo + m, :]
      k_rows += [(kk[lo:lo + m] * jnp.exp(b_ref_row - b[lo:lo + m])).astype(BF16), zero_half]
      q_rows += [zero_half, (qq[lo + m:lo + 2 * m] * jnp.exp(b[lo + m:lo + 2 * m] - b_ref_row)).astype(BF16)]
    shift = m.bit_length() - 1
    mask = ((t_idx >> (shift + 1)) == (s_idx >> (shift + 1))) & (((t_idx >> shift) & 1) == 1) & (
        ((s_idx >> shift) & 1) == 0)
    levels.append((jnp.concatenate(q_rows, axis=0), jnp.concatenate(k_rows, axis=0), mask))
    yield 1
    m //= 2
  blk = min(HG_BLOCK, c)
  q_rows, k_rows = [], []
  for i in range(c // blk):
    lo = i * blk
    b_loc = b[lo:lo + blk] if i == 0 else b[lo:lo + blk] - b[lo - 1:lo, :]
    q_rows.append(qq[lo:lo + blk] * jnp.exp(b_loc))
    k_rows.append(kk[lo:lo + blk] * jnp.exp(-b_loc))
  shift = blk.bit_length() - 1
  levels.append((jnp.concatenate(q_rows, axis=0).astype(BF16), padr(jnp.concatenate(k_rows, axis=0)).astype(BF16),
                 ((t_idx >> shift) == (s_idx >> shift)) & causal))

  yield 2
  head_slices = [slice(h * HG_DK, (h + 1) * HG_DK) for h in range(HG_HEADS)]
  v_bf = padr(cols(OFF_IV, HG_WIDTH)).astype(BF16)
  k_end_bf = padr(k_end).astype(BF16)
  att_bf = []
  for sl in head_slices:
    att = jnp.zeros((c, cs), F32)
    for ql, kl, mk in levels:
      att = jnp.where(mk, _dot_nt(ql[:, sl], kl[:, sl]), att)
    att_bf.append(att.astype(BF16))
    yield 1
  s0 = [hg_ref[h] for h in range(HG_HEADS)]
  o_heads = [_dot(att_bf[h], v_bf[:, sl]) + _dot(q_inter[:, sl], s0[h].astype(BF16))
             for h, sl in enumerate(head_slices)]
  for h, sl in enumerate(head_slices):
    hg_ref[h] = s0[h] * _col_matrix(state_decay[:, sl], HG_DV) + _dot_tn(k_end_bf[:, sl], v_bf[:, sl])
  o_parts = [o_h * lax.rsqrt(jnp.mean(o_h * o_h, axis=-1, keepdims=True) + EPS) for o_h in o_heads]
  o = jnp.concatenate(o_parts, axis=1) * p.hg_norm * _silu(cols(OFF_OG, HG_WIDTH))
  return jnp.concatenate([y_ssd, o], axis=1)


def _load_mix_params(refs):
  return MixParams(*[r[...] for r in refs])


def _drive(gen, between=lambda n: None):
  while True:
    try:
      between(next(gen))
    except StopIteration as done:
      return done.value


def _drive_round_robin(gens):
  results = [None] * len(gens)
  active = list(range(len(gens)))
  while active:
    for idx in list(active):
      try:
        next(gens[idx])
      except StopIteration as done:
        results[idx] = done.value
        active.remove(idx)
  return results


def _swiglu(h, wup_ref, wdn_ref, between_chunks=lambda ci: None):
  acc = jnp.zeros((h.shape[0], D_MODEL), F32)
  for ci in range(N_FF_CHUNKS):
    lo = ci * FF_CHUNK
    g = _dot(h, wup_ref[:, lo:lo + FF_CHUNK])
    u = _dot(h, wup_ref[:, D_FF + lo:D_FF + lo + FF_CHUNK])
    acc = acc + _dot((_silu(g) * u).astype(BF16), wdn_ref[lo:lo + FF_CHUNK, :])
    between_chunks(ci)
  return acc


def _first_step_select(first_ref, rest_ref):
  return jnp.where(pl.program_id(0) == 0, first_ref[...], rest_ref[...])


def _ffn1_kernel(xs_ref, xp_ref, nw_ref, wup_ref, wdn_ref, os_ref, op_ref):
  x = _first_step_select(xs_ref, xp_ref)
  h = _rmsnorm(x, nw_ref[...]).astype(BF16)
  op_ref[...] = x + 0.5 * _swiglu(h, wup_ref, wdn_ref)

  @pl.when(pl.program_id(0) == 0)
  def _():
    os_ref[...] = op_ref[...]


def _split_w_in_kernel(wt_ref, wa_ref, wb_ref, wdt_ref):
  o_dt = SSD_DINNER + CONV_DIM
  o_q = o_dt + SSD_HEADS
  wa_ref[...] = jnp.transpose(wt_ref[0:o_dt, :]).astype(BF16)
  wb_ref[...] = jnp.transpose(wt_ref[o_q:, :]).astype(BF16)
  dt_rows = jnp.concatenate([wt_ref[o_dt:o_q, :], jnp.zeros((LANES - SSD_HEADS, LANES), F32)], axis=0)
  wdt_ref[...] = jnp.transpose(dt_rows).astype(BF16)


def _proj_blocks(w_refs):
  blocks = []
  dst = 0
  for w in w_refs:
    width = w.shape[1]
    blocks += [(w, lo, min(lo + PROJ_BLOCK, width), dst + lo) for lo in range(0, width, PROJ_BLOCK)]
    dst += width
  assert dst == PROJ_W
  return blocks


def _mix_prompt_kernel(x_ref, nmix_ref, wa_ref, wb_ref, wdt_ref, *rest, chunks_per_seq):
  prm_refs = rest[:len(MixParams._fields)]
  mixed_ref, conv_ref, ssm_ref, hg_ref, xb_ref, proj_a, proj_b = rest[len(MixParams._fields):]
  i = pl.program_id(0)
  t_cur = lax.rem(jnp.maximum(i - 1, 0), chunks_per_seq)

  @pl.when(i == 0)
  def _():
    proj_b[...] = jnp.zeros(proj_b.shape, F32)

  @pl.when(t_cur == 0)
  def _():
    xb_ref[0:SUBLANES, :] = jnp.zeros((SUBLANES, CONV_DIM), F32)
    ssm_ref[...] = jnp.zeros(ssm_ref.shape, F32)
    hg_ref[...] = jnp.zeros(hg_ref.shape, F32)

  def step(cur_ref, next_ref):
    h = _rmsnorm(x_ref[0], nmix_ref[...]).astype(BF16)
    pending = _proj_blocks((wa_ref, wb_ref, wdt_ref))
    n_blocks = len(pending)
    credit = [0.0]

    def project_blocks(n):
      credit[0] += n * n_blocks / MIX_PHASE_WEIGHT
      while pending and credit[0] >= 1.0:
        credit[0] -= 1.0
        w_ref, lo, hi, dst = pending.pop(0)
        next_ref[:, dst:dst + hi - lo] = _dot(h, w_ref[:, lo:hi])

    mixed = _drive(_mix_chunk(cur_ref, c=CHUNK, valid=CHUNK, p=_load_mix_params(prm_refs), xb_ref=xb_ref,
                              ssm_ref=ssm_ref.at[0], hg_ref=hg_ref.at[0]), project_blocks)
    project_blocks(MIX_PHASE_WEIGHT)
    mixed_ref[0] = mixed.astype(BF16)
    tail = xb_ref[CHUNK:CHUNK + SUBLANES, :]
    xb_ref[0:SUBLANES, :] = tail

  parity = lax.rem(i, 2)
  pl.when(parity == 0)(functools.partial(step, proj_b, proj_a))
  pl.when(parity == 1)(functools.partial(step, proj_a, proj_b))

  @pl.when((t_cur == chunks_per_seq - 1) & (i > 0))
  def _():
    conv_ref[0] = xb_ref[SUBLANES + CHUNK - (CONV_W - 1):SUBLANES + CHUNK, :]


def _proj_sample_kernel(x_ref, nmix_ref, wa_ref, wb_ref, wdt_ref, o_ref):
  h = _rmsnorm(x_ref[...], nmix_ref[...]).astype(BF16)
  for w_ref, lo, hi, dst in _proj_blocks((wa_ref, wb_ref, wdt_ref)):
    o_ref[:, dst:dst + hi - lo] = _dot(h, w_ref[:, lo:hi])


def _rec_sample_kernel(proj_ref, conv_in_ref, ssm_in_ref, hg_in_ref, *rest, valid):
  prm_refs = rest[:len(MixParams._fields)]
  mixed_ref, conv_ref, ssm_ref, hg_ref, xb_ref, padded_ref = rest[len(MixParams._fields):]
  n_seq = conv_in_ref.shape[0]
  prm = _load_mix_params(prm_refs)
  ssm_ref[...] = ssm_in_ref[...]
  hg_ref[...] = hg_in_ref[...]
  gens = []
  for s in range(n_seq):
    xb_ref[s, 0:SUBLANES, :] = jnp.zeros((SUBLANES, CONV_DIM), F32)
    xb_ref[s, SUBLANES - (CONV_W - 1):SUBLANES, :] = conv_in_ref[s]
    padded_ref[s, 0:valid, :] = proj_ref[s * valid:(s + 1) * valid, :]
    padded_ref[s, valid:SAMPLE_ROWS, :] = jnp.zeros((SAMPLE_ROWS - valid, PROJ_W), F32)
    gens.append(_mix_chunk(padded_ref.at[s], c=SAMPLE_ROWS, valid=valid, p=prm, xb_ref=xb_ref.at[s],
                           ssm_ref=ssm_ref.at[s], hg_ref=hg_ref.at[s]))
  for s, mixed in enumerate(_drive_round_robin(gens)):
    mixed_ref[s] = mixed.astype(BF16)
    conv_ref[s] = xb_ref[s, SUBLANES + valid - (CONV_W - 1):SUBLANES + valid, :]


def _tail_kernel(x1s_ref, x1p_ref, mixs_ref, mixp_ref, ps_ref, pp_ref, wout_ref, nf2_ref, wup_ref, wdn_ref, nple_ref,
                 wgate_ref, wproj_ref, ppost_ref, nfin_ref, os_ref, op_ref, x3_ref, e_ref, *, n_tiles):
  i = pl.program_id(0)

  @pl.when(i == 0)
  def _():
    x3_ref[...] = jnp.zeros(x3_ref.shape, F32)
    e_ref[...] = jnp.zeros(e_ref.shape, F32)

  def finish_rows(lo):
    rows = slice(lo, lo + OVERLAP_ROWS)
    x3 = x3_ref[rows, :]
    gate = _sigmoid(_dot(_rmsnorm(x3, nple_ref[...]).astype(BF16), wgate_ref[...]))
    x4 = x3 + gate * e_ref[rows, :]
    op_ref[rows, :] = _rmsnorm(x4, nfin_ref[...])

  finish_starts = list(range(0, FFN_TILE, OVERLAP_ROWS))

  @pl.when(i < n_tiles)
  def _():
    x2 = _first_step_select(x1s_ref, x1p_ref) + _dot(_first_step_select(mixs_ref, mixp_ref), wout_ref[...])
    h = _rmsnorm(x2, nf2_ref[...]).astype(BF16)
    e = _rmsnorm(_dot(_first_step_select(ps_ref, pp_ref).astype(BF16), wproj_ref[...]), ppost_ref[...])
    pending = list(finish_starts)
    every = N_FF_CHUNKS // (len(pending) + 1)

    def between_chunks(ci):
      if pending and (ci + 1) % every == 0:
        finish_rows(pending.pop(0))

    x3 = x2 + 0.5 * _swiglu(h, wup_ref, wdn_ref, between_chunks)
    while pending:
      finish_rows(pending.pop(0))
    x3_ref[...] = x3
    e_ref[...] = e

  @pl.when(i == n_tiles)
  def _():
    for lo in finish_starts:
      finish_rows(lo)

  @pl.when(i == 1)
  def _():
    os_ref[...] = op_ref[...]


def _resident(shape):
  nd = len(shape)
  return pl.BlockSpec(shape, lambda *_: (0,) * nd, pipeline_mode=pl.Buffered(1))


def _params(semantics, flags=None):
  return pltpu.CompilerParams(dimension_semantics=semantics, vmem_limit_bytes=VMEM_LIMIT_BYTES, flags=flags)


def _sample_then_prompt_specs(width):
  sample = pl.BlockSpec((FFN_TILE, width), lambda i: (0, 0))
  prompt = pl.BlockSpec((FFN_TILE, width), lambda i: (jnp.maximum(i - 1, 0), 0))
  return sample, prompt


def _ffn1(xs, xp, nw, wup, wdn):
  assert xs.shape[0] == FFN_TILE and xp.shape[0] % FFN_TILE == 0
  specs = _sample_then_prompt_specs(D_MODEL)
  return pl.pallas_call(
      _ffn1_kernel,
      grid=(xp.shape[0] // FFN_TILE + 1,),
      in_specs=[*specs, _resident(nw.shape), _resident(wup.shape), _resident(wdn.shape)],
      out_specs=specs,
      out_shape=(jax.ShapeDtypeStruct(xs.shape, F32), jax.ShapeDtypeStruct(xp.shape, F32)),
      compiler_params=_params(("arbitrary",)),
      name="ffn1",
  )(xs, xp, nw, wup, wdn)


def _split_w_in(wt):
  rows = LANES
  width_a = SSD_DINNER + CONV_DIM
  width_b = 4 * HG_WIDTH
  assert wt.shape == (width_a + SSD_HEADS + width_b, D_MODEL)
  out = lambda width: pl.BlockSpec((rows, width), lambda i: (i, 0))
  return pl.pallas_call(
      _split_w_in_kernel,
      grid=(D_MODEL // rows,),
      in_specs=[pl.BlockSpec((wt.shape[0], rows), lambda i: (0, i))],
      out_specs=(out(width_a), out(width_b), out(LANES)),
      out_shape=(jax.ShapeDtypeStruct((D_MODEL, width_a), BF16), jax.ShapeDtypeStruct((D_MODEL, width_b), BF16),
                 jax.ShapeDtypeStruct((D_MODEL, LANES), BF16)),
      compiler_params=_params(("arbitrary",)),
      name="split_w_in",
  )(wt)


def _mix_prompt(x1, nmix, wall, prm):
  bsz, seq, _ = x1.shape
  nt = seq // CHUNK
  n_chunks = bsz * nt
  out_shape = (
      jax.ShapeDtypeStruct((bsz, seq, D_MIX), BF16),
      jax.ShapeDtypeStruct((bsz, CONV_W - 1, CONV_DIM), F32),
      jax.ShapeDtypeStruct((bsz, SSD_HEADS, SSD_HEADDIM, SSD_STATE), F32),
      jax.ShapeDtypeStruct((bsz, HG_HEADS, HG_DK, HG_DV), F32),
  )
  proj_chunk = lambda i: jnp.minimum(i, n_chunks - 1)
  mix_chunk = lambda i: jnp.maximum(i - 1, 0)
  out_specs = (
      pl.BlockSpec((1, CHUNK, D_MIX), lambda i: (mix_chunk(i) // nt, mix_chunk(i) % nt, 0)),
      pl.BlockSpec((1, CONV_W - 1, CONV_DIM), lambda i: (mix_chunk(i) // nt, 0, 0)),
      pl.BlockSpec((1, SSD_HEADS, SSD_HEADDIM, SSD_STATE), lambda i: (mix_chunk(i) // nt, 0, 0, 0)),
      pl.BlockSpec((1, HG_HEADS, HG_DK, HG_DV), lambda i: (mix_chunk(i) // nt, 0, 0, 0)),
  )
  in_specs = [pl.BlockSpec((1, CHUNK, D_MODEL), lambda i: (proj_chunk(i) // nt, proj_chunk(i) % nt, 0)),
              _resident(nmix.shape)] + [_resident(a.shape) for a in (*wall, *prm)]
  return pl.pallas_call(
      functools.partial(_mix_prompt_kernel, chunks_per_seq=nt),
      grid=(n_chunks + 1,),
      in_specs=in_specs,
      out_specs=out_specs,
      out_shape=out_shape,
      scratch_shapes=[pltpu.VMEM((SUBLANES + CHUNK, CONV_DIM), F32), pltpu.VMEM((CHUNK, PROJ_W), F32),
                      pltpu.VMEM((CHUNK, PROJ_W), F32)],
      compiler_params=_params(("arbitrary",)),
      name="mix_prompt",
  )(x1, nmix, *wall, *prm)


def _proj_sample(x1p, nmix, wall):
  n = x1p.shape[0]
  tile = 256
  return pl.pallas_call(
      _proj_sample_kernel,
      grid=(n // tile,),
      in_specs=[pl.BlockSpec((tile, D_MODEL), lambda i: (i, 0)), _resident(nmix.shape)] + [
          _resident(w.shape) for w in wall],
      out_specs=pl.BlockSpec((tile, PROJ_W), lambda i: (i, 0)),
      out_shape=jax.ShapeDtypeStruct((n, PROJ_W), F32),
      compiler_params=_params(("arbitrary",)),
      name="proj_sample",
  )(x1p, nmix, *wall)


def _rec_sample(proj, conv0, ssm0, hg0, prm, valid):
  bsz = conv0.shape[0]
  assert proj.shape[0] == bsz * valid
  out_shape = (
      jax.ShapeDtypeStruct((bsz, SAMPLE_ROWS, D_MIX), BF16),
      jax.ShapeDtypeStruct((bsz, CONV_W - 1, CONV_DIM), F32),
      jax.ShapeDtypeStruct((bsz, SSD_HEADS, SSD_HEADDIM, SSD_STATE), F32),
      jax.ShapeDtypeStruct((bsz, HG_HEADS, HG_DK, HG_DV), F32),
  )
  g = SAMPLE_SEQS_PER_STEP
  state_specs = [
      pl.BlockSpec((g, CONV_W - 1, CONV_DIM), lambda b: (b, 0, 0)),
      pl.BlockSpec((g, SSD_HEADS, SSD_HEADDIM, SSD_STATE), lambda b: (b, 0, 0, 0)),
      pl.BlockSpec((g, HG_HEADS, HG_DK, HG_DV), lambda b: (b, 0, 0, 0)),
  ]
  in_specs = [pl.BlockSpec((g * valid, PROJ_W), lambda b: (b, 0))] + state_specs + [_resident(a.shape) for a in prm]
  out_specs = tuple([pl.BlockSpec((g, SAMPLE_ROWS, D_MIX), lambda b: (b, 0, 0))] + state_specs)
  return pl.pallas_call(
      functools.partial(_rec_sample_kernel, valid=valid),
      grid=(bsz // g,),
      in_specs=in_specs,
      out_specs=out_specs,
      out_shape=out_shape,
      scratch_shapes=[pltpu.VMEM((g, 2 * SUBLANES, CONV_DIM), F32), pltpu.VMEM((g, SAMPLE_ROWS, PROJ_W), F32)],
      compiler_params=_params(("arbitrary",)),
      name="rec_sample",
  )(proj, conv0, ssm0, hg0, *prm)


def _tail(x1s, x1p, mixs, mixp, ps, pp, wout, nf2, wup, wdn, nple, wgate, wproj, ppost, nfin):
  assert x1s.shape[0] == FFN_TILE and x1p.shape[0] % FFN_TILE == 0
  weights = (wout, nf2, wup, wdn, nple, wgate, wproj, ppost, nfin)
  n_prompt_tiles = x1p.shape[0] // FFN_TILE
  n_tiles = n_prompt_tiles + 1

  def in_specs(width):
    return (pl.BlockSpec((FFN_TILE, width), lambda i: (0, 0), pipeline_mode=pl.Buffered(1)),
            pl.BlockSpec((FFN_TILE, width), lambda i: (jnp.clip(i - 1, 0, n_prompt_tiles - 1), 0)))

  out_specs = (
      pl.BlockSpec((FFN_TILE, D_MODEL), lambda i: (0, 0)),
      pl.BlockSpec((FFN_TILE, D_MODEL), lambda i: (jnp.clip(i - 2, 0, n_prompt_tiles - 1), 0)))
  return pl.pallas_call(
      functools.partial(_tail_kernel, n_tiles=n_tiles),
      grid=(n_tiles + 1,),
      in_specs=[*in_specs(D_MODEL), *in_specs(D_MIX), *in_specs(PLE_DIM)] + [_resident(w.shape) for w in weights],
      out_specs=out_specs,
      out_shape=(jax.ShapeDtypeStruct(x1s.shape, F32), jax.ShapeDtypeStruct(x1p.shape, F32)),
      scratch_shapes=[pltpu.VMEM((FFN_TILE, D_MODEL), F32), pltpu.VMEM((FFN_TILE, D_MODEL), F32)],
      compiler_params=_params(("arbitrary",)),
      name="tail",
  )(x1s, x1p, mixs, mixp, ps, pp, *weights)


def kernel(x_prompt, x_sample, state_conv, state_ssm, state_hgrn, p_prompt, p_sample, norm_ffn1, w_ffn1_up, w_ffn1_down, norm_mix, w_in, conv_w, conv_b, dt_bias, a_log, d_skip, ssd_norm, hg_lb_logits, hg_norm, w_out, norm_ffn2, w_ffn2_up, w_ffn2_down, norm_ple, w_ple_gate, w_ple_proj, ple_post_norm, norm_final):
  bp, seq, _ = x_prompt.shape
  bs, dec_seq, _ = x_sample.shape
  row = lambda v: v.reshape(1, -1).astype(F32)
  per_head = lambda v: jnp.repeat(v.astype(F32), SSD_HEADDIM).reshape(1, SSD_DINNER)
  head_lanes = lambda v: jnp.pad(v.astype(F32), (0, LANES - SSD_HEADS)).reshape(1, LANES)

  w1u, w1d = w_ffn1_up[0].astype(BF16), w_ffn1_down[0].astype(BF16)
  w2u, w2d = w_ffn2_up[0].astype(BF16), w_ffn2_down[0].astype(BF16)
  wall = _split_w_in(jnp.swapaxes(w_in[0], 0, 1))
  prm = MixParams(
      conv_w=conv_w[0].astype(F32), conv_b=row(conv_b[0]), dtb_c=head_lanes(dt_bias[0]),
      alog_c=head_lanes(a_log[0]), dskip_e=per_head(d_skip[0]), ssd_norm=row(ssd_norm[0]),
      lb_logits=hg_lb_logits.astype(F32), hg_norm=row(hg_norm[0]))
  nmix = row(norm_mix[0])
  tail_w = (w_out[0].astype(BF16), row(norm_ffn2[0]), w2u, w2d, row(norm_ple[0]), w_ple_gate[0].astype(BF16),
            w_ple_proj[0].astype(BF16), row(ple_post_norm[0]), row(norm_final))

  xp = x_prompt.reshape(bp * seq, D_MODEL)
  xs = x_sample.reshape(bs * dec_seq, D_MODEL)
  x1s, x1p = _ffn1(xs, xp, row(norm_ffn1[0]), w1u, w1d)

  mixed_p, conv_p, ssm_p, hg_p = _mix_prompt(x1p.reshape(bp, seq, D_MODEL), nmix, wall, prm)
  proj_s = _proj_sample(x1s, nmix, wall)
  mixed_s, conv_s, ssm_s, hg_s = _rec_sample(proj_s, state_conv[0], state_ssm[0], state_hgrn[0], prm, dec_seq)
  mixed_s = mixed_s[:, :dec_seq].reshape(bs * dec_seq, D_MIX)

  y_sample, y_prompt = _tail(x1s, x1p, mixed_s, mixed_p.reshape(bp * seq, D_MIX),
                             p_sample[0].reshape(bs * dec_seq, PLE_DIM), p_prompt[0].reshape(bp * seq, PLE_DIM), *tail_w)

  return (y_prompt.reshape(bp, seq, D_MODEL), y_sample.reshape(bs, dec_seq, D_MODEL), conv_p[None], ssm_p[None],
          hg_p[None], conv_s[None], ssm_s[None], hg_s[None])
```

```python
import functools
from typing import NamedTuple

import jax
import jax.numpy as jnp
from jax import lax
from jax.experimental import pallas as pl
from jax.experimental.pallas import tpu as pltpu

F32 = jnp.float32
BF16 = jnp.bfloat16

D_MODEL = 1024
D_FF = 2816
PLE_DIM = 256
EPS = 1e-6
SSD_HEADS = 16
SSD_HEADDIM = 64
SSD_DINNER = SSD_HEADS * SSD_HEADDIM
SSD_STATE = 128
SSD_GROUPS = 2
SSD_HEADS_PER_GROUP = SSD_HEADS // SSD_GROUPS
SSD_GROUP_WIDTH = SSD_DINNER // SSD_GROUPS
CONV_W = 4
CONV_DIM = SSD_DINNER + 2 * SSD_GROUPS * SSD_STATE
HG_HEADS = 8
HG_DK = 128
HG_DV = 128
HG_WIDTH = HG_HEADS * HG_DV
HG_BLOCK = 16
D_MIX = SSD_DINNER + HG_WIDTH

LANES = 128
SUBLANES = 8
VMEM_LIMIT_BYTES = 56 * 1024 * 1024

CHUNK = 128
CHUNKS_PER_STEP = 2
SAMPLE_ROWS = 8
SAMPLE_SEQS_PER_STEP = 8
FFN_TILE = 512
OVERLAP_ROWS = 128
FF_CHUNK = 256
N_FF_CHUNKS = D_FF // FF_CHUNK

OFF_Z = 0
OFF_XBC = OFF_Z + SSD_DINNER
OFF_Q = OFF_XBC + CONV_DIM
OFF_FR = OFF_Q + HG_WIDTH
OFF_IV = OFF_FR + HG_WIDTH
OFF_OG = OFF_IV + HG_WIDTH
OFF_DT = OFF_OG + HG_WIDTH
PROJ_W = OFF_DT + LANES
PROJ_BLOCK = 256
MIX_PHASE_WEIGHT = 34


class MixParams(NamedTuple):
  conv_w: jax.Array
  conv_b: jax.Array
  dtb_c: jax.Array
  alog_c: jax.Array
  dskip_e: jax.Array
  ssd_norm: jax.Array
  lb_logits: jax.Array
  hg_norm: jax.Array


def _dot(a, b):
  return jnp.dot(a, b, preferred_element_type=F32)


def _dot_nt(a, b):
  return lax.dot_general(a, b, (((1,), (1,)), ((), ())), preferred_element_type=F32)


def _dot_tn(a, b):
  return lax.dot_general(a, b, (((0,), (0,)), ((), ())), preferred_element_type=F32)


NEG_LOG2_E = -1.4426950408889634


def _sigmoid(x):
  return 1.0 / (1.0 + jnp.exp2(x * NEG_LOG2_E))


def _silu(x):
  return x * _sigmoid(x)


def _softplus(x):
  return jnp.maximum(x, 0.0) + jnp.log1p(jnp.exp(-jnp.abs(x)))


def _rmsnorm(x, w):
  ms = jnp.mean(x * x, axis=-1, keepdims=True)
  return x * lax.rsqrt(ms + EPS) * w


def _cumsum_rows(x):
  c, n = x.shape
  g = c // SUBLANES
  x3 = x.reshape(g, SUBLANES, n)
  sub = lax.broadcasted_iota(jnp.int32, (1, SUBLANES, n), 1)
  s = 1
  while s < SUBLANES:
    x3 = x3 + jnp.where(sub >= s, pltpu.roll(x3, s, axis=1), 0.0)
    s *= 2
  if g > 1:
    tot = jnp.broadcast_to(x3[:, SUBLANES - 1:SUBLANES, :], (g, SUBLANES, n))
    offs = [jnp.zeros((1, SUBLANES, n), x.dtype)]
    for k in range(1, g):
      offs.append(offs[-1] + tot[k - 1:k])
    x3 = x3 + jnp.concatenate(offs, axis=0)
  return x3.reshape(c, n)


def _expand_heads(xc):
  c = xc.shape[0]
  first_head = lax.broadcasted_iota(jnp.int32, (1, LANES), 1) < SSD_HEADDIM
  parts = []
  for j in range(SSD_HEADS // 2):
    a = jnp.broadcast_to(xc[:, 2 * j:2 * j + 1], (c, LANES))
    b = jnp.broadcast_to(xc[:, 2 * j + 1:2 * j + 2], (c, LANES))
    parts.append(jnp.where(first_head, a, b))
  return jnp.concatenate(parts, axis=1)


def _col_matrix(row_vec, n_lanes):
  n = row_vec.shape[1]
  return jnp.transpose(jnp.broadcast_to(row_vec, (n_lanes, n)))


def _mix_chunk(proj_ref, *, c, valid, p, xb_ref, ssm_ref, hg_ref):
  cs = max(c, LANES)

  def padr(x):
    if x.shape[0] == cs:
      return x
    return jnp.concatenate([x, jnp.zeros((cs - x.shape[0], x.shape[1]), x.dtype)], axis=0)

  row_valid = None
  if valid < c:
    row_valid = lax.broadcasted_iota(jnp.int32, (c, 1), 0) < valid

  def mask_rows(x):
    return x if row_valid is None else jnp.where(row_valid, x, 0.0)

  cols = lambda off, width: proj_ref[:, off:off + width]

  xbc = cols(OFF_XBC, CONV_DIM)
  xb_ref[SUBLANES:SUBLANES + c, :] = xbc
  conv = p.conv_b + p.conv_w[CONV_W - 1:CONV_W] * xbc
  for j in range(CONV_W - 1):
    conv = conv + p.conv_w[j:j + 1] * xb_ref[SUBLANES - (CONV_W - 1) + j:SUBLANES - (CONV_W - 1) + j + c, :]
  yield 2
  conv = _silu(conv)
  yield 2
  xs = conv[:, :SSD_DINNER]
  bm = conv[:, SSD_DINNER:SSD_DINNER + SSD_GROUPS * SSD_STATE]
  cm = conv[:, SSD_DINNER + SSD_GROUPS * SSD_STATE:]

  head_lane = lax.broadcasted_iota(jnp.int32, (1, LANES), 1) < SSD_HEADS
  dt_c = mask_rows(jnp.where(head_lane, _softplus(cols(OFF_DT, LANES) + p.dtb_c), 0.0))
  acs_c = _cumsum_rows(dt_c * -jnp.exp(p.alog_c))
  last_c = acs_c[c - 1:c, :]
  dec_in = _expand_heads(jnp.exp(acs_c))
  xd_end = xs * _expand_heads(dt_c * jnp.exp(last_c - acs_c))
  yield 2
  acs_t = jnp.transpose(padr(acs_c))
  dt_t = jnp.transpose(padr(dt_c))
  chunk_decay_t = jnp.transpose(jnp.broadcast_to(jnp.exp(last_c), (LANES, LANES)))

  t_idx = lax.broadcasted_iota(jnp.int32, (c, cs), 0)
  s_idx = lax.broadcasted_iota(jnp.int32, (c, cs), 1)
  causal = s_idx <= t_idx
  lane = lax.broadcasted_iota(jnp.int32, (1, LANES), 1)
  first_head = lane < SSD_HEADDIM

  group_heads = [range(g * SSD_HEADS_PER_GROUP, (g + 1) * SSD_HEADS_PER_GROUP) for g in range(SSD_GROUPS)]
  bg_p = [padr(bm[:, g * SSD_STATE:(g + 1) * SSD_STATE]).astype(BF16) for g in range(SSD_GROUPS)]
  cg = [cm[:, g * SSD_STATE:(g + 1) * SSD_STATE].astype(BF16) for g in range(SSD_GROUPS)]
  h0 = [ssm_ref[heads.start:heads.stop].reshape(SSD_GROUP_WIDTH, SSD_STATE) for heads in group_heads]
  cb = [_dot_nt(cg[g], bg_p[g]) for g in range(SSD_GROUPS)]
  y_inter = [_dot_nt(cg[g], h0[g].astype(BF16)) for g in range(SSD_GROUPS)]
  xs_bf = padr(xs).astype(BF16)
  zero_bf = jnp.zeros((cs, LANES), BF16)
  y_intra = []
  for g in range(SSD_GROUPS):
    for j in range(SSD_HEADS_PER_GROUP // 2):
      h_first = g * SSD_HEADS_PER_GROUP + 2 * j
      scores = []
      for h in (h_first, h_first + 1):
        diff = acs_c[:, h:h + 1] - acs_t[h:h + 1, :]
        decay_dt = jnp.where(causal, jnp.exp(diff), 0.0) * dt_t[h:h + 1, :]
        scores.append((cb[g] * decay_dt).astype(BF16))
      pair = xs_bf[:, h_first * SSD_HEADDIM:(h_first + 2) * SSD_HEADDIM]
      rhs = jnp.concatenate([jnp.where(first_head, pair, zero_bf), jnp.where(first_head, zero_bf, pair)], axis=0)
      y_intra.append(_dot(jnp.concatenate(scores, axis=1), rhs))
      yield 1
  for g, heads in enumerate(group_heads):
    xd_g = padr(xd_end[:, g * SSD_GROUP_WIDTH:(g + 1) * SSD_GROUP_WIDTH]).astype(BF16)
    decay_cols = jnp.concatenate(
        [jnp.broadcast_to(chunk_decay_t[h:h + 1, :], (SSD_HEADDIM, SSD_STATE)) for h in heads], axis=0)
    h1 = h0[g] * decay_cols + _dot_tn(xd_g, bg_p[g])
    ssm_ref[heads.start:heads.stop] = h1.reshape(SSD_HEADS_PER_GROUP, SSD_HEADDIM, SSD_STATE)
  y = jnp.concatenate(y_intra, axis=1) + jnp.concatenate(y_inter, axis=1) * dec_in + p.dskip_e * xs
  yield 1
  yg = y * _silu(cols(OFF_Z, SSD_DINNER))
  yield 1
  y_norm = []
  for g in range(SSD_GROUPS):
    blk = yg[:, g * SSD_GROUP_WIDTH:(g + 1) * SSD_GROUP_WIDTH]
    y_norm.append(blk * lax.rsqrt(jnp.mean(blk * blk, axis=-1, keepdims=True) + EPS))
  y_ssd = jnp.concatenate(y_norm, axis=1) * p.ssd_norm

  lg = p.lb_logits
  lg_max = jnp.max(lg, axis=0, keepdims=True)
  lg_exp = jnp.exp(lg - lg_max)
  lb = lg_exp[0:1] / jnp.sum(lg_exp, axis=0, keepdims=True)
  f = lb + (1.0 - lb) * _sigmoid(cols(OFF_FR, HG_WIDTH))
  yield 1
  logf = mask_rows(jnp.log(f))
  kk = mask_rows(1.0 - f)
  qq = _silu(cols(OFF_Q, HG_WIDTH))
  yield 1
  b = _cumsum_rows(logf)
  yield 2
  b_last = b[c - 1:c, :]
  q_inter = (qq * jnp.exp(b)).astype(BF16)
  k_end = kk * jnp.exp(b_last - b)
  state_decay = jnp.exp(b_last)
  yield 1

  levels = []
  m = c // 2
  while m >= HG_BLOCK:
    q_rows, k_rows = [], []
    zero_half = jnp.zeros((m, HG_WIDTH), BF16)
    for i in range(c // (2 * m)):
      lo = i * 2 * m
      b_ref_row = b[lo + m - 1:lo + m, :]
      k_rows += [(kk[lo:lo + m] * jnp.exp(b_ref_row - b[lo:lo + m])).astype(BF16), zero_half]
      q_rows += [zero_half, (qq[lo + m:lo + 2 * m] * jnp.exp(b[lo + m:lo + 2 * m] - b_ref_row)).astype(BF16)]
    shift = m.bit_length() - 1
    mask = ((t_idx >> (shift + 1)) == (s_idx >> (shift + 1))) & (((t_idx >> shift) & 1) == 1) & (
        ((s_idx >> shift) & 1) == 0)
    levels.append((jnp.concatenate(q_rows, axis=0), jnp.concatenate(k_rows, axis=0), mask))
    yield 1
    m //= 2
  blk = min(HG_BLOCK, c)
  q_rows, k_rows = [], []
  for i in range(c // blk):
    lo = i * blk
    b_loc = b[lo:lo + blk] if i == 0 else b[lo:lo + blk] - b[lo - 1:lo, :]
    q_rows.append(qq[lo:lo + blk] * jnp.exp(b_loc))
    k_rows.append(kk[lo:lo + blk] * jnp.exp(-b_loc))
  shift = blk.bit_length() - 1
  levels.append((jnp.concatenate(q_rows, axis=0).astype(BF16), padr(jnp.concatenate(k_rows, axis=0)).astype(BF16),
                 ((t_idx >> shift) == (s_idx >> shift)) & causal))

  yield 2
  head_slices = [slice(h * HG_DK, (h + 1) * HG_DK) for h in range(HG_HEADS)]
  v_bf = padr(cols(OFF_IV, HG_WIDTH)).astype(BF16)
  k_end_bf = padr(k_end).astype(BF16)
  att_bf = []
  for sl in head_slices:
    att = jnp.zeros((c, cs), F32)
    for ql, kl, mk in levels:
      att = jnp.where(mk, _dot_nt(ql[:, sl], kl[:, sl]), att)
    att_bf.append(att.astype(BF16))
    yield 1
  s0 = [hg_ref[h] for h in range(HG_HEADS)]
  o_heads = [_dot(att_bf[h], v_bf[:, sl]) + _dot(q_inter[:, sl], s0[h].astype(BF16))
             for h, sl in enumerate(head_slices)]
  for h, sl in enumerate(head_slices):
    hg_ref[h] = s0[h] * _col_matrix(state_decay[:, sl], HG_DV) + _dot_tn(k_end_bf[:, sl], v_bf[:, sl])
  o_parts = [o_h * lax.rsqrt(jnp.mean(o_h * o_h, axis=-1, keepdims=True) + EPS) for o_h in o_heads]
  o = jnp.concatenate(o_parts, axis=1) * p.hg_norm * _silu(cols(OFF_OG, HG_WIDTH))
  return jnp.concatenate([y_ssd, o], axis=1)


def _load_mix_params(refs):
  return MixParams(*[r[...] for r in refs])


def _drive(gen, between=lambda n: None):
  while True:
    try:
      between(next(gen))
    except StopIteration as done:
      return done.value


def _drive_round_robin(gens):
  results = [None] * len(gens)
  active = list(range(len(gens)))
  while active:
    for idx in list(active):
      try:
        next(gens[idx])
      except StopIteration as done:
        results[idx] = done.value
        active.remove(idx)
  return results


def _swiglu(h, wup_ref, wdn_ref, between_chunks=lambda ci: None):
  acc = jnp.zeros((h.shape[0], D_MODEL), F32)
  for ci in range(N_FF_CHUNKS):
    lo = ci * FF_CHUNK
    g = _dot(h, wup_ref[:, lo:lo + FF_CHUNK])
    u = _dot(h, wup_ref[:, D_FF + lo:D_FF + lo + FF_CHUNK])
    acc = acc + _dot((_silu(g) * u).astype(BF16), wdn_ref[lo:lo + FF_CHUNK, :])
    between_chunks(ci)
  return acc


def _first_step_select(first_ref, rest_ref):
  return jnp.where(pl.program_id(0) == 0, first_ref[...], rest_ref[...])


def _ffn1_kernel(xs_ref, xp_ref, nw_ref, wup_ref, wdn_ref, os_ref, op_ref):
  x = _first_step_select(xs_ref, xp_ref)
  h = _rmsnorm(x, nw_ref[...]).astype(BF16)
  op_ref[...] = x + 0.5 * _swiglu(h, wup_ref, wdn_ref)

  @pl.when(pl.program_id(0) == 0)
  def _():
    os_ref[...] = op_ref[...]


def _split_w_in_kernel(wt_ref, wa_ref, wb_ref, wdt_ref):
  o_dt = SSD_DINNER + CONV_DIM
  o_q = o_dt + SSD_HEADS
  wa_ref[...] = jnp.transpose(wt_ref[0:o_dt, :]).astype(BF16)
  wb_ref[...] = jnp.transpose(wt_ref[o_q:, :]).astype(BF16)
  dt_rows = jnp.concatenate([wt_ref[o_dt:o_q, :], jnp.zeros((LANES - SSD_HEADS, LANES), F32)], axis=0)
  wdt_ref[...] = jnp.transpose(dt_rows).astype(BF16)


def _proj_blocks(w_refs):
  blocks = []
  dst = 0
  for w in w_refs:
    width = w.shape[1]
    blocks += [(w, lo, min(lo + PROJ_BLOCK, width), dst + lo) for lo in range(0, width, PROJ_BLOCK)]
    dst += width
  assert dst == PROJ_W
  return blocks


def _mix_prompt_kernel(x_ref, nmix_ref, wa_ref, wb_ref, wdt_ref, *rest, chunks_per_seq):
  prm_refs = rest[:len(MixParams._fields)]
  mixed_ref, conv_ref, ssm_ref, hg_ref, xb_ref, h_ref, proj_a, proj_b = rest[len(MixParams._fields):]
  pairs_per_seq = chunks_per_seq // CHUNKS_PER_STEP
  i = pl.program_id(0)
  t_cur = lax.rem(jnp.maximum(i - 1, 0), pairs_per_seq)

  @pl.when(i == 0)
  def _():
    proj_b[...] = jnp.zeros(proj_b.shape, F32)

  @pl.when(t_cur == 0)
  def _():
    xb_ref[0:SUBLANES, :] = jnp.zeros((SUBLANES, CONV_DIM), F32)
    ssm_ref[...] = jnp.zeros(ssm_ref.shape, F32)
    hg_ref[...] = jnp.zeros(hg_ref.shape, F32)

  def step(cur_buf, next_buf):
    h_ref[...] = _rmsnorm(x_ref[0], nmix_ref[...]).astype(BF16)
    blocks = _proj_blocks((wa_ref, wb_ref, wdt_ref))
    per_chunk = -(-len(blocks) // CHUNKS_PER_STEP)
    prm = _load_mix_params(prm_refs)
    for k in range(CHUNKS_PER_STEP):
      pending = blocks[k * per_chunk:(k + 1) * per_chunk]
      n_blocks = len(pending)
      credit = [0.0]

      def project_blocks(n):
        credit[0] += n * n_blocks / MIX_PHASE_WEIGHT
        while pending and credit[0] >= 1.0:
          credit[0] -= 1.0
          w_ref, lo, hi, dst = pending.pop(0)
          next_buf[:, dst:dst + hi - lo] = _dot(h_ref[...], w_ref[:, lo:hi])

      rows = pl.ds(k * CHUNK, CHUNK)
      mixed = _drive(_mix_chunk(cur_buf.at[rows], c=CHUNK, valid=CHUNK, p=prm, xb_ref=xb_ref,
                                ssm_ref=ssm_ref.at[0], hg_ref=hg_ref.at[0]), project_blocks)
      project_blocks(MIX_PHASE_WEIGHT)
      mixed_ref[0, rows, :] = mixed.astype(BF16)
      tail = xb_ref[CHUNK:CHUNK + SUBLANES, :]
      xb_ref[0:SUBLANES, :] = tail

  parity = lax.rem(i, 2)
  pl.when(parity == 0)(functools.partial(step, proj_b, proj_a))
  pl.when(parity == 1)(functools.partial(step, proj_a, proj_b))

  @pl.when((t_cur == pairs_per_seq - 1) & (i > 0))
  def _():
    conv_ref[0] = xb_ref[SUBLANES + CHUNK - (CONV_W - 1):SUBLANES + CHUNK, :]


def _proj_sample_kernel(x_ref, nmix_ref, wa_ref, wb_ref, wdt_ref, o_ref):
  h = _rmsnorm(x_ref[...], nmix_ref[...]).astype(BF16)
  for w_ref, lo, hi, dst in _proj_blocks((wa_ref, wb_ref, wdt_ref)):
    o_ref[:, dst:dst + hi - lo] = _dot(h, w_ref[:, lo:hi])


def _rec_sample_kernel(proj_ref, conv_in_ref, ssm_in_ref, hg_in_ref, *rest, valid):
  prm_refs = rest[:len(MixParams._fields)]
  mixed_ref, conv_ref, ssm_ref, hg_ref, xb_ref, padded_ref = rest[len(MixParams._fields):]
  n_seq = conv_in_ref.shape[0]
  prm = _load_mix_params(prm_refs)
  ssm_ref[...] = ssm_in_ref[...]
  hg_ref[...] = hg_in_ref[...]
  gens = []
  for s in range(n_seq):
    xb_ref[s, 0:SUBLANES, :] = jnp.zeros((SUBLANES, CONV_DIM), F32)
    xb_ref[s, SUBLANES - (CONV_W - 1):SUBLANES, :] = conv_in_ref[s]
    padded_ref[s, 0:valid, :] = proj_ref[s * valid:(s + 1) * valid, :]
    padded_ref[s, valid:SAMPLE_ROWS, :] = jnp.zeros((SAMPLE_ROWS - valid, PROJ_W), F32)
    gens.append(_mix_chunk(padded_ref.at[s], c=SAMPLE_ROWS, valid=valid, p=prm, xb_ref=xb_ref.at[s],
                           ssm_ref=ssm_ref.at[s], hg_ref=hg_ref.at[s]))
  for s, mixed in enumerate(_drive_round_robin(gens)):
    mixed_ref[s] = mixed.astype(BF16)
    conv_ref[s] = xb_ref[s, SUBLANES + valid - (CONV_W - 1):SUBLANES + valid, :]


def _tail_kernel(x1s_ref, x1p_ref, mixs_ref, mixp_ref, ps_ref, pp_ref, wout_ref, nf2_ref, wup_ref, wdn_ref, nple_ref,
                 wgate_ref, wproj_ref, ppost_ref, nfin_ref, os_ref, op_ref, x3_ref, e_ref, *, n_tiles):
  i = pl.program_id(0)

  @pl.when(i == 0)
  def _():
    x3_ref[...] = jnp.zeros(x3_ref.shape, F32)
    e_ref[...] = jnp.zeros(e_ref.shape, F32)

  def finish_rows(lo):
    rows = slice(lo, lo + OVERLAP_ROWS)
    x3 = x3_ref[rows, :]
    gate = _sigmoid(_dot(_rmsnorm(x3, nple_ref[...]).astype(BF16), wgate_ref[...]))
    x4 = x3 + gate * e_ref[rows, :]
    op_ref[rows, :] = _rmsnorm(x4, nfin_ref[...])

  finish_starts = list(range(0, FFN_TILE, OVERLAP_ROWS))

  @pl.when(i < n_tiles)
  def _():
    x2 = _first_step_select(x1s_ref, x1p_ref) + _dot(_first_step_select(mixs_ref, mixp_ref), wout_ref[...])
    h = _rmsnorm(x2, nf2_ref[...]).astype(BF16)
    e = _rmsnorm(_dot(_first_step_select(ps_ref, pp_ref).astype(BF16), wproj_ref[...]), ppost_ref[...])
    pending = list(finish_starts)
    every = N_FF_CHUNKS // (len(pending) + 1)

    def between_chunks(ci):
      if pending and (ci + 1) % every == 0:
        finish_rows(pending.pop(0))

    x3 = x2 + 0.5 * _swiglu(h, wup_ref, wdn_ref, between_chunks)
    while pending:
      finish_rows(pending.pop(0))
    x3_ref[...] = x3
    e_ref[...] = e

  @pl.when(i == n_tiles)
  def _():
    for lo in finish_starts:
      finish_rows(lo)

  @pl.when(i == 1)
  def _():
    os_ref[...] = op_ref[...]


def _resident(shape):
  nd = len(shape)
  return pl.BlockSpec(shape, lambda *_: (0,) * nd, pipeline_mode=pl.Buffered(1))


def _params(semantics, flags=None):
  return pltpu.CompilerParams(dimension_semantics=semantics, vmem_limit_bytes=VMEM_LIMIT_BYTES, flags=flags)


def _sample_then_prompt_specs(width):
  sample = pl.BlockSpec((FFN_TILE, width), lambda i: (0, 0))
  prompt = pl.BlockSpec((FFN_TILE, width), lambda i: (jnp.maximum(i - 1, 0), 0))
  return sample, prompt


def _ffn1(xs, xp, nw, wup, wdn):
  assert xs.shape[0] == FFN_TILE and xp.shape[0] % FFN_TILE == 0
  specs = _sample_then_prompt_specs(D_MODEL)
  return pl.pallas_call(
      _ffn1_kernel,
      grid=(xp.shape[0] // FFN_TILE + 1,),
      in_specs=[*specs, _resident(nw.shape), _resident(wup.shape), _resident(wdn.shape)],
      out_specs=specs,
      out_shape=(jax.ShapeDtypeStruct(xs.shape, F32), jax.ShapeDtypeStruct(xp.shape, F32)),
      compiler_params=_params(("arbitrary",)),
      name="ffn1",
  )(xs, xp, nw, wup, wdn)


def _split_w_in(wt):
  rows = LANES
  width_a = SSD_DINNER + CONV_DIM
  width_b = 4 * HG_WIDTH
  assert wt.shape == (width_a + SSD_HEADS + width_b, D_MODEL)
  out = lambda width: pl.BlockSpec((rows, width), lambda i: (i, 0))
  return pl.pallas_call(
      _split_w_in_kernel,
      grid=(D_MODEL // rows,),
      in_specs=[pl.BlockSpec((wt.shape[0], rows), lambda i: (0, i))],
      out_specs=(out(width_a), out(width_b), out(LANES)),
      out_shape=(jax.ShapeDtypeStruct((D_MODEL, width_a), BF16), jax.ShapeDtypeStruct((D_MODEL, width_b), BF16),
                 jax.ShapeDtypeStruct((D_MODEL, LANES), BF16)),
      compiler_params=_params(("arbitrary",)),
      name="split_w_in",
  )(wt)


def _mix_prompt(x1, nmix, wall, prm):
  bsz, seq, _ = x1.shape
  nt = seq // CHUNK
  n_chunks = bsz * nt
  out_shape = (
      jax.ShapeDtypeStruct((bsz, seq, D_MIX), BF16),
      jax.ShapeDtypeStruct((bsz, CONV_W - 1, CONV_DIM), F32),
      jax.ShapeDtypeStruct((bsz, SSD_HEADS, SSD_HEADDIM, SSD_STATE), F32),
      jax.ShapeDtypeStruct((bsz, HG_HEADS, HG_DK, HG_DV), F32),
  )
  assert nt % CHUNKS_PER_STEP == 0
  rows = CHUNKS_PER_STEP * CHUNK
  pps = nt // CHUNKS_PER_STEP
  n_pairs = n_chunks // CHUNKS_PER_STEP
  proj_pair = lambda i: jnp.minimum(i, n_pairs - 1)
  mix_pair = lambda i: jnp.maximum(i - 1, 0)
  out_specs = (
      pl.BlockSpec((1, rows, D_MIX), lambda i: (mix_pair(i) // pps, mix_pair(i) % pps, 0)),
      pl.BlockSpec((1, CONV_W - 1, CONV_DIM), lambda i: (mix_pair(i) // pps, 0, 0)),
      pl.BlockSpec((1, SSD_HEADS, SSD_HEADDIM, SSD_STATE), lambda i: (mix_pair(i) // pps, 0, 0, 0)),
      pl.BlockSpec((1, HG_HEADS, HG_DK, HG_DV), lambda i: (mix_pair(i) // pps, 0, 0, 0)),
  )
  in_specs = [pl.BlockSpec((1, rows, D_MODEL), lambda i: (proj_pair(i) // pps, proj_pair(i) % pps, 0)),
              _resident(nmix.shape)] + [_resident(a.shape) for a in (*wall, *prm)]
  return pl.pallas_call(
      functools.partial(_mix_prompt_kernel, chunks_per_seq=nt),
      grid=(n_pairs + 1,),
      in_specs=in_specs,
      out_specs=out_specs,
      out_shape=out_shape,
      scratch_shapes=[pltpu.VMEM((SUBLANES + CHUNK, CONV_DIM), F32), pltpu.VMEM((rows, D_MODEL), BF16),
                      pltpu.VMEM((rows, PROJ_W), F32), pltpu.VMEM((rows, PROJ_W), F32)],
      compiler_params=_params(("arbitrary",)),
      name="mix_prompt",
  )(x1, nmix, *wall, *prm)


def _proj_sample(x1p, nmix, wall):
  n = x1p.shape[0]
  tile = 256
  return pl.pallas_call(
      _proj_sample_kernel,
      grid=(n // tile,),
      in_specs=[pl.BlockSpec((tile, D_MODEL), lambda i: (i, 0)), _resident(nmix.shape)] + [
          _resident(w.shape) for w in wall],
      out_specs=pl.BlockSpec((tile, PROJ_W), lambda i: (i, 0)),
      out_shape=jax.ShapeDtypeStruct((n, PROJ_W), F32),
      compiler_params=_params(("arbitrary",)),
      name="proj_sample",
  )(x1p, nmix, *wall)


def _rec_sample(proj, conv0, ssm0, hg0, prm, valid):
  bsz = conv0.shape[0]
  assert proj.shape[0] == bsz * valid
  out_shape = (
      jax.ShapeDtypeStruct((bsz, SAMPLE_ROWS, D_MIX), BF16),
      jax.ShapeDtypeStruct((bsz, CONV_W - 1, CONV_DIM), F32),
      jax.ShapeDtypeStruct((bsz, SSD_HEADS, SSD_HEADDIM, SSD_STATE), F32),
      jax.ShapeDtypeStruct((bsz, HG_HEADS, HG_DK, HG_DV), F32),
  )
  g = SAMPLE_SEQS_PER_STEP
  state_specs = [
      pl.BlockSpec((g, CONV_W - 1, CONV_DIM), lambda b: (b, 0, 0)),
      pl.BlockSpec((g, SSD_HEADS, SSD_HEADDIM, SSD_STATE), lambda b: (b, 0, 0, 0)),
      pl.BlockSpec((g, HG_HEADS, HG_DK, HG_DV), lambda b: (b, 0, 0, 0)),
  ]
  in_specs = [pl.BlockSpec((g * valid, PROJ_W), lambda b: (b, 0))] + state_specs + [_resident(a.shape) for a in prm]
  out_specs = tuple([pl.BlockSpec((g, SAMPLE_ROWS, D_MIX), lambda b: (b, 0, 0))] + state_specs)
  return pl.pallas_call(
      functools.partial(_rec_sample_kernel, valid=valid),
      grid=(bsz // g,),
      in_specs=in_specs,
      out_specs=out_specs,
      out_shape=out_shape,
      scratch_shapes=[pltpu.VMEM((g, 2 * SUBLANES, CONV_DIM), F32), pltpu.VMEM((g, SAMPLE_ROWS, PROJ_W), F32)],
      compiler_params=_params(("arbitrary",)),
      name="rec_sample",
  )(proj, conv0, ssm0, hg0, *prm)


def _tail(x1s, x1p, mixs, mixp, ps, pp, wout, nf2, wup, wdn, nple, wgate, wproj, ppost, nfin):
  assert x1s.shape[0] == FFN_TILE and x1p.shape[0] % FFN_TILE == 0
  weights = (wout, nf2, wup, wdn, nple, wgate, wproj, ppost, nfin)
  n_prompt_tiles = x1p.shape[0] // FFN_TILE
  n_tiles = n_prompt_tiles + 1

  def in_specs(width):
    return (pl.BlockSpec((FFN_TILE, width), lambda i: (0, 0), pipeline_mode=pl.Buffered(1)),
            pl.BlockSpec((FFN_TILE, width), lambda i: (jnp.clip(i - 1, 0, n_prompt_tiles - 1), 0)))

  out_specs = (
      pl.BlockSpec((FFN_TILE, D_MODEL), lambda i: (0, 0)),
      pl.BlockSpec((FFN_TILE, D_MODEL), lambda i: (jnp.clip(i - 2, 0, n_prompt_tiles - 1), 0)))
  return pl.pallas_call(
      functools.partial(_tail_kernel, n_tiles=n_tiles),
      grid=(n_tiles + 1,),
      in_specs=[*in_specs(D_MODEL), *in_specs(D_MIX), *in_specs(PLE_DIM)] + [_resident(w.shape) for w in weights],
      out_specs=out_specs,
      out_shape=(jax.ShapeDtypeStruct(x1s.shape, F32), jax.ShapeDtypeStruct(x1p.shape, F32)),
      scratch_shapes=[pltpu.VMEM((FFN_TILE, D_MODEL), F32), pltpu.VMEM((FFN_TILE, D_MODEL), F32)],
      compiler_params=_params(("arbitrary",)),
      name="tail",
  )(x1s, x1p, mixs, mixp, ps, pp, *weights)


def kernel(x_prompt, x_sample, state_conv, state_ssm, state_hgrn, p_prompt, p_sample, norm_ffn1, w_ffn1_up, w_ffn1_down, norm_mix, w_in, conv_w, conv_b, dt_bias, a_log, d_skip, ssd_norm, hg_lb_logits, hg_norm, w_out, norm_ffn2, w_ffn2_up, w_ffn2_down, norm_ple, w_ple_gate, w_ple_proj, ple_post_norm, norm_final):
  bp, seq, _ = x_prompt.shape
  bs, dec_seq, _ = x_sample.shape
  row = lambda v: v.reshape(1, -1).astype(F32)
  per_head = lambda v: jnp.repeat(v.astype(F32), SSD_HEADDIM).reshape(1, SSD_DINNER)
  head_lanes = lambda v: jnp.pad(v.astype(F32), (0, LANES - SSD_HEADS)).reshape(1, LANES)

  w1u, w1d = w_ffn1_up[0].astype(BF16), w_ffn1_down[0].astype(BF16)
  w2u, w2d = w_ffn2_up[0].astype(BF16), w_ffn2_down[0].astype(BF16)
  wall = _split_w_in(jnp.swapaxes(w_in[0], 0, 1))
  prm = MixParams(
      conv_w=conv_w[0].astype(F32), conv_b=row(conv_b[0]), dtb_c=head_lanes(dt_bias[0]),
      alog_c=head_lanes(a_log[0]), dskip_e=per_head(d_skip[0]), ssd_norm=row(ssd_norm[0]),
      lb_logits=hg_lb_logits.astype(F32), hg_norm=row(hg_norm[0]))
  nmix = row(norm_mix[0])
  tail_w = (w_out[0].astype(BF16), row(norm_ffn2[0]), w2u, w2d, row(norm_ple[0]), w_ple_gate[0].astype(BF16),
            w_ple_proj[0].astype(BF16), row(ple_post_norm[0]), row(norm_final))

  xp = x_prompt.reshape(bp * seq, D_MODEL)
  xs = x_sample.reshape(bs * dec_seq, D_MODEL)
  x1s, x1p = _ffn1(xs, xp, row(norm_ffn1[0]), w1u, w1d)

  mixed_p, conv_p, ssm_p, hg_p = _mix_prompt(x1p.reshape(bp, seq, D_MODEL), nmix, wall, prm)
  proj_s = _proj_sample(x1s, nmix, wall)
  mixed_s, conv_s, ssm_s, hg_s = _rec_sample(proj_s, state_conv[0], state_ssm[0], state_hgrn[0], prm, dec_seq)
  mixed_s = mixed_s[:, :dec_seq].reshape(bs * dec_seq, D_MIX)

  y_sample, y_prompt = _tail(x1s, x1p, mixed_s, mixed_p.reshape(bp * seq, D_MIX),
                             p_sample[0].reshape(bs * dec_seq, PLE_DIM), p_prompt[0].reshape(bp * seq, PLE_DIM), *tail_w)

  return (y_prompt.reshape(bp, seq, D_MODEL), y_sample.reshape(bs, dec_seq, D_MODEL), conv_p[None], ssm_p[None],
          hg_p[None], conv_s[None], ssm_s[None], hg_s[None])
```

```python
import functools
from typing import NamedTuple

import jax
import jax.numpy as jnp
from jax import lax
from jax.experimental import pallas as pl
from jax.experimental.pallas import tpu as pltpu

F32 = jnp.float32
BF16 = jnp.bfloat16

D_MODEL = 1024
D_FF = 2816
PLE_DIM = 256
EPS = 1e-6
SSD_HEADS = 16
SSD_HEADDIM = 64
SSD_DINNER = SSD_HEADS * SSD_HEADDIM
SSD_STATE = 128
SSD_GROUPS = 2
SSD_HEADS_PER_GROUP = SSD_HEADS // SSD_GROUPS
SSD_GROUP_WIDTH = SSD_DINNER // SSD_GROUPS
CONV_W = 4
CONV_DIM = SSD_DINNER + 2 * SSD_GROUPS * SSD_STATE
HG_HEADS = 8
HG_DK = 128
HG_DV = 128
HG_WIDTH = HG_HEADS * HG_DV
HG_BLOCK = 16
D_MIX = SSD_DINNER + HG_WIDTH

LANES = 128
SUBLANES = 8
VMEM_LIMIT_BYTES = 56 * 1024 * 1024

CHUNK = 128
SAMPLE_ROWS = 8
SAMPLE_SEQS_PER_STEP = 8
FFN_TILE = 512
OVERLAP_ROWS = 128
FF_CHUNK = 256
N_FF_CHUNKS = D_FF // FF_CHUNK
W_STEPS = N_FF_CHUNKS
UP_BLOCK = 2 * D_FF // W_STEPS

OFF_Z = 0
OFF_XBC = OFF_Z + SSD_DINNER
OFF_Q = OFF_XBC + CONV_DIM
OFF_FR = OFF_Q + HG_WIDTH
OFF_IV = OFF_FR + HG_WIDTH
OFF_OG = OFF_IV + HG_WIDTH
OFF_DT = OFF_OG + HG_WIDTH
PROJ_W = OFF_DT + LANES
PROJ_BLOCK = 256
MIX_PHASE_WEIGHT = 34


class MixParams(NamedTuple):
  conv_w: jax.Array
  conv_b: jax.Array
  dtb_c: jax.Array
  alog_c: jax.Array
  dskip_e: jax.Array
  ssd_norm: jax.Array
  lb_logits: jax.Array
  hg_norm: jax.Array


def _dot(a, b):
  return jnp.dot(a, b, preferred_element_type=F32)


def _dot_nt(a, b):
  return lax.dot_general(a, b, (((1,), (1,)), ((), ())), preferred_element_type=F32)


def _dot_tn(a, b):
  return lax.dot_general(a, b, (((0,), (0,)), ((), ())), preferred_element_type=F32)


NEG_LOG2_E = -1.4426950408889634


def _sigmoid(x):
  return 1.0 / (1.0 + jnp.exp2(x * NEG_LOG2_E))


def _silu(x):
  return x * _sigmoid(x)


def _softplus(x):
  return jnp.maximum(x, 0.0) + jnp.log1p(jnp.exp(-jnp.abs(x)))


def _rmsnorm(x, w):
  ms = jnp.mean(x * x, axis=-1, keepdims=True)
  return x * lax.rsqrt(ms + EPS) * w


def _cumsum_rows(x):
  c, n = x.shape
  g = c // SUBLANES
  x3 = x.reshape(g, SUBLANES, n)
  sub = lax.broadcasted_iota(jnp.int32, (1, SUBLANES, n), 1)
  s = 1
  while s < SUBLANES:
    x3 = x3 + jnp.where(sub >= s, pltpu.roll(x3, s, axis=1), 0.0)
    s *= 2
  if g > 1:
    tot = jnp.broadcast_to(x3[:, SUBLANES - 1:SUBLANES, :], (g, SUBLANES, n))
    offs = [jnp.zeros((1, SUBLANES, n), x.dtype)]
    for k in range(1, g):
      offs.append(offs[-1] + tot[k - 1:k])
    x3 = x3 + jnp.concatenate(offs, axis=0)
  return x3.reshape(c, n)


def _expand_heads(xc):
  c = xc.shape[0]
  first_head = lax.broadcasted_iota(jnp.int32, (1, LANES), 1) < SSD_HEADDIM
  parts = []
  for j in range(SSD_HEADS // 2):
    a = jnp.broadcast_to(xc[:, 2 * j:2 * j + 1], (c, LANES))
    b = jnp.broadcast_to(xc[:, 2 * j + 1:2 * j + 2], (c, LANES))
    parts.append(jnp.where(first_head, a, b))
  return jnp.concatenate(parts, axis=1)


def _col_matrix(row_vec, n_lanes):
  n = row_vec.shape[1]
  return jnp.transpose(jnp.broadcast_to(row_vec, (n_lanes, n)))


def _mix_chunk(proj_ref, *, c, valid, p, xb_ref, ssm_ref, hg_ref):
  cs = max(c, LANES)

  def padr(x):
    if x.shape[0] == cs:
      return x
    return jnp.concatenate([x, jnp.zeros((cs - x.shape[0], x.shape[1]), x.dtype)], axis=0)

  row_valid = None
  if valid < c:
    row_valid = lax.broadcasted_iota(jnp.int32, (c, 1), 0) < valid

  def mask_rows(x):
    return x if row_valid is None else jnp.where(row_valid, x, 0.0)

  cols = lambda off, width: proj_ref[:, off:off + width]

  xbc = cols(OFF_XBC, CONV_DIM)
  xb_ref[SUBLANES:SUBLANES + c, :] = xbc
  conv = p.conv_b + p.conv_w[CONV_W - 1:CONV_W] * xbc
  for j in range(CONV_W - 1):
    conv = conv + p.conv_w[j:j + 1] * xb_ref[SUBLANES - (CONV_W - 1) + j:SUBLANES - (CONV_W - 1) + j + c, :]
  yield 2
  conv = _silu(conv)
  yield 2
  xs = conv[:, :SSD_DINNER]
  bm = conv[:, SSD_DINNER:SSD_DINNER + SSD_GROUPS * SSD_STATE]
  cm = conv[:, SSD_DINNER + SSD_GROUPS * SSD_STATE:]

  head_lane = lax.broadcasted_iota(jnp.int32, (1, LANES), 1) < SSD_HEADS
  dt_c = mask_rows(jnp.where(head_lane, _softplus(cols(OFF_DT, LANES) + p.dtb_c), 0.0))
  acs_c = _cumsum_rows(dt_c * -jnp.exp(p.alog_c))
  last_c = acs_c[c - 1:c, :]
  dec_in = _expand_heads(jnp.exp(acs_c))
  xd_end = xs * _expand_heads(dt_c * jnp.exp(last_c - acs_c))
  yield 2
  acs_t = jnp.transpose(padr(acs_c))
  dt_t = jnp.transpose(padr(dt_c))
  chunk_decay_t = jnp.transpose(jnp.broadcast_to(jnp.exp(last_c), (LANES, LANES)))

  t_idx = lax.broadcasted_iota(jnp.int32, (c, cs), 0)
  s_idx = lax.broadcasted_iota(jnp.int32, (c, cs), 1)
  causal = s_idx <= t_idx
  lane = lax.broadcasted_iota(jnp.int32, (1, LANES), 1)
  first_head = lane < SSD_HEADDIM

  group_heads = [range(g * SSD_HEADS_PER_GROUP, (g + 1) * SSD_HEADS_PER_GROUP) for g in range(SSD_GROUPS)]
  bg_p = [padr(bm[:, g * SSD_STATE:(g + 1) * SSD_STATE]).astype(BF16) for g in range(SSD_GROUPS)]
  cg = [cm[:, g * SSD_STATE:(g + 1) * SSD_STATE].astype(BF16) for g in range(SSD_GROUPS)]
  h0 = [ssm_ref[heads.start:heads.stop].reshape(SSD_GROUP_WIDTH, SSD_STATE) for heads in group_heads]
  cb = [_dot_nt(cg[g], bg_p[g]) for g in range(SSD_GROUPS)]
  y_inter = [_dot_nt(cg[g], h0[g].astype(BF16)) for g in range(SSD_GROUPS)]
  xs_bf = padr(xs).astype(BF16)
  zero_bf = jnp.zeros((cs, LANES), BF16)
  y_intra = []
  for g in range(SSD_GROUPS):
    for j in range(SSD_HEADS_PER_GROUP // 2):
      h_first = g * SSD_HEADS_PER_GROUP + 2 * j
      scores = []
      for h in (h_first, h_first + 1):
        diff = acs_c[:, h:h + 1] - acs_t[h:h + 1, :]
        decay_dt = jnp.where(causal, jnp.exp(diff), 0.0) * dt_t[h:h + 1, :]
        scores.append((cb[g] * decay_dt).astype(BF16))
      pair = xs_bf[:, h_first * SSD_HEADDIM:(h_first + 2) * SSD_HEADDIM]
      rhs = jnp.concatenate([jnp.where(first_head, pair, zero_bf), jnp.where(first_head, zero_bf, pair)], axis=0)
      y_intra.append(_dot(jnp.concatenate(scores, axis=1), rhs))
      yield 1
  for g, heads in enumerate(group_heads):
    xd_g = padr(xd_end[:, g * SSD_GROUP_WIDTH:(g + 1) * SSD_GROUP_WIDTH]).astype(BF16)
    decay_cols = jnp.concatenate(
        [jnp.broadcast_to(chunk_decay_t[h:h + 1, :], (SSD_HEADDIM, SSD_STATE)) for h in heads], axis=0)
    h1 = h0[g] * decay_cols + _dot_tn(xd_g, bg_p[g])
    ssm_ref[heads.start:heads.stop] = h1.reshape(SSD_HEADS_PER_GROUP, SSD_HEADDIM, SSD_STATE)
  y = jnp.concatenate(y_intra, axis=1) + jnp.concatenate(y_inter, axis=1) * dec_in + p.dskip_e * xs
  yield 1
  yg = y * _silu(cols(OFF_Z, SSD_DINNER))
  yield 1
  y_norm = []
  for g in range(SSD_GROUPS):
    blk = yg[:, g * SSD_GROUP_WIDTH:(g + 1) * SSD_GROUP_WIDTH]
    y_norm.append(blk * lax.rsqrt(jnp.mean(blk * blk, axis=-1, keepdims=True) + EPS))
  y_ssd = jnp.concatenate(y_norm, axis=1) * p.ssd_norm

  lg = p.lb_logits
  lg_max = jnp.max(lg, axis=0, keepdims=True)
  lg_exp = jnp.exp(lg - lg_max)
  lb = lg_exp[0:1] / jnp.sum(lg_exp, axis=0, keepdims=True)
  f = lb + (1.0 - lb) * _sigmoid(cols(OFF_FR, HG_WIDTH))
  yield 1
  logf = mask_rows(jnp.log(f))
  kk = mask_rows(1.0 - f)
  qq = _silu(cols(OFF_Q, HG_WIDTH))
  yield 1
  b = _cumsum_rows(logf)
  yield 2
  b_last = b[c - 1:c, :]
  q_inter = (qq * jnp.exp(b)).astype(BF16)
  k_end = kk * jnp.exp(b_last - b)
  state_decay = jnp.exp(b_last)
  yield 1

  levels = []
  m = c // 2
  while m >= HG_BLOCK:
    q_rows, k_rows = [], []
    zero_half = jnp.zeros((m, HG_WIDTH), BF16)
    for i in range(c // (2 * m)):
      lo = i * 2 * m
      b_ref_row = b[lo + m - 1:lo + m, :]
      k_rows += [(kk[lo:lo + m] * jnp.exp(b_ref_row - b[lo:lo + m])).astype(BF16), zero_half]
      q_rows += [zero_half, (qq[lo + m:lo + 2 * m] * jnp.exp(b[lo + m:lo + 2 * m] - b_ref_row)).astype(BF16)]
    shift = m.bit_length() - 1
    mask = ((t_idx >> (shift + 1)) == (s_idx >> (shift + 1))) & (((t_idx >> shift) & 1) == 1) & (
        ((s_idx >> shift) & 1) == 0)
    levels.append((jnp.concatenate(q_rows, axis=0), jnp.concatenate(k_rows, axis=0), mask))
    yield 1
    m //= 2
  blk = min(HG_BLOCK, c)
  q_rows, k_rows = [], []
  for i in range(c // blk):
    lo = i * blk
    b_loc = b[lo:lo + blk] if i == 0 else b[lo:lo + blk] - b[lo - 1:lo, :]
    q_rows.append(qq[lo:lo + blk] * jnp.exp(b_loc))
    k_rows.append(kk[lo:lo + blk] * jnp.exp(-b_loc))
  shift = blk.bit_length() - 1
  levels.append((jnp.concatenate(q_rows, axis=0).astype(BF16), padr(jnp.concatenate(k_rows, axis=0)).astype(BF16),
                 ((t_idx >> shift) == (s_idx >> shift)) & causal))

  yield 2
  head_slices = [slice(h * HG_DK, (h + 1) * HG_DK) for h in range(HG_HEADS)]
  v_bf = padr(cols(OFF_IV, HG_WIDTH)).astype(BF16)
  k_end_bf = padr(k_end).astype(BF16)
  att_bf = []
  for sl in head_slices:
    att = jnp.zeros((c, cs), F32)
    for ql, kl, mk in levels:
      att = jnp.where(mk, _dot_nt(ql[:, sl], kl[:, sl]), att)
    att_bf.append(att.astype(BF16))
    yield 1
  s0 = [hg_ref[h] for h in range(HG_HEADS)]
  o_heads = [_dot(att_bf[h], v_bf[:, sl]) + _dot(q_inter[:, sl], s0[h].astype(BF16))
             for h, sl in enumerate(head_slices)]
  for h, sl in enumerate(head_slices):
    hg_ref[h] = s0[h] * _col_matrix(state_decay[:, sl], HG_DV) + _dot_tn(k_end_bf[:, sl], v_bf[:, sl])
  o_parts = [o_h * lax.rsqrt(jnp.mean(o_h * o_h, axis=-1, keepdims=True) + EPS) for o_h in o_heads]
  o = jnp.concatenate(o_parts, axis=1) * p.hg_norm * _silu(cols(OFF_OG, HG_WIDTH))
  return jnp.concatenate([y_ssd, o], axis=1)


def _load_mix_params(refs):
  return MixParams(*[r[...] for r in refs])


def _drive(gen, between=lambda n: None):
  while True:
    try:
      between(next(gen))
    except StopIteration as done:
      return done.value


def _drive_round_robin(gens):
  results = [None] * len(gens)
  active = list(range(len(gens)))
  while active:
    for idx in list(active):
      try:
        next(gens[idx])
      except StopIteration as done:
        results[idx] = done.value
        active.remove(idx)
  return results


def _stage_ffn_weights(step, wup_f32_ref, wdn_f32_ref, wup_ref, wdn_ref):
  wup_ref[step] = wup_f32_ref[...].astype(BF16)
  wdn_ref[step] = wdn_f32_ref[...].astype(BF16)


def _swiglu(h, up_cols, down_rows, between_chunks=lambda ci: None):
  acc = jnp.zeros((h.shape[0], D_MODEL), F32)
  for ci in range(N_FF_CHUNKS):
    lo = ci * FF_CHUNK
    g = _dot(h, up_cols(lo))
    u = _dot(h, up_cols(D_FF + lo))
    acc = acc + _dot((_silu(g) * u).astype(BF16), down_rows(ci))
    between_chunks(ci)
  return acc


def _first_tile_select(first_ref, rest_ref, first_step):
  return jnp.where(pl.program_id(0) == first_step, first_ref[...], rest_ref[...])


def _ffn1_kernel(xs_ref, xp_ref, nw_ref, wup_f32_ref, wdn_f32_ref, os_ref, op_ref, wup_ref, wdn_ref):
  i = pl.program_id(0)

  @pl.when(i < W_STEPS)
  def _():
    _stage_ffn_weights(i, wup_f32_ref, wdn_f32_ref, wup_ref, wdn_ref)

  @pl.when(i >= W_STEPS)
  def _():
    x = _first_tile_select(xs_ref, xp_ref, W_STEPS)
    h = _rmsnorm(x, nw_ref[...]).astype(BF16)
    up_cols = lambda lo: wup_ref[lo // UP_BLOCK, :, lo % UP_BLOCK:lo % UP_BLOCK + FF_CHUNK]
    op_ref[...] = x + 0.5 * _swiglu(h, up_cols, lambda ci: wdn_ref[ci])

  @pl.when(i == W_STEPS)
  def _():
    os_ref[...] = op_ref[...]


def _split_w_in_kernel(wt_ref, wa_ref, wb_ref, wdt_ref):
  o_dt = SSD_DINNER + CONV_DIM
  o_q = o_dt + SSD_HEADS
  wa_ref[...] = jnp.transpose(wt_ref[0:o_dt, :]).astype(BF16)
  wb_ref[...] = jnp.transpose(wt_ref[o_q:, :]).astype(BF16)
  dt_rows = jnp.concatenate([wt_ref[o_dt:o_q, :], jnp.zeros((LANES - SSD_HEADS, LANES), F32)], axis=0)
  wdt_ref[...] = jnp.transpose(dt_rows).astype(BF16)


def _proj_blocks(w_refs):
  blocks = []
  dst = 0
  for w in w_refs:
    width = w.shape[1]
    blocks += [(w, lo, min(lo + PROJ_BLOCK, width), dst + lo) for lo in range(0, width, PROJ_BLOCK)]
    dst += width
  assert dst == PROJ_W
  return blocks


def _mix_prompt_kernel(x_ref, nmix_ref, wa_ref, wb_ref, wdt_ref, *rest, chunks_per_seq):
  prm_refs = rest[:len(MixParams._fields)]
  mixed_ref, conv_ref, ssm_ref, hg_ref, xb_ref, proj_a, proj_b = rest[len(MixParams._fields):]
  i = pl.program_id(0)
  t_cur = lax.rem(jnp.maximum(i - 1, 0), chunks_per_seq)

  @pl.when(i == 0)
  def _():
    proj_b[...] = jnp.zeros(proj_b.shape, F32)

  @pl.when(t_cur == 0)
  def _():
    xb_ref[0:SUBLANES, :] = jnp.zeros((SUBLANES, CONV_DIM), F32)
    ssm_ref[...] = jnp.zeros(ssm_ref.shape, F32)
    hg_ref[...] = jnp.zeros(hg_ref.shape, F32)

  def step(cur_ref, next_ref):
    h = _rmsnorm(x_ref[0], nmix_ref[...]).astype(BF16)
    pending = _proj_blocks((wa_ref, wb_ref, wdt_ref))
    n_blocks = len(pending)
    credit = [0.0]

    def project_blocks(n):
      credit[0] += n * n_blocks / MIX_PHASE_WEIGHT
      while pending and credit[0] >= 1.0:
        credit[0] -= 1.0
        w_ref, lo, hi, dst = pending.pop(0)
        next_ref[:, dst:dst + hi - lo] = _dot(h, w_ref[:, lo:hi])

    mixed = _drive(_mix_chunk(cur_ref, c=CHUNK, valid=CHUNK, p=_load_mix_params(prm_refs), xb_ref=xb_ref,
                              ssm_ref=ssm_ref.at[0], hg_ref=hg_ref.at[0]), project_blocks)
    project_blocks(MIX_PHASE_WEIGHT)
    mixed_ref[0] = mixed.astype(BF16)
    tail = xb_ref[CHUNK:CHUNK + SUBLANES, :]
    xb_ref[0:SUBLANES, :] = tail

  parity = lax.rem(i, 2)
  pl.when(parity == 0)(functools.partial(step, proj_b, proj_a))
  pl.when(parity == 1)(functools.partial(step, proj_a, proj_b))

  @pl.when((t_cur == chunks_per_seq - 1) & (i > 0))
  def _():
    conv_ref[0] = xb_ref[SUBLANES + CHUNK - (CONV_W - 1):SUBLANES + CHUNK, :]


def _proj_sample_kernel(x_ref, nmix_ref, wa_ref, wb_ref, wdt_ref, o_ref):
  h = _rmsnorm(x_ref[...], nmix_ref[...]).astype(BF16)
  for w_ref, lo, hi, dst in _proj_blocks((wa_ref, wb_ref, wdt_ref)):
    o_ref[:, dst:dst + hi - lo] = _dot(h, w_ref[:, lo:hi])


def _rec_sample_kernel(proj_ref, conv_in_ref, ssm_in_ref, hg_in_ref, *rest, valid):
  prm_refs = rest[:len(MixParams._fields)]
  mixed_ref, conv_ref, ssm_ref, hg_ref, xb_ref, padded_ref = rest[len(MixParams._fields):]
  n_seq = conv_in_ref.shape[0]
  prm = _load_mix_params(prm_refs)
  ssm_ref[...] = ssm_in_ref[...]
  hg_ref[...] = hg_in_ref[...]
  gens = []
  for s in range(n_seq):
    xb_ref[s, 0:SUBLANES, :] = jnp.zeros((SUBLANES, CONV_DIM), F32)
    xb_ref[s, SUBLANES - (CONV_W - 1):SUBLANES, :] = conv_in_ref[s]
    padded_ref[s, 0:valid, :] = proj_ref[s * valid:(s + 1) * valid, :]
    padded_ref[s, valid:SAMPLE_ROWS, :] = jnp.zeros((SAMPLE_ROWS - valid, PROJ_W), F32)
    gens.append(_mix_chunk(padded_ref.at[s], c=SAMPLE_ROWS, valid=valid, p=prm, xb_ref=xb_ref.at[s],
                           ssm_ref=ssm_ref.at[s], hg_ref=hg_ref.at[s]))
  for s, mixed in enumerate(_drive_round_robin(gens)):
    mixed_ref[s] = mixed.astype(BF16)
    conv_ref[s] = xb_ref[s, SUBLANES + valid - (CONV_W - 1):SUBLANES + valid, :]


def _tail_kernel(x1s_ref, x1p_ref, mixs_ref, mixp_ref, ps_ref, pp_ref, wout_ref, nf2_ref, wup_ref, wdn_ref, nple_ref,
                 wgate_ref, wproj_ref, ppost_ref, nfin_ref, os_ref, op_ref, x3_ref, e_ref, *, n_tiles):
  i = pl.program_id(0)
  tile = i

  @pl.when(i == 0)
  def _():
    x3_ref[...] = jnp.zeros(x3_ref.shape, F32)
    e_ref[...] = jnp.zeros(e_ref.shape, F32)

  def finish_rows(lo):
    rows = slice(lo, lo + OVERLAP_ROWS)
    x3 = x3_ref[rows, :]
    gate = _sigmoid(_dot(_rmsnorm(x3, nple_ref[...]).astype(BF16), wgate_ref[...]))
    x4 = x3 + gate * e_ref[rows, :]
    op_ref[rows, :] = _rmsnorm(x4, nfin_ref[...])

  finish_starts = list(range(0, FFN_TILE, OVERLAP_ROWS))

  @pl.when((tile >= 0) & (tile < n_tiles))
  def _():
    x2 = _first_tile_select(x1s_ref, x1p_ref, 0) + _dot(_first_tile_select(mixs_ref, mixp_ref, 0), wout_ref[...])
    h = _rmsnorm(x2, nf2_ref[...]).astype(BF16)
    e = _rmsnorm(_dot(_first_tile_select(ps_ref, pp_ref, 0).astype(BF16), wproj_ref[...]), ppost_ref[...])
    pending = list(finish_starts)
    every = N_FF_CHUNKS // (len(pending) + 1)

    def between_chunks(ci):
      if pending and (ci + 1) % every == 0:
        finish_rows(pending.pop(0))

    x3 = x2 + 0.5 * _swiglu(h, lambda lo: wup_ref[:, lo:lo + FF_CHUNK],
                            lambda ci: wdn_ref[ci * FF_CHUNK:(ci + 1) * FF_CHUNK, :], between_chunks)
    while pending:
      finish_rows(pending.pop(0))
    x3_ref[...] = x3
    e_ref[...] = e

  @pl.when(tile == n_tiles)
  def _():
    for lo in finish_starts:
      finish_rows(lo)

  @pl.when(tile == 1)
  def _():
    os_ref[...] = op_ref[...]


def _resident(shape):
  nd = len(shape)
  return pl.BlockSpec(shape, lambda *_: (0,) * nd, pipeline_mode=pl.Buffered(1))


def _params(semantics, flags=None):
  return pltpu.CompilerParams(dimension_semantics=semantics, vmem_limit_bytes=VMEM_LIMIT_BYTES, flags=flags)


def _sample_then_prompt_specs(width, n_prompt_tiles, first_step):
  sample = pl.BlockSpec((FFN_TILE, width), lambda i: (0, 0))
  prompt = pl.BlockSpec((FFN_TILE, width), lambda i: (jnp.clip(i - first_step - 1, 0, n_prompt_tiles - 1), 0))
  return sample, prompt


def _ffn_weight_specs():
  staged = lambda i: jnp.minimum(i, W_STEPS - 1)
  return [pl.BlockSpec((D_MODEL, UP_BLOCK), lambda i: (0, staged(i))),
          pl.BlockSpec((FF_CHUNK, D_MODEL), lambda i: (staged(i), 0))]


FFN_WEIGHT_SCRATCH = [pltpu.VMEM((W_STEPS, D_MODEL, UP_BLOCK), BF16), pltpu.VMEM((W_STEPS, FF_CHUNK, D_MODEL), BF16)]


def _ffn1(xs, xp, nw, wup, wdn):
  assert xs.shape[0] == FFN_TILE and xp.shape[0] % FFN_TILE == 0
  n_prompt_tiles = xp.shape[0] // FFN_TILE
  specs = _sample_then_prompt_specs(D_MODEL, n_prompt_tiles, first_step=W_STEPS)
  return pl.pallas_call(
      _ffn1_kernel,
      grid=(W_STEPS + 1 + n_prompt_tiles,),
      in_specs=[*specs, _resident(nw.shape), *_ffn_weight_specs()],
      out_specs=specs,
      out_shape=(jax.ShapeDtypeStruct(xs.shape, F32), jax.ShapeDtypeStruct(xp.shape, F32)),
      scratch_shapes=FFN_WEIGHT_SCRATCH,
      compiler_params=_params(("arbitrary",)),
      name="ffn1",
  )(xs, xp, nw, wup, wdn)


def _split_w_in(wt):
  rows = LANES
  width_a = SSD_DINNER + CONV_DIM
  width_b = 4 * HG_WIDTH
  assert wt.shape == (width_a + SSD_HEADS + width_b, D_MODEL)
  out = lambda width: pl.BlockSpec((rows, width), lambda i: (i, 0))
  return pl.pallas_call(
      _split_w_in_kernel,
      grid=(D_MODEL // rows,),
      in_specs=[pl.BlockSpec((wt.shape[0], rows), lambda i: (0, i))],
      out_specs=(out(width_a), out(width_b), out(LANES)),
      out_shape=(jax.ShapeDtypeStruct((D_MODEL, width_a), BF16), jax.ShapeDtypeStruct((D_MODEL, width_b), BF16),
                 jax.ShapeDtypeStruct((D_MODEL, LANES), BF16)),
      compiler_params=_params(("arbitrary",)),
      name="split_w_in",
  )(wt)


def _mix_prompt(x1, nmix, wall, prm):
  bsz, seq, _ = x1.shape
  nt = seq // CHUNK
  n_chunks = bsz * nt
  out_shape = (
      jax.ShapeDtypeStruct((bsz, seq, D_MIX), BF16),
      jax.ShapeDtypeStruct((bsz, CONV_W - 1, CONV_DIM), F32),
      jax.ShapeDtypeStruct((bsz, SSD_HEADS, SSD_HEADDIM, SSD_STATE), F32),
      jax.ShapeDtypeStruct((bsz, HG_HEADS, HG_DK, HG_DV), F32),
  )
  proj_chunk = lambda i: jnp.minimum(i, n_chunks - 1)
  mix_chunk = lambda i: jnp.maximum(i - 1, 0)
  out_specs = (
      pl.BlockSpec((1, CHUNK, D_MIX), lambda i: (mix_chunk(i) // nt, mix_chunk(i) % nt, 0)),
      pl.BlockSpec((1, CONV_W - 1, CONV_DIM), lambda i: (mix_chunk(i) // nt, 0, 0)),
      pl.BlockSpec((1, SSD_HEADS, SSD_HEADDIM, SSD_STATE), lambda i: (mix_chunk(i) // nt, 0, 0, 0)),
      pl.BlockSpec((1, HG_HEADS, HG_DK, HG_DV), lambda i: (mix_chunk(i) // nt, 0, 0, 0)),
  )
  in_specs = [pl.BlockSpec((1, CHUNK, D_MODEL), lambda i: (proj_chunk(i) // nt, proj_chunk(i) % nt, 0)),
              _resident(nmix.shape)] + [_resident(a.shape) for a in (*wall, *prm)]
  return pl.pallas_call(
      functools.partial(_mix_prompt_kernel, chunks_per_seq=nt),
      grid=(n_chunks + 1,),
      in_specs=in_specs,
      out_specs=out_specs,
      out_shape=out_shape,
      scratch_shapes=[pltpu.VMEM((SUBLANES + CHUNK, CONV_DIM), F32), pltpu.VMEM((CHUNK, PROJ_W), F32),
                      pltpu.VMEM((CHUNK, PROJ_W), F32)],
      compiler_params=_params(("arbitrary",)),
      name="mix_prompt",
  )(x1, nmix, *wall, *prm)


def _proj_sample(x1p, nmix, wall):
  n = x1p.shape[0]
  tile = 256
  return pl.pallas_call(
      _proj_sample_kernel,
      grid=(n // tile,),
      in_specs=[pl.BlockSpec((tile, D_MODEL), lambda i: (i, 0)), _resident(nmix.shape)] + [
          _resident(w.shape) for w in wall],
      out_specs=pl.BlockSpec((tile, PROJ_W), lambda i: (i, 0)),
      out_shape=jax.ShapeDtypeStruct((n, PROJ_W), F32),
      compiler_params=_params(("arbitrary",)),
      name="proj_sample",
  )(x1p, nmix, *wall)


def _rec_sample(proj, conv0, ssm0, hg0, prm, valid):
  bsz = conv0.shape[0]
  assert proj.shape[0] == bsz * valid
  out_shape = (
      jax.ShapeDtypeStruct((bsz, SAMPLE_ROWS, D_MIX), BF16),
      jax.ShapeDtypeStruct((bsz, CONV_W - 1, CONV_DIM), F32),
      jax.ShapeDtypeStruct((bsz, SSD_HEADS, SSD_HEADDIM, SSD_STATE), F32),
      jax.ShapeDtypeStruct((bsz, HG_HEADS, HG_DK, HG_DV), F32),
  )
  g = SAMPLE_SEQS_PER_STEP
  state_specs = [
      pl.BlockSpec((g, CONV_W - 1, CONV_DIM), lambda b: (b, 0, 0)),
      pl.BlockSpec((g, SSD_HEADS, SSD_HEADDIM, SSD_STATE), lambda b: (b, 0, 0, 0)),
      pl.BlockSpec((g, HG_HEADS, HG_DK, HG_DV), lambda b: (b, 0, 0, 0)),
  ]
  in_specs = [pl.BlockSpec((g * valid, PROJ_W), lambda b: (b, 0))] + state_specs + [_resident(a.shape) for a in prm]
  out_specs = tuple([pl.BlockSpec((g, SAMPLE_ROWS, D_MIX), lambda b: (b, 0, 0))] + state_specs)
  return pl.pallas_call(
      functools.partial(_rec_sample_kernel, valid=valid),
      grid=(bsz // g,),
      in_specs=in_specs,
      out_specs=out_specs,
      out_shape=out_shape,
      scratch_shapes=[pltpu.VMEM((g, 2 * SUBLANES, CONV_DIM), F32), pltpu.VMEM((g, SAMPLE_ROWS, PROJ_W), F32)],
      compiler_params=_params(("arbitrary",)),
      name="rec_sample",
  )(proj, conv0, ssm0, hg0, *prm)


def _tail(x1s, x1p, mixs, mixp, ps, pp, wout, nf2, wup, wdn, nple, wgate, wproj, ppost, nfin):
  assert x1s.shape[0] == FFN_TILE and x1p.shape[0] % FFN_TILE == 0
  weights = (wout, nf2, wup, wdn, nple, wgate, wproj, ppost, nfin)
  n_prompt_tiles = x1p.shape[0] // FFN_TILE
  n_tiles = n_prompt_tiles + 1

  def in_specs(width):
    sample, prompt = _sample_then_prompt_specs(width, n_prompt_tiles, first_step=0)
    return pl.BlockSpec(sample.block_shape, sample.index_map, pipeline_mode=pl.Buffered(1)), prompt

  out_specs = _sample_then_prompt_specs(D_MODEL, n_prompt_tiles, first_step=1)
  return pl.pallas_call(
      functools.partial(_tail_kernel, n_tiles=n_tiles),
      grid=(n_tiles + 1,),
      in_specs=[*in_specs(D_MODEL), *in_specs(D_MIX), *in_specs(PLE_DIM)] + [_resident(w.shape) for w in weights],
      out_specs=out_specs,
      out_shape=(jax.ShapeDtypeStruct(x1s.shape, F32), jax.ShapeDtypeStruct(x1p.shape, F32)),
      scratch_shapes=[pltpu.VMEM((FFN_TILE, D_MODEL), F32), pltpu.VMEM((FFN_TILE, D_MODEL), F32)],
      compiler_params=_params(("arbitrary",)),
      name="tail",
  )(x1s, x1p, mixs, mixp, ps, pp, *weights)


def kernel(x_prompt, x_sample, state_conv, state_ssm, state_hgrn, p_prompt, p_sample, norm_ffn1, w_ffn1_up, w_ffn1_down, norm_mix, w_in, conv_w, conv_b, dt_bias, a_log, d_skip, ssd_norm, hg_lb_logits, hg_norm, w_out, norm_ffn2, w_ffn2_up, w_ffn2_down, norm_ple, w_ple_gate, w_ple_proj, ple_post_norm, norm_final):
  bp, seq, _ = x_prompt.shape
  bs, dec_seq, _ = x_sample.shape
  row = lambda v: v.reshape(1, -1).astype(F32)
  per_head = lambda v: jnp.repeat(v.astype(F32), SSD_HEADDIM).reshape(1, SSD_DINNER)
  head_lanes = lambda v: jnp.pad(v.astype(F32), (0, LANES - SSD_HEADS)).reshape(1, LANES)

  w1u, w1d = w_ffn1_up[0].astype(F32), w_ffn1_down[0].astype(F32)
  w2u, w2d = w_ffn2_up[0].astype(BF16), w_ffn2_down[0].astype(BF16)
  wall = _split_w_in(jnp.swapaxes(w_in[0], 0, 1))
  prm = MixParams(
      conv_w=conv_w[0].astype(F32), conv_b=row(conv_b[0]), dtb_c=head_lanes(dt_bias[0]),
      alog_c=head_lanes(a_log[0]), dskip_e=per_head(d_skip[0]), ssd_norm=row(ssd_norm[0]),
      lb_logits=hg_lb_logits.astype(F32), hg_norm=row(hg_norm[0]))
  nmix = row(norm_mix[0])
  tail_w = (w_out[0].astype(BF16), row(norm_ffn2[0]), w2u, w2d, row(norm_ple[0]), w_ple_gate[0].astype(BF16),
            w_ple_proj[0].astype(BF16), row(ple_post_norm[0]), row(norm_final))

  xp = x_prompt.reshape(bp * seq, D_MODEL)
  xs = x_sample.reshape(bs * dec_seq, D_MODEL)
  x1s, x1p = _ffn1(xs, xp, row(norm_ffn1[0]), w1u, w1d)

  mixed_p, conv_p, ssm_p, hg_p = _mix_prompt(x1p.reshape(bp, seq, D_MODEL), nmix, wall, prm)
  proj_s = _proj_sample(x1s, nmix, wall)
  mixed_s, conv_s, ssm_s, hg_s = _rec_sample(proj_s, state_conv[0], state_ssm[0], state_hgrn[0], prm, dec_seq)
  mixed_s = mixed_s[:, :dec_seq].reshape(bs * dec_seq, D_MIX)

  y_sample, y_prompt = _tail(x1s, x1p, mixed_s, mixed_p.reshape(bp * seq, D_MIX),
                             p_sample[0].reshape(bs * dec_seq, PLE_DIM), p_prompt[0].reshape(bp * seq, PLE_DIM), *tail_w)

  return (y_prompt.reshape(bp, seq, D_MODEL), y_sample.reshape(bs, dec_seq, D_MODEL), conv_p[None], ssm_p[None],
          hg_p[None], conv_s[None], ssm_s[None], hg_s[None])
```

```python
import functools
from typing import NamedTuple

import jax
import jax.numpy as jnp
from jax import lax
from jax.experimental import pallas as pl
from jax.experimental.pallas import tpu as pltpu

F32 = jnp.float32
BF16 = jnp.bfloat16

D_MODEL = 1024
D_FF = 2816
PLE_DIM = 256
EPS = 1e-6
SSD_HEADS = 16
SSD_HEADDIM = 64
SSD_DINNER = SSD_HEADS * SSD_HEADDIM
SSD_STATE = 128
SSD_GROUPS = 2
SSD_HEADS_PER_GROUP = SSD_HEADS // SSD_GROUPS
SSD_GROUP_WIDTH = SSD_DINNER // SSD_GROUPS
CONV_W = 4
CONV_DIM = SSD_DINNER + 2 * SSD_GROUPS * SSD_STATE
HG_HEADS = 8
HG_DK = 128
HG_DV = 128
HG_WIDTH = HG_HEADS * HG_DV
HG_BLOCK = 16
D_MIX = SSD_DINNER + HG_WIDTH

LANES = 128
SUBLANES = 8
VMEM_LIMIT_BYTES = 56 * 1024 * 1024

CHUNK = 128
SAMPLE_ROWS = 8
SAMPLE_SEQS_PER_STEP = 8
FFN_TILE = 512
OVERLAP_ROWS = 128
FF_CHUNK = 256
N_FF_CHUNKS = D_FF // FF_CHUNK
W_STEPS = N_FF_CHUNKS
UP_BLOCK = 2 * D_FF // W_STEPS

OFF_Z = 0
OFF_XBC = OFF_Z + SSD_DINNER
OFF_Q = OFF_XBC + CONV_DIM
OFF_FR = OFF_Q + HG_WIDTH
OFF_IV = OFF_FR + HG_WIDTH
OFF_OG = OFF_IV + HG_WIDTH
OFF_DT = OFF_OG + HG_WIDTH
PROJ_W = OFF_DT + LANES
PROJ_BLOCK = 256
MIX_PHASE_WEIGHT = 34


class MixParams(NamedTuple):
  conv_w: jax.Array
  conv_b: jax.Array
  dtb_c: jax.Array
  alog_c: jax.Array
  dskip_e: jax.Array
  ssd_norm: jax.Array
  lb_logits: jax.Array
  hg_norm: jax.Array


def _dot(a, b):
  return jnp.dot(a, b, preferred_element_type=F32)


def _dot_nt(a, b):
  return lax.dot_general(a, b, (((1,), (1,)), ((), ())), preferred_element_type=F32)


def _dot_tn(a, b):
  return lax.dot_general(a, b, (((0,), (0,)), ((), ())), preferred_element_type=F32)


NEG_LOG2_E = -1.4426950408889634


def _sigmoid(x):
  return 1.0 / (1.0 + jnp.exp2(x * NEG_LOG2_E))


def _silu(x):
  return x * _sigmoid(x)


def _softplus(x):
  return jnp.maximum(x, 0.0) + jnp.log1p(jnp.exp(-jnp.abs(x)))


def _rmsnorm(x, w):
  ms = jnp.mean(x * x, axis=-1, keepdims=True)
  return x * lax.rsqrt(ms + EPS) * w


def _cumsum_rows(x):
  c, n = x.shape
  g = c // SUBLANES
  x3 = x.reshape(g, SUBLANES, n)
  sub = lax.broadcasted_iota(jnp.int32, (1, SUBLANES, n), 1)
  s = 1
  while s < SUBLANES:
    x3 = x3 + jnp.where(sub >= s, pltpu.roll(x3, s, axis=1), 0.0)
    s *= 2
  if g > 1:
    tot = jnp.broadcast_to(x3[:, SUBLANES - 1:SUBLANES, :], (g, SUBLANES, n))
    offs = [jnp.zeros((1, SUBLANES, n), x.dtype)]
    for k in range(1, g):
      offs.append(offs[-1] + tot[k - 1:k])
    x3 = x3 + jnp.concatenate(offs, axis=0)
  return x3.reshape(c, n)


def _expand_heads(xc):
  c = xc.shape[0]
  first_head = lax.broadcasted_iota(jnp.int32, (1, LANES), 1) < SSD_HEADDIM
  parts = []
  for j in range(SSD_HEADS // 2):
    a = jnp.broadcast_to(xc[:, 2 * j:2 * j + 1], (c, LANES))
    b = jnp.broadcast_to(xc[:, 2 * j + 1:2 * j + 2], (c, LANES))
    parts.append(jnp.where(first_head, a, b))
  return jnp.concatenate(parts, axis=1)


def _col_matrix(row_vec, n_lanes):
  n = row_vec.shape[1]
  return jnp.transpose(jnp.broadcast_to(row_vec, (n_lanes, n)))


def _mix_chunk(proj_ref, *, c, valid, p, xb_ref, ssm_ref, hg_ref):
  cs = max(c, LANES)

  def padr(x):
    if x.shape[0] == cs:
      return x
    return jnp.concatenate([x, jnp.zeros((cs - x.shape[0], x.shape[1]), x.dtype)], axis=0)

  row_valid = None
  if valid < c:
    row_valid = lax.broadcasted_iota(jnp.int32, (c, 1), 0) < valid

  def mask_rows(x):
    return x if row_valid is None else jnp.where(row_valid, x, 0.0)

  cols = lambda off, width: proj_ref[:, off:off + width]

  xbc = cols(OFF_XBC, CONV_DIM)
  xb_ref[SUBLANES:SUBLANES + c, :] = xbc
  conv = p.conv_b + p.conv_w[CONV_W - 1:CONV_W] * xbc
  for j in range(CONV_W - 1):
    conv = conv + p.conv_w[j:j + 1] * xb_ref[SUBLANES - (CONV_W - 1) + j:SUBLANES - (CONV_W - 1) + j + c, :]
  yield 2
  conv = _silu(conv)
  yield 2
  xs = conv[:, :SSD_DINNER]
  bm = conv[:, SSD_DINNER:SSD_DINNER + SSD_GROUPS * SSD_STATE]
  cm = conv[:, SSD_DINNER + SSD_GROUPS * SSD_STATE:]

  head_lane = lax.broadcasted_iota(jnp.int32, (1, LANES), 1) < SSD_HEADS
  dt_c = mask_rows(jnp.where(head_lane, _softplus(cols(OFF_DT, LANES) + p.dtb_c), 0.0))
  acs_c = _cumsum_rows(dt_c * -jnp.exp(p.alog_c))
  last_c = acs_c[c - 1:c, :]
  dec_in = _expand_heads(jnp.exp(acs_c))
  xd_end = xs * _expand_heads(dt_c * jnp.exp(last_c - acs_c))
  yield 2
  acs_t = jnp.transpose(padr(acs_c))
  dt_t = jnp.transpose(padr(dt_c))
  chunk_decay_t = jnp.transpose(jnp.broadcast_to(jnp.exp(last_c), (LANES, LANES)))

  t_idx = lax.broadcasted_iota(jnp.int32, (c, cs), 0)
  s_idx = lax.broadcasted_iota(jnp.int32, (c, cs), 1)
  causal = s_idx <= t_idx
  lane = lax.broadcasted_iota(jnp.int32, (1, LANES), 1)
  first_head = lane < SSD_HEADDIM

  group_heads = [range(g * SSD_HEADS_PER_GROUP, (g + 1) * SSD_HEADS_PER_GROUP) for g in range(SSD_GROUPS)]
  bg_p = [padr(bm[:, g * SSD_STATE:(g + 1) * SSD_STATE]).astype(BF16) for g in range(SSD_GROUPS)]
  cg = [cm[:, g * SSD_STATE:(g + 1) * SSD_STATE].astype(BF16) for g in range(SSD_GROUPS)]
  h0 = [ssm_ref[heads.start:heads.stop].reshape(SSD_GROUP_WIDTH, SSD_STATE) for heads in group_heads]
  cb = [_dot_nt(cg[g], bg_p[g]) for g in range(SSD_GROUPS)]
  y_inter = [_dot_nt(cg[g], h0[g].astype(BF16)) for g in range(SSD_GROUPS)]
  xs_bf = padr(xs).astype(BF16)
  zero_bf = jnp.zeros((cs, LANES), BF16)
  y_intra = []
  for g in range(SSD_GROUPS):
    for j in range(SSD_HEADS_PER_GROUP // 2):
      h_first = g * SSD_HEADS_PER_GROUP + 2 * j
      scores = []
      for h in (h_first, h_first + 1):
        diff = acs_c[:, h:h + 1] - acs_t[h:h + 1, :]
        decay_dt = jnp.where(causal, jnp.exp(diff), 0.0) * dt_t[h:h + 1, :]
        scores.append((cb[g] * decay_dt).astype(BF16))
      pair = xs_bf[:, h_first * SSD_HEADDIM:(h_first + 2) * SSD_HEADDIM]
      rhs = jnp.concatenate([jnp.where(first_head, pair, zero_bf), jnp.where(first_head, zero_bf, pair)], axis=0)
      y_intra.append(_dot(jnp.concatenate(scores, axis=1), rhs))
      yield 1
  for g, heads in enumerate(group_heads):
    xd_g = padr(xd_end[:, g * SSD_GROUP_WIDTH:(g + 1) * SSD_GROUP_WIDTH]).astype(BF16)
    decay_cols = jnp.concatenate(
        [jnp.broadcast_to(chunk_decay_t[h:h + 1, :], (SSD_HEADDIM, SSD_STATE)) for h in heads], axis=0)
    h1 = h0[g] * decay_cols + _dot_tn(xd_g, bg_p[g])
    ssm_ref[heads.start:heads.stop] = h1.reshape(SSD_HEADS_PER_GROUP, SSD_HEADDIM, SSD_STATE)
  y = jnp.concatenate(y_intra, axis=1) + jnp.concatenate(y_inter, axis=1) * dec_in + p.dskip_e * xs
  yield 1
  yg = y * _silu(cols(OFF_Z, SSD_DINNER))
  yield 1
  y_norm = []
  for g in range(SSD_GROUPS):
    blk = yg[:, g * SSD_GROUP_WIDTH:(g + 1) * SSD_GROUP_WIDTH]
    y_norm.append(blk * lax.rsqrt(jnp.mean(blk * blk, axis=-1, keepdims=True) + EPS))
  y_ssd = jnp.concatenate(y_norm, axis=1) * p.ssd_norm

  lg = p.lb_logits
  lg_max = jnp.max(lg, axis=0, keepdims=True)
  lg_exp = jnp.exp(lg - lg_max)
  lb = lg_exp[0:1] / jnp.sum(lg_exp, axis=0, keepdims=True)
  f = lb + (1.0 - lb) * _sigmoid(cols(OFF_FR, HG_WIDTH))
  yield 1
  logf = mask_rows(jnp.log(f))
  kk = mask_rows(1.0 - f)
  qq = _silu(cols(OFF_Q, HG_WIDTH))
  yield 1
  b = _cumsum_rows(logf)
  yield 2
  b_last = b[c - 1:c, :]
  q_inter = (qq * jnp.exp(b)).astype(BF16)
  k_end = kk * jnp.exp(b_last - b)
  state_decay = jnp.exp(b_last)
  yield 1

  levels = []
  m = c // 2
  while m >= HG_BLOCK:
    q_rows, k_rows = [], []
    zero_half = jnp.zeros((m, HG_WIDTH), BF16)
    for i in range(c // (2 * m)):
      lo = i * 2 * m
      b_ref_row = b[lo + m - 1:lo + m, :]
      k_rows += [(kk[lo:lo + m] * jnp.exp(b_ref_row - b[lo:lo + m])).astype(BF16), zero_half]
      q_rows += [zero_half, (qq[lo + m:lo + 2 * m] * jnp.exp(b[lo + m:lo + 2 * m] - b_ref_row)).astype(BF16)]
    shift = m.bit_length() - 1
    mask = ((t_idx >> (shift + 1)) == (s_idx >> (shift + 1))) & (((t_idx >> shift) & 1) == 1) & (
        ((s_idx >> shift) & 1) == 0)
    levels.append((jnp.concatenate(q_rows, axis=0), jnp.concatenate(k_rows, axis=0), mask))
    yield 1
    m //= 2
  blk = min(HG_BLOCK, c)
  q_rows, k_rows = [], []
  for i in range(c // blk):
    lo = i * blk
    b_loc = b[lo:lo + blk] if i == 0 else b[lo:lo + blk] - b[lo - 1:lo, :]
    q_rows.append(qq[lo:lo + blk] * jnp.exp(b_loc))
    k_rows.append(kk[lo:lo + blk] * jnp.exp(-b_loc))
  shift = blk.bit_length() - 1
  levels.append((jnp.concatenate(q_rows, axis=0).astype(BF16), padr(jnp.concatenate(k_rows, axis=0)).astype(BF16),
                 ((t_idx >> shift) == (s_idx >> shift)) & causal))

  yield 2
  head_slices = [slice(h * HG_DK, (h + 1) * HG_DK) for h in range(HG_HEADS)]
  v_bf = padr(cols(OFF_IV, HG_WIDTH)).astype(BF16)
  k_end_bf = padr(k_end).astype(BF16)
  att_bf = []
  for sl in head_slices:
    att = jnp.zeros((c, cs), F32)
    for ql, kl, mk in levels:
      att = jnp.where(mk, _dot_nt(ql[:, sl], kl[:, sl]), att)
    att_bf.append(att.astype(BF16))
    yield 1
  s0 = [hg_ref[h] for h in range(HG_HEADS)]
  o_heads = [_dot(att_bf[h], v_bf[:, sl]) + _dot(q_inter[:, sl], s0[h].astype(BF16))
             for h, sl in enumerate(head_slices)]
  for h, sl in enumerate(head_slices):
    hg_ref[h] = s0[h] * _col_matrix(state_decay[:, sl], HG_DV) + _dot_tn(k_end_bf[:, sl], v_bf[:, sl])
  o_parts = [o_h * lax.rsqrt(jnp.mean(o_h * o_h, axis=-1, keepdims=True) + EPS) for o_h in o_heads]
  o = jnp.concatenate(o_parts, axis=1) * p.hg_norm * _silu(cols(OFF_OG, HG_WIDTH))
  return jnp.concatenate([y_ssd, o], axis=1)


def _load_mix_params(refs):
  return MixParams(*[r[...] for r in refs])


def _drive(gen, between=lambda n: None):
  while True:
    try:
      between(next(gen))
    except StopIteration as done:
      return done.value


def _drive_round_robin(gens):
  results = [None] * len(gens)
  active = list(range(len(gens)))
  while active:
    for idx in list(active):
      try:
        next(gens[idx])
      except StopIteration as done:
        results[idx] = done.value
        active.remove(idx)
  return results


def _stage_ffn_weights(step, wup_f32_ref, wdn_f32_ref, wup_ref, wdn_ref):
  wup_ref[step] = wup_f32_ref[...].astype(BF16)
  wdn_ref[step] = wdn_f32_ref[...].astype(BF16)


def _swiglu(h, up_cols, down_rows, between_chunks=lambda ci: None):
  acc = jnp.zeros((h.shape[0], D_MODEL), F32)
  for ci in range(N_FF_CHUNKS):
    lo = ci * FF_CHUNK
    g = _dot(h, up_cols(lo))
    u = _dot(h, up_cols(D_FF + lo))
    acc = acc + _dot((_silu(g) * u).astype(BF16), down_rows(ci))
    between_chunks(ci)
  return acc


def _first_tile_select(first_ref, rest_ref, first_step):
  return jnp.where(pl.program_id(0) == first_step, first_ref[...], rest_ref[...])


def _ffn1_kernel(xs_ref, xp_ref, nw_ref, wup_f32_ref, wdn_f32_ref, os_ref, op_ref, wup_ref, wdn_ref):
  i = pl.program_id(0)

  @pl.when(i < W_STEPS)
  def _():
    _stage_ffn_weights(i, wup_f32_ref, wdn_f32_ref, wup_ref, wdn_ref)

  @pl.when(i >= W_STEPS)
  def _():
    x = _first_tile_select(xs_ref, xp_ref, W_STEPS)
    h = _rmsnorm(x, nw_ref[...]).astype(BF16)
    up_cols = lambda lo: wup_ref[lo // UP_BLOCK, :, lo % UP_BLOCK:lo % UP_BLOCK + FF_CHUNK]
    op_ref[...] = x + 0.5 * _swiglu(h, up_cols, lambda ci: wdn_ref[ci])

  @pl.when(i == W_STEPS)
  def _():
    os_ref[...] = op_ref[...]


def _split_w_in_kernel(wt_ref, wall_ref):
  o_dt = SSD_DINNER + CONV_DIM
  o_q = o_dt + SSD_HEADS
  wall_ref[:, OFF_Z:OFF_Q] = jnp.transpose(wt_ref[0:o_dt, :]).astype(BF16)
  wall_ref[:, OFF_Q:OFF_DT] = jnp.transpose(wt_ref[o_q:, :]).astype(BF16)
  dt_rows = jnp.concatenate([wt_ref[o_dt:o_q, :], jnp.zeros((LANES - SSD_HEADS, LANES), F32)], axis=0)
  wall_ref[:, OFF_DT:PROJ_W] = jnp.transpose(dt_rows).astype(BF16)


def _proj_blocks():
  return [(lo, min(lo + PROJ_BLOCK, PROJ_W)) for lo in range(0, PROJ_W, PROJ_BLOCK)]


def _mix_prompt_kernel(x_ref, nmix_ref, wall_ref, *rest, chunks_per_seq):
  prm_refs = rest[:len(MixParams._fields)]
  mixed_ref, conv_ref, ssm_ref, hg_ref, xb_ref, proj_a, proj_b = rest[len(MixParams._fields):]
  i = pl.program_id(0)
  t_cur = lax.rem(jnp.maximum(i - 1, 0), chunks_per_seq)

  @pl.when(i == 0)
  def _():
    proj_b[...] = jnp.zeros(proj_b.shape, F32)

  @pl.when(t_cur == 0)
  def _():
    xb_ref[0:SUBLANES, :] = jnp.zeros((SUBLANES, CONV_DIM), F32)
    ssm_ref[...] = jnp.zeros(ssm_ref.shape, F32)
    hg_ref[...] = jnp.zeros(hg_ref.shape, F32)

  def step(cur_ref, next_ref):
    h = _rmsnorm(x_ref[0], nmix_ref[...]).astype(BF16)
    pending = _proj_blocks()
    n_blocks = len(pending)
    credit = [0.0]

    def project_blocks(n):
      credit[0] += n * n_blocks / MIX_PHASE_WEIGHT
      while pending and credit[0] >= 1.0:
        credit[0] -= 1.0
        lo, hi = pending.pop(0)
        next_ref[:, lo:hi] = _dot(h, wall_ref[:, lo:hi])

    mixed = _drive(_mix_chunk(cur_ref, c=CHUNK, valid=CHUNK, p=_load_mix_params(prm_refs), xb_ref=xb_ref,
                              ssm_ref=ssm_ref.at[0], hg_ref=hg_ref.at[0]), project_blocks)
    project_blocks(MIX_PHASE_WEIGHT)
    mixed_ref[0] = mixed.astype(BF16)
    tail = xb_ref[CHUNK:CHUNK + SUBLANES, :]
    xb_ref[0:SUBLANES, :] = tail

  parity = lax.rem(i, 2)
  pl.when(parity == 0)(functools.partial(step, proj_b, proj_a))
  pl.when(parity == 1)(functools.partial(step, proj_a, proj_b))

  @pl.when((t_cur == chunks_per_seq - 1) & (i > 0))
  def _():
    conv_ref[0] = xb_ref[SUBLANES + CHUNK - (CONV_W - 1):SUBLANES + CHUNK, :]


def _proj_sample_kernel(x_ref, nmix_ref, wall_ref, o_ref):
  h = _rmsnorm(x_ref[...], nmix_ref[...]).astype(BF16)
  for lo, hi in _proj_blocks():
    o_ref[:, lo:hi] = _dot(h, wall_ref[:, lo:hi])


def _rec_sample_kernel(proj_ref, conv_in_ref, ssm_in_ref, hg_in_ref, *rest, valid):
  prm_refs = rest[:len(MixParams._fields)]
  mixed_ref, conv_ref, ssm_ref, hg_ref, xb_ref, padded_ref = rest[len(MixParams._fields):]
  n_seq = conv_in_ref.shape[0]
  prm = _load_mix_params(prm_refs)
  ssm_ref[...] = ssm_in_ref[...]
  hg_ref[...] = hg_in_ref[...]
  gens = []
  for s in range(n_seq):
    xb_ref[s, 0:SUBLANES, :] = jnp.zeros((SUBLANES, CONV_DIM), F32)
    xb_ref[s, SUBLANES - (CONV_W - 1):SUBLANES, :] = conv_in_ref[s]
    padded_ref[s, 0:valid, :] = proj_ref[s * valid:(s + 1) * valid, :]
    padded_ref[s, valid:SAMPLE_ROWS, :] = jnp.zeros((SAMPLE_ROWS - valid, PROJ_W), F32)
    gens.append(_mix_chunk(padded_ref.at[s], c=SAMPLE_ROWS, valid=valid, p=prm, xb_ref=xb_ref.at[s],
                           ssm_ref=ssm_ref.at[s], hg_ref=hg_ref.at[s]))
  for s, mixed in enumerate(_drive_round_robin(gens)):
    mixed_ref[s] = mixed.astype(BF16)
    conv_ref[s] = xb_ref[s, SUBLANES + valid - (CONV_W - 1):SUBLANES + valid, :]


def _tail_kernel(x1s_ref, x1p_ref, mixs_ref, mixp_ref, ps_ref, pp_ref, wout_ref, nf2_ref, wup_ref, wdn_ref, nple_ref,
                 wgate_ref, wproj_ref, ppost_ref, nfin_ref, os_ref, op_ref, x3_ref, e_ref, *, n_tiles):
  i = pl.program_id(0)
  tile = i

  @pl.when(i == 0)
  def _():
    x3_ref[...] = jnp.zeros(x3_ref.shape, F32)
    e_ref[...] = jnp.zeros(e_ref.shape, F32)

  def finish_rows(lo):
    rows = slice(lo, lo + OVERLAP_ROWS)
    x3 = x3_ref[rows, :]
    gate = _sigmoid(_dot(_rmsnorm(x3, nple_ref[...]).astype(BF16), wgate_ref[...]))
    x4 = x3 + gate * e_ref[rows, :]
    op_ref[rows, :] = _rmsnorm(x4, nfin_ref[...])

  finish_starts = list(range(0, FFN_TILE, OVERLAP_ROWS))

  @pl.when((tile >= 0) & (tile < n_tiles))
  def _():
    x2 = _first_tile_select(x1s_ref, x1p_ref, 0) + _dot(_first_tile_select(mixs_ref, mixp_ref, 0), wout_ref[...])
    h = _rmsnorm(x2, nf2_ref[...]).astype(BF16)
    e = _rmsnorm(_dot(_first_tile_select(ps_ref, pp_ref, 0).astype(BF16), wproj_ref[...]), ppost_ref[...])
    pending = list(finish_starts)
    every = N_FF_CHUNKS // (len(pending) + 1)

    def between_chunks(ci):
      if pending and (ci + 1) % every == 0:
        finish_rows(pending.pop(0))

    x3 = x2 + 0.5 * _swiglu(h, lambda lo: wup_ref[:, lo:lo + FF_CHUNK],
                            lambda ci: wdn_ref[ci * FF_CHUNK:(ci + 1) * FF_CHUNK, :], between_chunks)
    while pending:
      finish_rows(pending.pop(0))
    x3_ref[...] = x3
    e_ref[...] = e

  @pl.when(tile == n_tiles)
  def _():
    for lo in finish_starts:
      finish_rows(lo)

  @pl.when(tile == 1)
  def _():
    os_ref[...] = op_ref[...]


def _resident(shape):
  nd = len(shape)
  return pl.BlockSpec(shape, lambda *_: (0,) * nd, pipeline_mode=pl.Buffered(1))


def _params(semantics, flags=None):
  return pltpu.CompilerParams(dimension_semantics=semantics, vmem_limit_bytes=VMEM_LIMIT_BYTES, flags=flags)


def _sample_then_prompt_specs(width, n_prompt_tiles, first_step):
  sample = pl.BlockSpec((FFN_TILE, width), lambda i: (0, 0))
  prompt = pl.BlockSpec((FFN_TILE, width), lambda i: (jnp.clip(i - first_step - 1, 0, n_prompt_tiles - 1), 0))
  return sample, prompt


def _ffn_weight_specs():
  staged = lambda i: jnp.minimum(i, W_STEPS - 1)
  return [pl.BlockSpec((D_MODEL, UP_BLOCK), lambda i: (0, staged(i))),
          pl.BlockSpec((FF_CHUNK, D_MODEL), lambda i: (staged(i), 0))]


FFN_WEIGHT_SCRATCH = [pltpu.VMEM((W_STEPS, D_MODEL, UP_BLOCK), BF16), pltpu.VMEM((W_STEPS, FF_CHUNK, D_MODEL), BF16)]


def _ffn1(xs, xp, nw, wup, wdn):
  assert xs.shape[0] == FFN_TILE and xp.shape[0] % FFN_TILE == 0
  n_prompt_tiles = xp.shape[0] // FFN_TILE
  specs = _sample_then_prompt_specs(D_MODEL, n_prompt_tiles, first_step=W_STEPS)
  return pl.pallas_call(
      _ffn1_kernel,
      grid=(W_STEPS + 1 + n_prompt_tiles,),
      in_specs=[*specs, _resident(nw.shape), *_ffn_weight_specs()],
      out_specs=specs,
      out_shape=(jax.ShapeDtypeStruct(xs.shape, F32), jax.ShapeDtypeStruct(xp.shape, F32)),
      scratch_shapes=FFN_WEIGHT_SCRATCH,
      compiler_params=_params(("arbitrary",)),
      name="ffn1",
  )(xs, xp, nw, wup, wdn)


def _split_w_in(wt):
  rows = LANES
  assert wt.shape == (SSD_DINNER + CONV_DIM + SSD_HEADS + 4 * HG_WIDTH, D_MODEL)
  return pl.pallas_call(
      _split_w_in_kernel,
      grid=(D_MODEL // rows,),
      in_specs=[pl.BlockSpec((wt.shape[0], rows), lambda i: (0, i))],
      out_specs=pl.BlockSpec((rows, PROJ_W), lambda i: (i, 0)),
      out_shape=jax.ShapeDtypeStruct((D_MODEL, PROJ_W), BF16),
      compiler_params=_params(("arbitrary",)),
      name="split_w_in",
  )(wt)


def _mix_prompt(x1, nmix, wall, prm):
  bsz, seq, _ = x1.shape
  nt = seq // CHUNK
  n_chunks = bsz * nt
  out_shape = (
      jax.ShapeDtypeStruct((bsz, seq, D_MIX), BF16),
      jax.ShapeDtypeStruct((bsz, CONV_W - 1, CONV_DIM), F32),
      jax.ShapeDtypeStruct((bsz, SSD_HEADS, SSD_HEADDIM, SSD_STATE), F32),
      jax.ShapeDtypeStruct((bsz, HG_HEADS, HG_DK, HG_DV), F32),
  )
  proj_chunk = lambda i: jnp.minimum(i, n_chunks - 1)
  mix_chunk = lambda i: jnp.maximum(i - 1, 0)
  out_specs = (
      pl.BlockSpec((1, CHUNK, D_MIX), lambda i: (mix_chunk(i) // nt, mix_chunk(i) % nt, 0)),
      pl.BlockSpec((1, CONV_W - 1, CONV_DIM), lambda i: (mix_chunk(i) // nt, 0, 0)),
      pl.BlockSpec((1, SSD_HEADS, SSD_HEADDIM, SSD_STATE), lambda i: (mix_chunk(i) // nt, 0, 0, 0)),
      pl.BlockSpec((1, HG_HEADS, HG_DK, HG_DV), lambda i: (mix_chunk(i) // nt, 0, 0, 0)),
  )
  in_specs = [pl.BlockSpec((1, CHUNK, D_MODEL), lambda i: (proj_chunk(i) // nt, proj_chunk(i) % nt, 0)),
              _resident(nmix.shape)] + [_resident(a.shape) for a in (wall, *prm)]
  return pl.pallas_call(
      functools.partial(_mix_prompt_kernel, chunks_per_seq=nt),
      grid=(n_chunks + 1,),
      in_specs=in_specs,
      out_specs=out_specs,
      out_shape=out_shape,
      scratch_shapes=[pltpu.VMEM((SUBLANES + CHUNK, CONV_DIM), F32), pltpu.VMEM((CHUNK, PROJ_W), F32),
                      pltpu.VMEM((CHUNK, PROJ_W), F32)],
      compiler_params=_params(("arbitrary",)),
      name="mix_prompt",
  )(x1, nmix, wall, *prm)


def _proj_sample(x1p, nmix, wall):
  n = x1p.shape[0]
  tile = 256
  return pl.pallas_call(
      _proj_sample_kernel,
      grid=(n // tile,),
      in_specs=[pl.BlockSpec((tile, D_MODEL), lambda i: (i, 0)), _resident(nmix.shape), _resident(wall.shape)],
      out_specs=pl.BlockSpec((tile, PROJ_W), lambda i: (i, 0)),
      out_shape=jax.ShapeDtypeStruct((n, PROJ_W), F32),
      compiler_params=_params(("arbitrary",)),
      name="proj_sample",
  )(x1p, nmix, wall)


def _rec_sample(proj, conv0, ssm0, hg0, prm, valid):
  bsz = conv0.shape[0]
  assert proj.shape[0] == bsz * valid
  out_shape = (
      jax.ShapeDtypeStruct((bsz, SAMPLE_ROWS, D_MIX), BF16),
      jax.ShapeDtypeStruct((bsz, CONV_W - 1, CONV_DIM), F32),
      jax.ShapeDtypeStruct((bsz, SSD_HEADS, SSD_HEADDIM, SSD_STATE), F32),
      jax.ShapeDtypeStruct((bsz, HG_HEADS, HG_DK, HG_DV), F32),
  )
  g = SAMPLE_SEQS_PER_STEP
  state_specs = [
      pl.BlockSpec((g, CONV_W - 1, CONV_DIM), lambda b: (b, 0, 0)),
      pl.BlockSpec((g, SSD_HEADS, SSD_HEADDIM, SSD_STATE), lambda b: (b, 0, 0, 0)),
      pl.BlockSpec((g, HG_HEADS, HG_DK, HG_DV), lambda b: (b, 0, 0, 0)),
  ]
  in_specs = [pl.BlockSpec((g * valid, PROJ_W), lambda b: (b, 0))] + state_specs + [_resident(a.shape) for a in prm]
  out_specs = tuple([pl.BlockSpec((g, SAMPLE_ROWS, D_MIX), lambda b: (b, 0, 0))] + state_specs)
  return pl.pallas_call(
      functools.partial(_rec_sample_kernel, valid=valid),
      grid=(bsz // g,),
      in_specs=in_specs,
      out_specs=out_specs,
      out_shape=out_shape,
      scratch_shapes=[pltpu.VMEM((g, 2 * SUBLANES, CONV_DIM), F32), pltpu.VMEM((g, SAMPLE_ROWS, PROJ_W), F32)],
      compiler_params=_params(("arbitrary",)),
      name="rec_sample",
  )(proj, conv0, ssm0, hg0, *prm)


def _tail(x1s, x1p, mixs, mixp, ps, pp, wout, nf2, wup, wdn, nple, wgate, wproj, ppost, nfin):
  assert x1s.shape[0] == FFN_TILE and x1p.shape[0] % FFN_TILE == 0
  weights = (wout, nf2, wup, wdn, nple, wgate, wproj, ppost, nfin)
  n_prompt_tiles = x1p.shape[0] // FFN_TILE
  n_tiles = n_prompt_tiles + 1

  def in_specs(width):
    sample, prompt = _sample_then_prompt_specs(width, n_prompt_tiles, first_step=0)
    return pl.BlockSpec(sample.block_shape, sample.index_map, pipeline_mode=pl.Buffered(1)), prompt

  out_specs = _sample_then_prompt_specs(D_MODEL, n_prompt_tiles, first_step=1)
  return pl.pallas_call(
      functools.partial(_tail_kernel, n_tiles=n_tiles),
      grid=(n_tiles + 1,),
      in_specs=[*in_specs(D_MODEL), *in_specs(D_MIX), *in_specs(PLE_DIM)] + [_resident(w.shape) for w in weights],
      out_specs=out_specs,
      out_shape=(jax.ShapeDtypeStruct(x1s.shape, F32), jax.ShapeDtypeStruct(x1p.shape, F32)),
      scratch_shapes=[pltpu.VMEM((FFN_TILE, D_MODEL), F32), pltpu.VMEM((FFN_TILE, D_MODEL), F32)],
      compiler_params=_params(("arbitrary",)),
      name="tail",
  )(x1s, x1p, mixs, mixp, ps, pp, *weights)


def kernel(x_prompt, x_sample, state_conv, state_ssm, state_hgrn, p_prompt, p_sample, norm_ffn1, w_ffn1_up, w_ffn1_down, norm_mix, w_in, conv_w, conv_b, dt_bias, a_log, d_skip, ssd_norm, hg_lb_logits, hg_norm, w_out, norm_ffn2, w_ffn2_up, w_ffn2_down, norm_ple, w_ple_gate, w_ple_proj, ple_post_norm, norm_final):
  bp, seq, _ = x_prompt.shape
  bs, dec_seq, _ = x_sample.shape
  row = lambda v: v.reshape(1, -1).astype(F32)
  per_head = lambda v: jnp.repeat(v.astype(F32), SSD_HEADDIM).reshape(1, SSD_DINNER)
  head_lanes = lambda v: jnp.pad(v.astype(F32), (0, LANES - SSD_HEADS)).reshape(1, LANES)

  w1u, w1d = w_ffn1_up[0].astype(F32), w_ffn1_down[0].astype(F32)
  w2u, w2d = w_ffn2_up[0].astype(BF16), w_ffn2_down[0].astype(BF16)
  wall = _split_w_in(jnp.swapaxes(w_in[0], 0, 1))
  prm = MixParams(
      conv_w=conv_w[0].astype(F32), conv_b=row(conv_b[0]), dtb_c=head_lanes(dt_bias[0]),
      alog_c=head_lanes(a_log[0]), dskip_e=per_head(d_skip[0]), ssd_norm=row(ssd_norm[0]),
      lb_logits=hg_lb_logits.astype(F32), hg_norm=row(hg_norm[0]))
  nmix = row(norm_mix[0])
  tail_w = (w_out[0].astype(BF16), row(norm_ffn2[0]), w2u, w2d, row(norm_ple[0]), w_ple_gate[0].astype(BF16),
            w_ple_proj[0].astype(BF16), row(ple_post_norm[0]), row(norm_final))

  xp = x_prompt.reshape(bp * seq, D_MODEL)
  xs = x_sample.reshape(bs * dec_seq, D_MODEL)
  x1s, x1p = _ffn1(xs, xp, row(norm_ffn1[0]), w1u, w1d)

  mixed_p, conv_p, ssm_p, hg_p = _mix_prompt(x1p.reshape(bp, seq, D_MODEL), nmix, wall, prm)
  proj_s = _proj_sample(x1s, nmix, wall)
  mixed_s, conv_s, ssm_s, hg_s = _rec_sample(proj_s, state_conv[0], state_ssm[0], state_hgrn[0], prm, dec_seq)
  mixed_s = mixed_s[:, :dec_seq].reshape(bs * dec_seq, D_MIX)

  y_sample, y_prompt = _tail(x1s, x1p, mixed_s, mixed_p.reshape(bp * seq, D_MIX),
                             p_sample[0].reshape(bs * dec_seq, PLE_DIM), p_prompt[0].reshape(bp * seq, PLE_DIM), *tail_w)

  return (y_prompt.reshape(bp, seq, D_MODEL), y_sample.reshape(bs, dec_seq, D_MODEL), conv_p[None], ssm_p[None],
          hg_p[None], conv_s[None], ssm_s[None], hg_s[None])
```

```python
import functools
from typing import NamedTuple

import jax
import jax.numpy as jnp
from jax import lax
from jax.experimental import pallas as pl
from jax.experimental.pallas import tpu as pltpu

F32 = jnp.float32
BF16 = jnp.bfloat16

D_MODEL = 1024
D_FF = 2816
PLE_DIM = 256
EPS = 1e-6
SSD_HEADS = 16
SSD_HEADDIM = 64
SSD_DINNER = SSD_HEADS * SSD_HEADDIM
SSD_STATE = 128
SSD_GROUPS = 2
SSD_HEADS_PER_GROUP = SSD_HEADS // SSD_GROUPS
SSD_GROUP_WIDTH = SSD_DINNER // SSD_GROUPS
CONV_W = 4
CONV_DIM = SSD_DINNER + 2 * SSD_GROUPS * SSD_STATE
HG_HEADS = 8
HG_DK = 128
HG_DV = 128
HG_WIDTH = HG_HEADS * HG_DV
HG_BLOCK = 16
D_MIX = SSD_DINNER + HG_WIDTH

LANES = 128
SUBLANES = 8
VMEM_LIMIT_BYTES = 56 * 1024 * 1024

CHUNK = 128
SAMPLE_ROWS = 8
SAMPLE_SEQS_PER_STEP = 8
FFN_TILE = 512
FF_CHUNK = 256
N_FF_CHUNKS = D_FF // FF_CHUNK
W_STEPS = N_FF_CHUNKS
UP_BLOCK = 2 * D_FF // W_STEPS

OFF_Z = 0
OFF_XBC = OFF_Z + SSD_DINNER
OFF_Q = OFF_XBC + CONV_DIM
OFF_FR = OFF_Q + HG_WIDTH
OFF_IV = OFF_FR + HG_WIDTH
OFF_OG = OFF_IV + HG_WIDTH
OFF_DT = OFF_OG + HG_WIDTH
PROJ_W = OFF_DT + LANES
PROJ_BLOCK = 256
MIX_PHASE_WEIGHT = 34


class MixParams(NamedTuple):
  conv_w: jax.Array
  conv_b: jax.Array
  dtb_c: jax.Array
  alog_c: jax.Array
  dskip_e: jax.Array
  ssd_norm: jax.Array
  lb_logits: jax.Array
  hg_norm: jax.Array


def _dot(a, b):
  return jnp.dot(a, b, preferred_element_type=F32)


def _dot_nt(a, b):
  return lax.dot_general(a, b, (((1,), (1,)), ((), ())), preferred_element_type=F32)


def _dot_tn(a, b):
  return lax.dot_general(a, b, (((0,), (0,)), ((), ())), preferred_element_type=F32)


NEG_LOG2_E = -1.4426950408889634


def _sigmoid(x):
  return 1.0 / (1.0 + jnp.exp2(x * NEG_LOG2_E))


def _silu(x):
  return x * _sigmoid(x)


def _softplus(x):
  return jnp.maximum(x, 0.0) + jnp.log1p(jnp.exp(-jnp.abs(x)))


def _rmsnorm(x, w):
  ms = jnp.mean(x * x, axis=-1, keepdims=True)
  return x * lax.rsqrt(ms + EPS) * w


def _cumsum_rows(x):
  c, n = x.shape
  g = c // SUBLANES
  x3 = x.reshape(g, SUBLANES, n)
  sub = lax.broadcasted_iota(jnp.int32, (1, SUBLANES, n), 1)
  s = 1
  while s < SUBLANES:
    x3 = x3 + jnp.where(sub >= s, pltpu.roll(x3, s, axis=1), 0.0)
    s *= 2
  if g > 1:
    tot = jnp.broadcast_to(x3[:, SUBLANES - 1:SUBLANES, :], (g, SUBLANES, n))
    offs = [jnp.zeros((1, SUBLANES, n), x.dtype)]
    for k in range(1, g):
      offs.append(offs[-1] + tot[k - 1:k])
    x3 = x3 + jnp.concatenate(offs, axis=0)
  return x3.reshape(c, n)


def _expand_heads(xc):
  c = xc.shape[0]
  first_head = lax.broadcasted_iota(jnp.int32, (1, LANES), 1) < SSD_HEADDIM
  parts = []
  for j in range(SSD_HEADS // 2):
    a = jnp.broadcast_to(xc[:, 2 * j:2 * j + 1], (c, LANES))
    b = jnp.broadcast_to(xc[:, 2 * j + 1:2 * j + 2], (c, LANES))
    parts.append(jnp.where(first_head, a, b))
  return jnp.concatenate(parts, axis=1)


def _col_matrix(row_vec, n_lanes):
  n = row_vec.shape[1]
  return jnp.transpose(jnp.broadcast_to(row_vec, (n_lanes, n)))


def _mix_chunk(proj_ref, *, c, valid, p, xb_ref, ssm_ref, hg_ref):
  cs = max(c, LANES)

  def padr(x):
    if x.shape[0] == cs:
      return x
    return jnp.concatenate([x, jnp.zeros((cs - x.shape[0], x.shape[1]), x.dtype)], axis=0)

  row_valid = None
  if valid < c:
    row_valid = lax.broadcasted_iota(jnp.int32, (c, 1), 0) < valid

  def mask_rows(x):
    return x if row_valid is None else jnp.where(row_valid, x, 0.0)

  cols = lambda off, width: proj_ref[:, off:off + width]

  xbc = cols(OFF_XBC, CONV_DIM)
  xb_ref[SUBLANES:SUBLANES + c, :] = xbc
  conv = p.conv_b + p.conv_w[CONV_W - 1:CONV_W] * xbc
  for j in range(CONV_W - 1):
    conv = conv + p.conv_w[j:j + 1] * xb_ref[SUBLANES - (CONV_W - 1) + j:SUBLANES - (CONV_W - 1) + j + c, :]
  yield 2
  conv = _silu(conv)
  yield 2
  xs = conv[:, :SSD_DINNER]
  bm = conv[:, SSD_DINNER:SSD_DINNER + SSD_GROUPS * SSD_STATE]
  cm = conv[:, SSD_DINNER + SSD_GROUPS * SSD_STATE:]

  head_lane = lax.broadcasted_iota(jnp.int32, (1, LANES), 1) < SSD_HEADS
  dt_c = mask_rows(jnp.where(head_lane, _softplus(cols(OFF_DT, LANES) + p.dtb_c), 0.0))
  acs_c = _cumsum_rows(dt_c * -jnp.exp(p.alog_c))
  last_c = acs_c[c - 1:c, :]
  dec_in = _expand_heads(jnp.exp(acs_c))
  xd_end = xs * _expand_heads(dt_c * jnp.exp(last_c - acs_c))
  yield 2
  acs_t = jnp.transpose(padr(acs_c))
  dt_t = jnp.transpose(padr(dt_c))
  chunk_decay_t = jnp.transpose(jnp.broadcast_to(jnp.exp(last_c), (LANES, LANES)))

  t_idx = lax.broadcasted_iota(jnp.int32, (c, cs), 0)
  s_idx = lax.broadcasted_iota(jnp.int32, (c, cs), 1)
  causal = s_idx <= t_idx
  lane = lax.broadcasted_iota(jnp.int32, (1, LANES), 1)
  first_head = lane < SSD_HEADDIM

  group_heads = [range(g * SSD_HEADS_PER_GROUP, (g + 1) * SSD_HEADS_PER_GROUP) for g in range(SSD_GROUPS)]
  bg_p = [padr(bm[:, g * SSD_STATE:(g + 1) * SSD_STATE]).astype(BF16) for g in range(SSD_GROUPS)]
  cg = [cm[:, g * SSD_STATE:(g + 1) * SSD_STATE].astype(BF16) for g in range(SSD_GROUPS)]
  h0 = [ssm_ref[heads.start:heads.stop].reshape(SSD_GROUP_WIDTH, SSD_STATE) for heads in group_heads]
  cb = [_dot_nt(cg[g], bg_p[g]) for g in range(SSD_GROUPS)]
  y_inter = [_dot_nt(cg[g], h0[g].astype(BF16)) for g in range(SSD_GROUPS)]
  xs_bf = padr(xs).astype(BF16)
  zero_bf = jnp.zeros((cs, LANES), BF16)
  y_intra = []
  for g in range(SSD_GROUPS):
    for j in range(SSD_HEADS_PER_GROUP // 2):
      h_first = g * SSD_HEADS_PER_GROUP + 2 * j
      scores = []
      for h in (h_first, h_first + 1):
        diff = acs_c[:, h:h + 1] - acs_t[h:h + 1, :]
        decay_dt = jnp.where(causal, jnp.exp(diff), 0.0) * dt_t[h:h + 1, :]
        scores.append((cb[g] * decay_dt).astype(BF16))
      pair = xs_bf[:, h_first * SSD_HEADDIM:(h_first + 2) * SSD_HEADDIM]
      rhs = jnp.concatenate([jnp.where(first_head, pair, zero_bf), jnp.where(first_head, zero_bf, pair)], axis=0)
      y_intra.append(_dot(jnp.concatenate(scores, axis=1), rhs))
      yield 1
  for g, heads in enumerate(group_heads):
    xd_g = padr(xd_end[:, g * SSD_GROUP_WIDTH:(g + 1) * SSD_GROUP_WIDTH]).astype(BF16)
    decay_cols = jnp.concatenate(
        [jnp.broadcast_to(chunk_decay_t[h:h + 1, :], (SSD_HEADDIM, SSD_STATE)) for h in heads], axis=0)
    h1 = h0[g] * decay_cols + _dot_tn(xd_g, bg_p[g])
    ssm_ref[heads.start:heads.stop] = h1.reshape(SSD_HEADS_PER_GROUP, SSD_HEADDIM, SSD_STATE)
  y = jnp.concatenate(y_intra, axis=1) + jnp.concatenate(y_inter, axis=1) * dec_in + p.dskip_e * xs
  yield 1
  yg = y * _silu(cols(OFF_Z, SSD_DINNER))
  yield 1
  y_norm = []
  for g in range(SSD_GROUPS):
    blk = yg[:, g * SSD_GROUP_WIDTH:(g + 1) * SSD_GROUP_WIDTH]
    y_norm.append(blk * lax.rsqrt(jnp.mean(blk * blk, axis=-1, keepdims=True) + EPS))
  y_ssd = jnp.concatenate(y_norm, axis=1) * p.ssd_norm

  lg = p.lb_logits
  lg_max = jnp.max(lg, axis=0, keepdims=True)
  lg_exp = jnp.exp(lg - lg_max)
  lb = lg_exp[0:1] / jnp.sum(lg_exp, axis=0, keepdims=True)
  f = lb + (1.0 - lb) * _sigmoid(cols(OFF_FR, HG_WIDTH))
  yield 1
  logf = mask_rows(jnp.log(f))
  kk = mask_rows(1.0 - f)
  qq = _silu(cols(OFF_Q, HG_WIDTH))
  yield 1
  b = _cumsum_rows(logf)
  yield 2
  b_last = b[c - 1:c, :]
  q_inter = (qq * jnp.exp(b)).astype(BF16)
  k_end = kk * jnp.exp(b_last - b)
  state_decay = jnp.exp(b_last)
  yield 1

  levels = []
  m = c // 2
  while m >= HG_BLOCK:
    q_rows, k_rows = [], []
    zero_half = jnp.zeros((m, HG_WIDTH), BF16)
    for i in range(c // (2 * m)):
      lo = i * 2 * m
      b_ref_row = b[lo + m - 1:lo + m, :]
      k_rows += [(kk[lo:lo + m] * jnp.exp(b_ref_row - b[lo:lo + m])).astype(BF16), zero_half]
      q_rows += [zero_half, (qq[lo + m:lo + 2 * m] * jnp.exp(b[lo + m:lo + 2 * m] - b_ref_row)).astype(BF16)]
    shift = m.bit_length() - 1
    mask = ((t_idx >> (shift + 1)) == (s_idx >> (shift + 1))) & (((t_idx >> shift) & 1) == 1) & (
        ((s_idx >> shift) & 1) == 0)
    levels.append((jnp.concatenate(q_rows, axis=0), jnp.concatenate(k_rows, axis=0), mask))
    yield 1
    m //= 2
  blk = min(HG_BLOCK, c)
  q_rows, k_rows = [], []
  for i in range(c // blk):
    lo = i * blk
    b_loc = b[lo:lo + blk] if i == 0 else b[lo:lo + blk] - b[lo - 1:lo, :]
    q_rows.append(qq[lo:lo + blk] * jnp.exp(b_loc))
    k_rows.append(kk[lo:lo + blk] * jnp.exp(-b_loc))
  shift = blk.bit_length() - 1
  levels.append((jnp.concatenate(q_rows, axis=0).astype(BF16), padr(jnp.concatenate(k_rows, axis=0)).astype(BF16),
                 ((t_idx >> shift) == (s_idx >> shift)) & causal))

  yield 2
  head_slices = [slice(h * HG_DK, (h + 1) * HG_DK) for h in range(HG_HEADS)]
  v_bf = padr(cols(OFF_IV, HG_WIDTH)).astype(BF16)
  k_end_bf = padr(k_end).astype(BF16)
  att_bf = []
  for sl in head_slices:
    att = jnp.zeros((c, cs), F32)
    for ql, kl, mk in levels:
      att = jnp.where(mk, _dot_nt(ql[:, sl], kl[:, sl]), att)
    att_bf.append(att.astype(BF16))
    yield 1
  s0 = [hg_ref[h] for h in range(HG_HEADS)]
  o_heads = [_dot(att_bf[h], v_bf[:, sl]) + _dot(q_inter[:, sl], s0[h].astype(BF16))
             for h, sl in enumerate(head_slices)]
  for h, sl in enumerate(head_slices):
    hg_ref[h] = s0[h] * _col_matrix(state_decay[:, sl], HG_DV) + _dot_tn(k_end_bf[:, sl], v_bf[:, sl])
  o_parts = [o_h * lax.rsqrt(jnp.mean(o_h * o_h, axis=-1, keepdims=True) + EPS) for o_h in o_heads]
  o = jnp.concatenate(o_parts, axis=1) * p.hg_norm * _silu(cols(OFF_OG, HG_WIDTH))
  return jnp.concatenate([y_ssd, o], axis=1)


def _load_mix_params(refs):
  return MixParams(*[r[...] for r in refs])


def _drive(gen, between=lambda n: None):
  while True:
    try:
      between(next(gen))
    except StopIteration as done:
      return done.value


def _drive_round_robin(gens):
  results = [None] * len(gens)
  active = list(range(len(gens)))
  while active:
    for idx in list(active):
      try:
        next(gens[idx])
      except StopIteration as done:
        results[idx] = done.value
        active.remove(idx)
  return results


def _stage_ffn_weights(step, wup_f32_ref, wdn_f32_ref, wup_ref, wdn_ref):
  wup_ref[step] = wup_f32_ref[...].astype(BF16)
  wdn_ref[step] = wdn_f32_ref[...].astype(BF16)


def _staged_up_cols(wup_ref, lo):
  return wup_ref[lo // UP_BLOCK, :, lo % UP_BLOCK:lo % UP_BLOCK + FF_CHUNK]


def _swiglu(h, wup_ref, wdn_ref):
  acc = jnp.zeros((h.shape[0], D_MODEL), F32)
  for ci in range(N_FF_CHUNKS):
    lo = ci * FF_CHUNK
    g = _dot(h, _staged_up_cols(wup_ref, lo))
    u = _dot(h, _staged_up_cols(wup_ref, D_FF + lo))
    acc = acc + _dot((_silu(g) * u).astype(BF16), wdn_ref[ci])
  return acc


def _first_tile_select(first_ref, rest_ref, first_step):
  return jnp.where(pl.program_id(0) == first_step, first_ref[...], rest_ref[...])


def _ffn1_kernel(xs_ref, xp_ref, nw_ref, wup_f32_ref, wdn_f32_ref, os_ref, op_ref, wup_ref, wdn_ref):
  i = pl.program_id(0)

  @pl.when(i < W_STEPS)
  def _():
    _stage_ffn_weights(i, wup_f32_ref, wdn_f32_ref, wup_ref, wdn_ref)

  @pl.when(i >= W_STEPS)
  def _():
    x = _first_tile_select(xs_ref, xp_ref, W_STEPS)
    h = _rmsnorm(x, nw_ref[...]).astype(BF16)
    op_ref[...] = x + 0.5 * _swiglu(h, wup_ref, wdn_ref)

  @pl.when(i == W_STEPS)
  def _():
    os_ref[...] = op_ref[...]


def _split_w_in_kernel(wt_ref, wall_ref):
  o_dt = SSD_DINNER + CONV_DIM
  o_q = o_dt + SSD_HEADS
  wall_ref[:, OFF_Z:OFF_Q] = jnp.transpose(wt_ref[0:o_dt, :]).astype(BF16)
  wall_ref[:, OFF_Q:OFF_DT] = jnp.transpose(wt_ref[o_q:, :]).astype(BF16)
  dt_rows = jnp.concatenate([wt_ref[o_dt:o_q, :], jnp.zeros((LANES - SSD_HEADS, LANES), F32)], axis=0)
  wall_ref[:, OFF_DT:PROJ_W] = jnp.transpose(dt_rows).astype(BF16)


def _proj_blocks():
  return [(lo, min(lo + PROJ_BLOCK, PROJ_W)) for lo in range(0, PROJ_W, PROJ_BLOCK)]


def _mix_prompt_kernel(x_ref, nmix_ref, wall_ref, *rest, chunks_per_seq):
  prm_refs = rest[:len(MixParams._fields)]
  mixed_ref, conv_ref, ssm_ref, hg_ref, xb_ref, proj_a, proj_b = rest[len(MixParams._fields):]
  i = pl.program_id(0)
  t_cur = lax.rem(jnp.maximum(i - 1, 0), chunks_per_seq)

  @pl.when(i == 0)
  def _():
    proj_b[...] = jnp.zeros(proj_b.shape, F32)

  @pl.when(t_cur == 0)
  def _():
    xb_ref[0:SUBLANES, :] = jnp.zeros((SUBLANES, CONV_DIM), F32)
    ssm_ref[...] = jnp.zeros(ssm_ref.shape, F32)
    hg_ref[...] = jnp.zeros(hg_ref.shape, F32)

  def step(cur_ref, next_ref):
    h = _rmsnorm(x_ref[0], nmix_ref[...]).astype(BF16)
    pending = _proj_blocks()
    n_blocks = len(pending)
    credit = [0.0]

    def project_blocks(n):
      credit[0] += n * n_blocks / MIX_PHASE_WEIGHT
      while pending and credit[0] >= 1.0:
        credit[0] -= 1.0
        lo, hi = pending.pop(0)
        next_ref[:, lo:hi] = _dot(h, wall_ref[:, lo:hi])

    mixed = _drive(_mix_chunk(cur_ref, c=CHUNK, valid=CHUNK, p=_load_mix_params(prm_refs), xb_ref=xb_ref,
                              ssm_ref=ssm_ref.at[0], hg_ref=hg_ref.at[0]), project_blocks)
    project_blocks(MIX_PHASE_WEIGHT)
    mixed_ref[0] = mixed.astype(BF16)
    tail = xb_ref[CHUNK:CHUNK + SUBLANES, :]
    xb_ref[0:SUBLANES, :] = tail

  parity = lax.rem(i, 2)
  pl.when(parity == 0)(functools.partial(step, proj_b, proj_a))
  pl.when(parity == 1)(functools.partial(step, proj_a, proj_b))

  @pl.when((t_cur == chunks_per_seq - 1) & (i > 0))
  def _():
    conv_ref[0] = xb_ref[SUBLANES + CHUNK - (CONV_W - 1):SUBLANES + CHUNK, :]


def _proj_sample_kernel(x_ref, nmix_ref, wall_ref, o_ref):
  h = _rmsnorm(x_ref[...], nmix_ref[...]).astype(BF16)
  for lo, hi in _proj_blocks():
    o_ref[:, lo:hi] = _dot(h, wall_ref[:, lo:hi])


def _rec_sample_kernel(proj_ref, conv_in_ref, ssm_in_ref, hg_in_ref, *rest, valid):
  prm_refs = rest[:len(MixParams._fields)]
  mixed_ref, conv_ref, ssm_ref, hg_ref, xb_ref, padded_ref = rest[len(MixParams._fields):]
  n_seq = conv_in_ref.shape[0]
  prm = _load_mix_params(prm_refs)
  ssm_ref[...] = ssm_in_ref[...]
  hg_ref[...] = hg_in_ref[...]
  gens = []
  for s in range(n_seq):
    xb_ref[s, 0:SUBLANES, :] = jnp.zeros((SUBLANES, CONV_DIM), F32)
    xb_ref[s, SUBLANES - (CONV_W - 1):SUBLANES, :] = conv_in_ref[s]
    padded_ref[s, 0:valid, :] = proj_ref[s * valid:(s + 1) * valid, :]
    padded_ref[s, valid:SAMPLE_ROWS, :] = jnp.zeros((SAMPLE_ROWS - valid, PROJ_W), F32)
    gens.append(_mix_chunk(padded_ref.at[s], c=SAMPLE_ROWS, valid=valid, p=prm, xb_ref=xb_ref.at[s],
                           ssm_ref=ssm_ref.at[s], hg_ref=hg_ref.at[s]))
  for s, mixed in enumerate(_drive_round_robin(gens)):
    mixed_ref[s] = mixed.astype(BF16)
    conv_ref[s] = xb_ref[s, SUBLANES + valid - (CONV_W - 1):SUBLANES + valid, :]


def _tail_kernel(x1s_ref, x1p_ref, mixs_ref, mixp_ref, ps_ref, pp_ref, wout_ref, nf2_ref, wup_f32_ref, wdn_f32_ref,
                 nple_ref, wgate_ref, wproj_ref, ppost_ref, nfin_ref, os_ref, op_ref, wup_ref, wdn_ref):
  i = pl.program_id(0)

  @pl.when(i < W_STEPS)
  def _():
    _stage_ffn_weights(i, wup_f32_ref, wdn_f32_ref, wup_ref, wdn_ref)

  @pl.when(i >= W_STEPS)
  def _():
    pick = lambda s_ref, p_ref: _first_tile_select(s_ref, p_ref, W_STEPS)
    x2 = pick(x1s_ref, x1p_ref) + _dot(pick(mixs_ref, mixp_ref), wout_ref[...])
    h = _rmsnorm(x2, nf2_ref[...]).astype(BF16)
    e = _rmsnorm(_dot(pick(ps_ref, pp_ref).astype(BF16), wproj_ref[...]), ppost_ref[...])
    x3 = x2 + 0.5 * _swiglu(h, wup_ref, wdn_ref)
    gate = _sigmoid(_dot(_rmsnorm(x3, nple_ref[...]).astype(BF16), wgate_ref[...]))
    x4 = x3 + gate * e
    op_ref[...] = _rmsnorm(x4, nfin_ref[...])

  @pl.when(i == W_STEPS)
  def _():
    os_ref[...] = op_ref[...]


def _resident(shape):
  nd = len(shape)
  return pl.BlockSpec(shape, lambda *_: (0,) * nd, pipeline_mode=pl.Buffered(1))


def _params(semantics, flags=None):
  return pltpu.CompilerParams(dimension_semantics=semantics, vmem_limit_bytes=VMEM_LIMIT_BYTES, flags=flags)


def _sample_then_prompt_specs(width, n_prompt_tiles, first_step):
  sample = pl.BlockSpec((FFN_TILE, width), lambda i: (0, 0))
  prompt = pl.BlockSpec((FFN_TILE, width), lambda i: (jnp.clip(i - first_step - 1, 0, n_prompt_tiles - 1), 0))
  return sample, prompt


def _ffn_weight_specs():
  staged = lambda i: jnp.minimum(i, W_STEPS - 1)
  return [pl.BlockSpec((D_MODEL, UP_BLOCK), lambda i: (0, staged(i))),
          pl.BlockSpec((FF_CHUNK, D_MODEL), lambda i: (staged(i), 0))]


FFN_WEIGHT_SCRATCH = [pltpu.VMEM((W_STEPS, D_MODEL, UP_BLOCK), BF16), pltpu.VMEM((W_STEPS, FF_CHUNK, D_MODEL), BF16)]


def _ffn1(xs, xp, nw, wup, wdn):
  assert xs.shape[0] == FFN_TILE and xp.shape[0] % FFN_TILE == 0
  n_prompt_tiles = xp.shape[0] // FFN_TILE
  specs = _sample_then_prompt_specs(D_MODEL, n_prompt_tiles, first_step=W_STEPS)
  return pl.pallas_call(
      _ffn1_kernel,
      grid=(W_STEPS + 1 + n_prompt_tiles,),
      in_specs=[*specs, _resident(nw.shape), *_ffn_weight_specs()],
      out_specs=specs,
      out_shape=(jax.ShapeDtypeStruct(xs.shape, F32), jax.ShapeDtypeStruct(xp.shape, F32)),
      scratch_shapes=FFN_WEIGHT_SCRATCH,
      compiler_params=_params(("arbitrary",)),
      name="ffn1",
  )(xs, xp, nw, wup, wdn)


def _split_w_in(wt):
  rows = LANES
  assert wt.shape == (SSD_DINNER + CONV_DIM + SSD_HEADS + 4 * HG_WIDTH, D_MODEL)
  return pl.pallas_call(
      _split_w_in_kernel,
      grid=(D_MODEL // rows,),
      in_specs=[pl.BlockSpec((wt.shape[0], rows), lambda i: (0, i))],
      out_specs=pl.BlockSpec((rows, PROJ_W), lambda i: (i, 0)),
      out_shape=jax.ShapeDtypeStruct((D_MODEL, PROJ_W), BF16),
      compiler_params=_params(("arbitrary",)),
      name="split_w_in",
  )(wt)


def _mix_prompt(x1, nmix, wall, prm):
  bsz, seq, _ = x1.shape
  nt = seq // CHUNK
  n_chunks = bsz * nt
  out_shape = (
      jax.ShapeDtypeStruct((bsz, seq, D_MIX), BF16),
      jax.ShapeDtypeStruct((bsz, CONV_W - 1, CONV_DIM), F32),
      jax.ShapeDtypeStruct((bsz, SSD_HEADS, SSD_HEADDIM, SSD_STATE), F32),
      jax.ShapeDtypeStruct((bsz, HG_HEADS, HG_DK, HG_DV), F32),
  )
  proj_chunk = lambda i: jnp.minimum(i, n_chunks - 1)
  mix_chunk = lambda i: jnp.maximum(i - 1, 0)
  out_specs = (
      pl.BlockSpec((1, CHUNK, D_MIX), lambda i: (mix_chunk(i) // nt, mix_chunk(i) % nt, 0)),
      pl.BlockSpec((1, CONV_W - 1, CONV_DIM), lambda i: (mix_chunk(i) // nt, 0, 0)),
      pl.BlockSpec((1, SSD_HEADS, SSD_HEADDIM, SSD_STATE), lambda i: (mix_chunk(i) // nt, 0, 0, 0)),
      pl.BlockSpec((1, HG_HEADS, HG_DK, HG_DV), lambda i: (mix_chunk(i) // nt, 0, 0, 0)),
  )
  in_specs = [pl.BlockSpec((1, CHUNK, D_MODEL), lambda i: (proj_chunk(i) // nt, proj_chunk(i) % nt, 0)),
              _resident(nmix.shape)] + [_resident(a.shape) for a in (wall, *prm)]
  return pl.pallas_call(
      functools.partial(_mix_prompt_kernel, chunks_per_seq=nt),
      grid=(n_chunks + 1,),
      in_specs=in_specs,
      out_specs=out_specs,
      out_shape=out_shape,
      scratch_shapes=[pltpu.VMEM((SUBLANES + CHUNK, CONV_DIM), F32), pltpu.VMEM((CHUNK, PROJ_W), F32),
                      pltpu.VMEM((CHUNK, PROJ_W), F32)],
      compiler_params=_params(("arbitrary",)),
      name="mix_prompt",
  )(x1, nmix, wall, *prm)


def _proj_sample(x1p, nmix, wall):
  n = x1p.shape[0]
  tile = 256
  return pl.pallas_call(
      _proj_sample_kernel,
      grid=(n // tile,),
      in_specs=[pl.BlockSpec((tile, D_MODEL), lambda i: (i, 0)), _resident(nmix.shape), _resident(wall.shape)],
      out_specs=pl.BlockSpec((tile, PROJ_W), lambda i: (i, 0)),
      out_shape=jax.ShapeDtypeStruct((n, PROJ_W), F32),
      compiler_params=_params(("arbitrary",)),
      name="proj_sample",
  )(x1p, nmix, wall)


def _rec_sample(proj, conv0, ssm0, hg0, prm, valid):
  bsz = conv0.shape[0]
  assert proj.shape[0] == bsz * valid
  out_shape = (
      jax.ShapeDtypeStruct((bsz, SAMPLE_ROWS, D_MIX), BF16),
      jax.ShapeDtypeStruct((bsz, CONV_W - 1, CONV_DIM), F32),
      jax.ShapeDtypeStruct((bsz, SSD_HEADS, SSD_HEADDIM, SSD_STATE), F32),
      jax.ShapeDtypeStruct((bsz, HG_HEADS, HG_DK, HG_DV), F32),
  )
  g = SAMPLE_SEQS_PER_STEP
  state_specs = [
      pl.BlockSpec((g, CONV_W - 1, CONV_DIM), lambda b: (b, 0, 0)),
      pl.BlockSpec((g, SSD_HEADS, SSD_HEADDIM, SSD_STATE), lambda b: (b, 0, 0, 0)),
      pl.BlockSpec((g, HG_HEADS, HG_DK, HG_DV), lambda b: (b, 0, 0, 0)),
  ]
  in_specs = [pl.BlockSpec((g * valid, PROJ_W), lambda b: (b, 0))] + state_specs + [_resident(a.shape) for a in prm]
  out_specs = tuple([pl.BlockSpec((g, SAMPLE_ROWS, D_MIX), lambda b: (b, 0, 0))] + state_specs)
  return pl.pallas_call(
      functools.partial(_rec_sample_kernel, valid=valid),
      grid=(bsz // g,),
      in_specs=in_specs,
      out_specs=out_specs,
      out_shape=out_shape,
      scratch_shapes=[pltpu.VMEM((g, 2 * SUBLANES, CONV_DIM), F32), pltpu.VMEM((g, SAMPLE_ROWS, PROJ_W), F32)],
      compiler_params=_params(("arbitrary",)),
      name="rec_sample",
  )(proj, conv0, ssm0, hg0, *prm)


def _tail(x1s, x1p, mixs, mixp, ps, pp, wout, nf2, wup, wdn, nple, wgate, wproj, ppost, nfin):
  assert x1s.shape[0] == FFN_TILE and x1p.shape[0] % FFN_TILE == 0
  weights = (wout, nf2, wup, wdn, nple, wgate, wproj, ppost, nfin)
  n_prompt_tiles = x1p.shape[0] // FFN_TILE

  def in_specs(width):
    sample, prompt = _sample_then_prompt_specs(width, n_prompt_tiles, first_step=W_STEPS)
    return pl.BlockSpec(sample.block_shape, sample.index_map, pipeline_mode=pl.Buffered(1)), prompt

  weight_specs = [_resident(wout.shape), _resident(nf2.shape), *_ffn_weight_specs()] + [
      _resident(w.shape) for w in (nple, wgate, wproj, ppost, nfin)]
  return pl.pallas_call(
      _tail_kernel,
      grid=(W_STEPS + 1 + n_prompt_tiles,),
      in_specs=[*in_specs(D_MODEL), *in_specs(D_MIX), *in_specs(PLE_DIM)] + weight_specs,
      out_specs=_sample_then_prompt_specs(D_MODEL, n_prompt_tiles, first_step=W_STEPS),
      out_shape=(jax.ShapeDtypeStruct(x1s.shape, F32), jax.ShapeDtypeStruct(x1p.shape, F32)),
      scratch_shapes=FFN_WEIGHT_SCRATCH,
      compiler_params=_params(("arbitrary",)),
      name="tail",
  )(x1s, x1p, mixs, mixp, ps, pp, *weights)


def kernel(x_prompt, x_sample, state_conv, state_ssm, state_hgrn, p_prompt, p_sample, norm_ffn1, w_ffn1_up, w_ffn1_down, norm_mix, w_in, conv_w, conv_b, dt_bias, a_log, d_skip, ssd_norm, hg_lb_logits, hg_norm, w_out, norm_ffn2, w_ffn2_up, w_ffn2_down, norm_ple, w_ple_gate, w_ple_proj, ple_post_norm, norm_final):
  bp, seq, _ = x_prompt.shape
  bs, dec_seq, _ = x_sample.shape
  row = lambda v: v.reshape(1, -1).astype(F32)
  per_head = lambda v: jnp.repeat(v.astype(F32), SSD_HEADDIM).reshape(1, SSD_DINNER)
  head_lanes = lambda v: jnp.pad(v.astype(F32), (0, LANES - SSD_HEADS)).reshape(1, LANES)

  w1u, w1d = w_ffn1_up[0].astype(F32), w_ffn1_down[0].astype(F32)
  w2u, w2d = w_ffn2_up[0].astype(F32), w_ffn2_down[0].astype(F32)
  wall = _split_w_in(jnp.swapaxes(w_in[0], 0, 1))
  prm = MixParams(
      conv_w=conv_w[0].astype(F32), conv_b=row(conv_b[0]), dtb_c=head_lanes(dt_bias[0]),
      alog_c=head_lanes(a_log[0]), dskip_e=per_head(d_skip[0]), ssd_norm=row(ssd_norm[0]),
      lb_logits=hg_lb_logits.astype(F32), hg_norm=row(hg_norm[0]))
  nmix = row(norm_mix[0])
  tail_w = (w_out[0].astype(BF16), row(norm_ffn2[0]), w2u, w2d, row(norm_ple[0]), w_ple_gate[0].astype(BF16),
            w_ple_proj[0].astype(BF16), row(ple_post_norm[0]), row(norm_final))

  xp = x_prompt.reshape(bp * seq, D_MODEL)
  xs = x_sample.reshape(bs * dec_seq, D_MODEL)
  x1s, x1p = _ffn1(xs, xp, row(norm_ffn1[0]), w1u, w1d)

  mixed_p, conv_p, ssm_p, hg_p = _mix_prompt(x1p.reshape(bp, seq, D_MODEL), nmix, wall, prm)
  proj_s = _proj_sample(x1s, nmix, wall)
  mixed_s, conv_s, ssm_s, hg_s = _rec_sample(proj_s, state_conv[0], state_ssm[0], state_hgrn[0], prm, dec_seq)
  mixed_s = mixed_s[:, :dec_seq].reshape(bs * dec_seq, D_MIX)

  y_sample, y_prompt = _tail(x1s, x1p, mixed_s, mixed_p.reshape(bp * seq, D_MIX),
                             p_sample[0].reshape(bs * dec_seq, PLE_DIM), p_prompt[0].reshape(bp * seq, PLE_DIM), *tail_w)

  return (y_prompt.reshape(bp, seq, D_MODEL), y_sample.reshape(bs, dec_seq, D_MODEL), conv_p[None], ssm_p[None],
          hg_p[None], conv_s[None], ssm_s[None], hg_s[None])
```

```python
import functools
from typing import NamedTuple

import jax
import jax.numpy as jnp
from jax import lax
from jax.experimental import pallas as pl
from jax.experimental.pallas import tpu as pltpu

F32 = jnp.float32
BF16 = jnp.bfloat16

D_MODEL = 1024
D_FF = 2816
PLE_DIM = 256
EPS = 1e-6
SSD_HEADS = 16
SSD_HEADDIM = 64
SSD_DINNER = SSD_HEADS * SSD_HEADDIM
SSD_STATE = 128
SSD_GROUPS = 2
SSD_HEADS_PER_GROUP = SSD_HEADS // SSD_GROUPS
SSD_GROUP_WIDTH = SSD_DINNER // SSD_GROUPS
CONV_W = 4
CONV_DIM = SSD_DINNER + 2 * SSD_GROUPS * SSD_STATE
HG_HEADS = 8
HG_DK = 128
HG_DV = 128
HG_WIDTH = HG_HEADS * HG_DV
HG_BLOCK = 16
D_MIX = SSD_DINNER + HG_WIDTH

LANES = 128
SUBLANES = 8
VMEM_LIMIT_BYTES = 56 * 1024 * 1024

CHUNK = 128
PROMPT_SEQS_PER_STEP = 2
SAMPLE_ROWS = 8
SAMPLE_SEQS_PER_STEP = 8
FFN_TILE = 512
FF_CHUNK = 256
N_FF_CHUNKS = D_FF // FF_CHUNK
W_STEPS = N_FF_CHUNKS
UP_BLOCK = 2 * D_FF // W_STEPS

OFF_Z = 0
OFF_XBC = OFF_Z + SSD_DINNER
OFF_Q = OFF_XBC + CONV_DIM
OFF_FR = OFF_Q + HG_WIDTH
OFF_IV = OFF_FR + HG_WIDTH
OFF_OG = OFF_IV + HG_WIDTH
OFF_DT = OFF_OG + HG_WIDTH
PROJ_W = OFF_DT + LANES
PROJ_BLOCK = 256
MIX_PHASE_WEIGHT = 34


class MixParams(NamedTuple):
  conv_w: jax.Array
  conv_b: jax.Array
  dtb_c: jax.Array
  alog_c: jax.Array
  dskip_e: jax.Array
  ssd_norm: jax.Array
  lb_logits: jax.Array
  hg_norm: jax.Array


def _dot(a, b):
  return jnp.dot(a, b, preferred_element_type=F32)


def _dot_nt(a, b):
  return lax.dot_general(a, b, (((1,), (1,)), ((), ())), preferred_element_type=F32)


def _dot_tn(a, b):
  return lax.dot_general(a, b, (((0,), (0,)), ((), ())), preferred_element_type=F32)


NEG_LOG2_E = -1.4426950408889634


def _sigmoid(x):
  return 1.0 / (1.0 + jnp.exp2(x * NEG_LOG2_E))


def _silu(x):
  return x * _sigmoid(x)


def _softplus(x):
  return jnp.maximum(x, 0.0) + jnp.log1p(jnp.exp(-jnp.abs(x)))


def _rmsnorm(x, w):
  ms = jnp.mean(x * x, axis=-1, keepdims=True)
  return x * lax.rsqrt(ms + EPS) * w


def _cumsum_rows(x):
  c, n = x.shape
  g = c // SUBLANES
  x3 = x.reshape(g, SUBLANES, n)
  sub = lax.broadcasted_iota(jnp.int32, (1, SUBLANES, n), 1)
  s = 1
  while s < SUBLANES:
    x3 = x3 + jnp.where(sub >= s, pltpu.roll(x3, s, axis=1), 0.0)
    s *= 2
  if g > 1:
    tot = jnp.broadcast_to(x3[:, SUBLANES - 1:SUBLANES, :], (g, SUBLANES, n))
    offs = [jnp.zeros((1, SUBLANES, n), x.dtype)]
    for k in range(1, g):
      offs.append(offs[-1] + tot[k - 1:k])
    x3 = x3 + jnp.concatenate(offs, axis=0)
  return x3.reshape(c, n)


def _expand_heads(xc):
  c = xc.shape[0]
  first_head = lax.broadcasted_iota(jnp.int32, (1, LANES), 1) < SSD_HEADDIM
  parts = []
  for j in range(SSD_HEADS // 2):
    a = jnp.broadcast_to(xc[:, 2 * j:2 * j + 1], (c, LANES))
    b = jnp.broadcast_to(xc[:, 2 * j + 1:2 * j + 2], (c, LANES))
    parts.append(jnp.where(first_head, a, b))
  return jnp.concatenate(parts, axis=1)


def _col_matrix(row_vec, n_lanes):
  n = row_vec.shape[1]
  return jnp.transpose(jnp.broadcast_to(row_vec, (n_lanes, n)))


def _mix_chunk(proj_ref, *, c, valid, p, xb_ref, ssm_ref, hg_ref):
  cs = max(c, LANES)

  def padr(x):
    if x.shape[0] == cs:
      return x
    return jnp.concatenate([x, jnp.zeros((cs - x.shape[0], x.shape[1]), x.dtype)], axis=0)

  row_valid = None
  if valid < c:
    row_valid = lax.broadcasted_iota(jnp.int32, (c, 1), 0) < valid

  def mask_rows(x):
    return x if row_valid is None else jnp.where(row_valid, x, 0.0)

  cols = lambda off, width: proj_ref[:, off:off + width]

  xbc = cols(OFF_XBC, CONV_DIM)
  xb_ref[SUBLANES:SUBLANES + c, :] = xbc
  conv = p.conv_b + p.conv_w[CONV_W - 1:CONV_W] * xbc
  for j in range(CONV_W - 1):
    conv = conv + p.conv_w[j:j + 1] * xb_ref[SUBLANES - (CONV_W - 1) + j:SUBLANES - (CONV_W - 1) + j + c, :]
  yield 2
  conv = _silu(conv)
  yield 2
  xs = conv[:, :SSD_DINNER]
  bm = conv[:, SSD_DINNER:SSD_DINNER + SSD_GROUPS * SSD_STATE]
  cm = conv[:, SSD_DINNER + SSD_GROUPS * SSD_STATE:]

  head_lane = lax.broadcasted_iota(jnp.int32, (1, LANES), 1) < SSD_HEADS
  dt_c = mask_rows(jnp.where(head_lane, _softplus(cols(OFF_DT, LANES) + p.dtb_c), 0.0))
  acs_c = _cumsum_rows(dt_c * -jnp.exp(p.alog_c))
  last_c = acs_c[c - 1:c, :]
  dec_in = _expand_heads(jnp.exp(acs_c))
  xd_end = xs * _expand_heads(dt_c * jnp.exp(last_c - acs_c))
  yield 2
  acs_t = jnp.transpose(padr(acs_c))
  dt_t = jnp.transpose(padr(dt_c))
  chunk_decay_t = jnp.transpose(jnp.broadcast_to(jnp.exp(last_c), (LANES, LANES)))

  t_idx = lax.broadcasted_iota(jnp.int32, (c, cs), 0)
  s_idx = lax.broadcasted_iota(jnp.int32, (c, cs), 1)
  causal = s_idx <= t_idx
  causal_bias = jnp.where(causal, 0.0, -jnp.inf)
  lane = lax.broadcasted_iota(jnp.int32, (1, LANES), 1)
  first_head = lane < SSD_HEADDIM

  group_heads = [range(g * SSD_HEADS_PER_GROUP, (g + 1) * SSD_HEADS_PER_GROUP) for g in range(SSD_GROUPS)]
  bg_p = [padr(bm[:, g * SSD_STATE:(g + 1) * SSD_STATE]).astype(BF16) for g in range(SSD_GROUPS)]
  cg = [cm[:, g * SSD_STATE:(g + 1) * SSD_STATE].astype(BF16) for g in range(SSD_GROUPS)]
  h0 = [ssm_ref[heads.start:heads.stop].reshape(SSD_GROUP_WIDTH, SSD_STATE) for heads in group_heads]
  cb = [_dot_nt(cg[g], bg_p[g]) for g in range(SSD_GROUPS)]
  y_inter = [_dot_nt(cg[g], h0[g].astype(BF16)) for g in range(SSD_GROUPS)]
  xs_bf = padr(xs).astype(BF16)
  zero_bf = jnp.zeros((cs, LANES), BF16)
  y_intra = []
  for g in range(SSD_GROUPS):
    for j in range(SSD_HEADS_PER_GROUP // 2):
      h_first = g * SSD_HEADS_PER_GROUP + 2 * j
      scores = []
      for h in (h_first, h_first + 1):
        diff = acs_c[:, h:h + 1] - acs_t[h:h + 1, :]
        decay_dt = jnp.exp(diff + causal_bias) * dt_t[h:h + 1, :]
        scores.append((cb[g] * decay_dt).astype(BF16))
      pair = xs_bf[:, h_first * SSD_HEADDIM:(h_first + 2) * SSD_HEADDIM]
      rhs = jnp.concatenate([jnp.where(first_head, pair, zero_bf), jnp.where(first_head, zero_bf, pair)], axis=0)
      y_intra.append(_dot(jnp.concatenate(scores, axis=1), rhs))
      yield 1
  for g, heads in enumerate(group_heads):
    xd_g = padr(xd_end[:, g * SSD_GROUP_WIDTH:(g + 1) * SSD_GROUP_WIDTH]).astype(BF16)
    decay_cols = jnp.concatenate(
        [jnp.broadcast_to(chunk_decay_t[h:h + 1, :], (SSD_HEADDIM, SSD_STATE)) for h in heads], axis=0)
    h1 = h0[g] * decay_cols + _dot_tn(xd_g, bg_p[g])
    ssm_ref[heads.start:heads.stop] = h1.reshape(SSD_HEADS_PER_GROUP, SSD_HEADDIM, SSD_STATE)
  y = jnp.concatenate(y_intra, axis=1) + jnp.concatenate(y_inter, axis=1) * dec_in + p.dskip_e * xs
  yield 1
  yg = y * _silu(cols(OFF_Z, SSD_DINNER))
  yield 1
  y_norm = []
  for g in range(SSD_GROUPS):
    blk = yg[:, g * SSD_GROUP_WIDTH:(g + 1) * SSD_GROUP_WIDTH]
    y_norm.append(blk * lax.rsqrt(jnp.mean(blk * blk, axis=-1, keepdims=True) + EPS))
  y_ssd = jnp.concatenate(y_norm, axis=1) * p.ssd_norm

  lg = p.lb_logits
  lg_max = jnp.max(lg, axis=0, keepdims=True)
  lg_exp = jnp.exp(lg - lg_max)
  lb = lg_exp[0:1] / jnp.sum(lg_exp, axis=0, keepdims=True)
  f = lb + (1.0 - lb) * _sigmoid(cols(OFF_FR, HG_WIDTH))
  yield 1
  logf = mask_rows(jnp.log(f))
  kk = mask_rows(1.0 - f)
  qq = _silu(cols(OFF_Q, HG_WIDTH))
  yield 1
  b = _cumsum_rows(logf)
  yield 2
  b_last = b[c - 1:c, :]
  q_inter = (qq * jnp.exp(b)).astype(BF16)
  k_end = kk * jnp.exp(b_last - b)
  state_decay = jnp.exp(b_last)
  yield 1

  levels = []
  m = c // 2
  while m >= HG_BLOCK:
    q_rows, k_rows = [], []
    zero_half = jnp.zeros((m, HG_WIDTH), BF16)
    for i in range(c // (2 * m)):
      lo = i * 2 * m
      b_ref_row = b[lo + m - 1:lo + m, :]
      k_rows += [(kk[lo:lo + m] * jnp.exp(b_ref_row - b[lo:lo + m])).astype(BF16), zero_half]
      q_rows += [zero_half, (qq[lo + m:lo + 2 * m] * jnp.exp(b[lo + m:lo + 2 * m] - b_ref_row)).astype(BF16)]
    shift = m.bit_length() - 1
    mask = ((t_idx >> (shift + 1)) == (s_idx >> (shift + 1))) & (((t_idx >> shift) & 1) == 1) & (
        ((s_idx >> shift) & 1) == 0)
    levels.append((jnp.concatenate(q_rows, axis=0), jnp.concatenate(k_rows, axis=0), mask))
    yield 1
    m //= 2
  blk = min(HG_BLOCK, c)
  q_rows, k_rows = [], []
  for i in range(c // blk):
    lo = i * blk
    b_loc = b[lo:lo + blk] if i == 0 else b[lo:lo + blk] - b[lo - 1:lo, :]
    q_rows.append(qq[lo:lo + blk] * jnp.exp(b_loc))
    k_rows.append(kk[lo:lo + blk] * jnp.exp(-b_loc))
  shift = blk.bit_length() - 1
  levels.append((jnp.concatenate(q_rows, axis=0).astype(BF16), padr(jnp.concatenate(k_rows, axis=0)).astype(BF16),
                 ((t_idx >> shift) == (s_idx >> shift)) & causal))

  yield 2
  head_slices = [slice(h * HG_DK, (h + 1) * HG_DK) for h in range(HG_HEADS)]
  v_bf = padr(cols(OFF_IV, HG_WIDTH)).astype(BF16)
  k_end_bf = padr(k_end).astype(BF16)
  att_bf = []
  for sl in head_slices:
    att = None
    for li, (ql, kl, mk) in enumerate(levels):
      block_scores = _dot_nt(ql[:, sl], kl[:, sl])
      if li == 0 and len(levels) > 1:
        att = block_scores
      else:
        att = jnp.where(mk, block_scores, 0.0 if att is None else att)
    att_bf.append(att.astype(BF16))
    yield 1
  s0 = [hg_ref[h] for h in range(HG_HEADS)]
  o_heads = [_dot(att_bf[h], v_bf[:, sl]) + _dot(q_inter[:, sl], s0[h].astype(BF16))
             for h, sl in enumerate(head_slices)]
  for h, sl in enumerate(head_slices):
    hg_ref[h] = s0[h] * _col_matrix(state_decay[:, sl], HG_DV) + _dot_tn(k_end_bf[:, sl], v_bf[:, sl])
  o_parts = [o_h * lax.rsqrt(jnp.mean(o_h * o_h, axis=-1, keepdims=True) + EPS) for o_h in o_heads]
  o = jnp.concatenate(o_parts, axis=1) * p.hg_norm * _silu(cols(OFF_OG, HG_WIDTH))
  return jnp.concatenate([y_ssd, o], axis=1)


def _load_mix_params(refs):
  return MixParams(*[r[...] for r in refs])


def _drive(gen, between=lambda n: None):
  while True:
    try:
      between(next(gen))
    except StopIteration as done:
      return done.value


def _drive_round_robin(gens, between=lambda n: None):
  results = [None] * len(gens)
  active = list(range(len(gens)))
  while active:
    for idx in list(active):
      try:
        between(next(gens[idx]))
      except StopIteration as done:
        results[idx] = done.value
        active.remove(idx)
  return results


def _stage_ffn_weights(step, wup_f32_ref, wdn_f32_ref, wup_ref, wdn_ref):
  wup_ref[step] = wup_f32_ref[...].astype(BF16)
  wdn_ref[step] = wdn_f32_ref[...].astype(BF16)


def _staged_up_cols(wup_ref, lo):
  return wup_ref[lo // UP_BLOCK, :, lo % UP_BLOCK:lo % UP_BLOCK + FF_CHUNK]


def _swiglu(h, wup_ref, wdn_ref):
  acc = jnp.zeros((h.shape[0], D_MODEL), F32)
  for ci in range(N_FF_CHUNKS):
    lo = ci * FF_CHUNK
    g = _dot(h, _staged_up_cols(wup_ref, lo))
    u = _dot(h, _staged_up_cols(wup_ref, D_FF + lo))
    acc = acc + _dot((_silu(g) * u).astype(BF16), wdn_ref[ci])
  return acc


def _first_tile_select(first_ref, rest_ref, first_step):
  return jnp.where(pl.program_id(0) == first_step, first_ref[...], rest_ref[...])


def _ffn1_kernel(xs_ref, xp_ref, nw_ref, wup_f32_ref, wdn_f32_ref, os_ref, op_ref, wup_ref, wdn_ref):
  i = pl.program_id(0)

  @pl.when(i < W_STEPS)
  def _():
    _stage_ffn_weights(i, wup_f32_ref, wdn_f32_ref, wup_ref, wdn_ref)

  @pl.when(i >= W_STEPS)
  def _():
    x = _first_tile_select(xs_ref, xp_ref, W_STEPS)
    h = _rmsnorm(x, nw_ref[...]).astype(BF16)
    op_ref[...] = x + 0.5 * _swiglu(h, wup_ref, wdn_ref)

  @pl.when(i == W_STEPS)
  def _():
    os_ref[...] = op_ref[...]


def _split_w_in_kernel(wt_ref, wall_ref):
  o_dt = SSD_DINNER + CONV_DIM
  o_q = o_dt + SSD_HEADS
  wall_ref[:, OFF_Z:OFF_Q] = jnp.transpose(wt_ref[0:o_dt, :]).astype(BF16)
  wall_ref[:, OFF_Q:OFF_DT] = jnp.transpose(wt_ref[o_q:, :]).astype(BF16)
  dt_rows = jnp.concatenate([wt_ref[o_dt:o_q, :], jnp.zeros((LANES - SSD_HEADS, LANES), F32)], axis=0)
  wall_ref[:, OFF_DT:PROJ_W] = jnp.transpose(dt_rows).astype(BF16)


def _proj_blocks():
  return [(lo, min(lo + PROJ_BLOCK, PROJ_W)) for lo in range(0, PROJ_W, PROJ_BLOCK)]


def _mix_prompt_kernel(x_ref, nmix_ref, wall_ref, *rest, chunks_per_seq):
  prm_refs = rest[:len(MixParams._fields)]
  mixed_ref, conv_ref, ssm_ref, hg_ref, xb_ref, proj_a, proj_b = rest[len(MixParams._fields):]
  n_seq = x_ref.shape[0]
  i = pl.program_id(0)
  t_cur = lax.rem(jnp.maximum(i - 1, 0), chunks_per_seq)

  @pl.when(i == 0)
  def _():
    proj_b[...] = jnp.zeros(proj_b.shape, F32)

  @pl.when(t_cur == 0)
  def _():
    xb_ref[:, 0:SUBLANES, :] = jnp.zeros((n_seq, SUBLANES, CONV_DIM), F32)
    ssm_ref[...] = jnp.zeros(ssm_ref.shape, F32)
    hg_ref[...] = jnp.zeros(hg_ref.shape, F32)

  def step(cur_buf, next_buf):
    h = [_rmsnorm(x_ref[s], nmix_ref[...]).astype(BF16) for s in range(n_seq)]
    pending = [(s, lo, hi) for lo, hi in _proj_blocks() for s in range(n_seq)]
    n_blocks = len(pending)
    credit = [0.0]

    def project_blocks(n):
      credit[0] += n * n_blocks / (n_seq * MIX_PHASE_WEIGHT)
      while pending and credit[0] >= 1.0:
        credit[0] -= 1.0
        s, lo, hi = pending.pop(0)
        next_buf[s * CHUNK:(s + 1) * CHUNK, lo:hi] = _dot(h[s], wall_ref[:, lo:hi])

    prm = _load_mix_params(prm_refs)
    gens = [_mix_chunk(cur_buf.at[pl.ds(s * CHUNK, CHUNK)], c=CHUNK, valid=CHUNK, p=prm, xb_ref=xb_ref.at[s],
                       ssm_ref=ssm_ref.at[s], hg_ref=hg_ref.at[s]) for s in range(n_seq)]
    results = _drive_round_robin(gens, project_blocks)
    project_blocks(n_seq * MIX_PHASE_WEIGHT)
    for s, mixed in enumerate(results):
      mixed_ref[s] = mixed.astype(BF16)
      tail = xb_ref[s, CHUNK:CHUNK + SUBLANES, :]
      xb_ref[s, 0:SUBLANES, :] = tail

  parity = lax.rem(i, 2)
  pl.when(parity == 0)(functools.partial(step, proj_b, proj_a))
  pl.when(parity == 1)(functools.partial(step, proj_a, proj_b))

  @pl.when((t_cur == chunks_per_seq - 1) & (i > 0))
  def _():
    conv_ref[...] = xb_ref[:, SUBLANES + CHUNK - (CONV_W - 1):SUBLANES + CHUNK, :]


def _proj_sample_kernel(x_ref, nmix_ref, wall_ref, o_ref):
  h = _rmsnorm(x_ref[...], nmix_ref[...]).astype(BF16)
  for lo, hi in _proj_blocks():
    o_ref[:, lo:hi] = _dot(h, wall_ref[:, lo:hi])


def _rec_sample_kernel(proj_ref, conv_in_ref, ssm_in_ref, hg_in_ref, *rest, valid):
  prm_refs = rest[:len(MixParams._fields)]
  mixed_ref, conv_ref, ssm_ref, hg_ref, xb_ref, padded_ref = rest[len(MixParams._fields):]
  n_seq = conv_in_ref.shape[0]
  prm = _load_mix_params(prm_refs)
  ssm_ref[...] = ssm_in_ref[...]
  hg_ref[...] = hg_in_ref[...]
  gens = []
  for s in range(n_seq):
    xb_ref[s, 0:SUBLANES, :] = jnp.zeros((SUBLANES, CONV_DIM), F32)
    xb_ref[s, SUBLANES - (CONV_W - 1):SUBLANES, :] = conv_in_ref[s]
    padded_ref[s, 0:valid, :] = proj_ref[s * valid:(s + 1) * valid, :]
    padded_ref[s, valid:SAMPLE_ROWS, :] = jnp.zeros((SAMPLE_ROWS - valid, PROJ_W), F32)
    gens.append(_mix_chunk(padded_ref.at[s], c=SAMPLE_ROWS, valid=valid, p=prm, xb_ref=xb_ref.at[s],
                           ssm_ref=ssm_ref.at[s], hg_ref=hg_ref.at[s]))
  for s, mixed in enumerate(_drive_round_robin(gens)):
    mixed_ref[s] = mixed.astype(BF16)
    conv_ref[s] = xb_ref[s, SUBLANES + valid - (CONV_W - 1):SUBLANES + valid, :]


def _tail_kernel(x1s_ref, x1p_ref, mixs_ref, mixp_ref, ps_ref, pp_ref, wout_ref, nf2_ref, wup_f32_ref, wdn_f32_ref,
                 nple_ref, wgate_ref, wproj_ref, ppost_ref, nfin_ref, os_ref, op_ref, wup_ref, wdn_ref):
  i = pl.program_id(0)

  @pl.when(i < W_STEPS)
  def _():
    _stage_ffn_weights(i, wup_f32_ref, wdn_f32_ref, wup_ref, wdn_ref)

  @pl.when(i >= W_STEPS)
  def _():
    pick = lambda s_ref, p_ref: _first_tile_select(s_ref, p_ref, W_STEPS)
    x2 = pick(x1s_ref, x1p_ref) + _dot(pick(mixs_ref, mixp_ref), wout_ref[...])
    h = _rmsnorm(x2, nf2_ref[...]).astype(BF16)
    e = _rmsnorm(_dot(pick(ps_ref, pp_ref).astype(BF16), wproj_ref[...]), ppost_ref[...])
    x3 = x2 + 0.5 * _swiglu(h, wup_ref, wdn_ref)
    gate = _sigmoid(_dot(_rmsnorm(x3, nple_ref[...]).astype(BF16), wgate_ref[...]))
    x4 = x3 + gate * e
    op_ref[...] = _rmsnorm(x4, nfin_ref[...])

  @pl.when(i == W_STEPS)
  def _():
    os_ref[...] = op_ref[...]


def _resident(shape):
  nd = len(shape)
  return pl.BlockSpec(shape, lambda *_: (0,) * nd, pipeline_mode=pl.Buffered(1))


def _params(semantics):
  return pltpu.CompilerParams(dimension_semantics=semantics, vmem_limit_bytes=VMEM_LIMIT_BYTES)


def _sample_then_prompt_specs(width, n_prompt_tiles, first_step):
  sample = pl.BlockSpec((FFN_TILE, width), lambda i: (0, 0))
  prompt = pl.BlockSpec((FFN_TILE, width), lambda i: (jnp.clip(i - first_step - 1, 0, n_prompt_tiles - 1), 0))
  return sample, prompt


def _ffn_weight_specs():
  staged = lambda i: jnp.minimum(i, W_STEPS - 1)
  return [pl.BlockSpec((D_MODEL, UP_BLOCK), lambda i: (0, staged(i))),
          pl.BlockSpec((FF_CHUNK, D_MODEL), lambda i: (staged(i), 0))]


FFN_WEIGHT_SCRATCH = [pltpu.VMEM((W_STEPS, D_MODEL, UP_BLOCK), BF16), pltpu.VMEM((W_STEPS, FF_CHUNK, D_MODEL), BF16)]


def _ffn1(xs, xp, nw, wup, wdn):
  assert xs.shape[0] == FFN_TILE and xp.shape[0] % FFN_TILE == 0
  n_prompt_tiles = xp.shape[0] // FFN_TILE
  specs = _sample_then_prompt_specs(D_MODEL, n_prompt_tiles, first_step=W_STEPS)
  return pl.pallas_call(
      _ffn1_kernel,
      grid=(W_STEPS + 1 + n_prompt_tiles,),
      in_specs=[*specs, _resident(nw.shape), *_ffn_weight_specs()],
      out_specs=specs,
      out_shape=(jax.ShapeDtypeStruct(xs.shape, F32), jax.ShapeDtypeStruct(xp.shape, F32)),
      scratch_shapes=FFN_WEIGHT_SCRATCH,
      compiler_params=_params(("arbitrary",)),
      name="ffn1",
  )(xs, xp, nw, wup, wdn)


def _split_w_in(wt):
  rows = LANES
  assert wt.shape == (SSD_DINNER + CONV_DIM + SSD_HEADS + 4 * HG_WIDTH, D_MODEL)
  return pl.pallas_call(
      _split_w_in_kernel,
      grid=(D_MODEL // rows,),
      in_specs=[pl.BlockSpec((wt.shape[0], rows), lambda i: (0, i))],
      out_specs=pl.BlockSpec((rows, PROJ_W), lambda i: (i, 0)),
      out_shape=jax.ShapeDtypeStruct((D_MODEL, PROJ_W), BF16),
      compiler_params=_params(("arbitrary",)),
      name="split_w_in",
  )(wt)


def _mix_prompt(x1, nmix, wall, prm):
  bsz, seq, _ = x1.shape
  nt = seq // CHUNK
  n_chunks = bsz * nt
  out_shape = (
      jax.ShapeDtypeStruct((bsz, seq, D_MIX), BF16),
      jax.ShapeDtypeStruct((bsz, CONV_W - 1, CONV_DIM), F32),
      jax.ShapeDtypeStruct((bsz, SSD_HEADS, SSD_HEADDIM, SSD_STATE), F32),
      jax.ShapeDtypeStruct((bsz, HG_HEADS, HG_DK, HG_DV), F32),
  )
  g = PROMPT_SEQS_PER_STEP
  assert bsz % g == 0
  n_pos = n_chunks // g
  proj_pos = lambda i: jnp.minimum(i, n_pos - 1)
  mix_pos = lambda i: jnp.maximum(i - 1, 0)
  out_specs = (
      pl.BlockSpec((g, CHUNK, D_MIX), lambda i: (mix_pos(i) // nt, mix_pos(i) % nt, 0)),
      pl.BlockSpec((g, CONV_W - 1, CONV_DIM), lambda i: (mix_pos(i) // nt, 0, 0)),
      pl.BlockSpec((g, SSD_HEADS, SSD_HEADDIM, SSD_STATE), lambda i: (mix_pos(i) // nt, 0, 0, 0)),
      pl.BlockSpec((g, HG_HEADS, HG_DK, HG_DV), lambda i: (mix_pos(i) // nt, 0, 0, 0)),
  )
  in_specs = [pl.BlockSpec((g, CHUNK, D_MODEL), lambda i: (proj_pos(i) // nt, proj_pos(i) % nt, 0)),
              _resident(nmix.shape)] + [_resident(a.shape) for a in (wall, *prm)]
  return pl.pallas_call(
      functools.partial(_mix_prompt_kernel, chunks_per_seq=nt),
      grid=(n_pos + 1,),
      in_specs=in_specs,
      out_specs=out_specs,
      out_shape=out_shape,
      scratch_shapes=[pltpu.VMEM((g, SUBLANES + CHUNK, CONV_DIM), F32), pltpu.VMEM((g * CHUNK, PROJ_W), F32),
                      pltpu.VMEM((g * CHUNK, PROJ_W), F32)],
      compiler_params=_params(("arbitrary",)),
      name="mix_prompt",
  )(x1, nmix, wall, *prm)


def _proj_sample(x1p, nmix, wall):
  n = x1p.shape[0]
  tile = 256
  return pl.pallas_call(
      _proj_sample_kernel,
      grid=(n // tile,),
      in_specs=[pl.BlockSpec((tile, D_MODEL), lambda i: (i, 0)), _resident(nmix.shape), _resident(wall.shape)],
      out_specs=pl.BlockSpec((tile, PROJ_W), lambda i: (i, 0)),
      out_shape=jax.ShapeDtypeStruct((n, PROJ_W), F32),
      compiler_params=_params(("arbitrary",)),
      name="proj_sample",
  )(x1p, nmix, wall)


def _rec_sample(proj, conv0, ssm0, hg0, prm, valid):
  bsz = conv0.shape[0]
  assert proj.shape[0] == bsz * valid
  out_shape = (
      jax.ShapeDtypeStruct((bsz, SAMPLE_ROWS, D_MIX), BF16),
      jax.ShapeDtypeStruct((bsz, CONV_W - 1, CONV_DIM), F32),
      jax.ShapeDtypeStruct((bsz, SSD_HEADS, SSD_HEADDIM, SSD_STATE), F32),
      jax.ShapeDtypeStruct((bsz, HG_HEADS, HG_DK, HG_DV), F32),
  )
  g = SAMPLE_SEQS_PER_STEP
  state_specs = [
      pl.BlockSpec((g, CONV_W - 1, CONV_DIM), lambda b: (b, 0, 0)),
      pl.BlockSpec((g, SSD_HEADS, SSD_HEADDIM, SSD_STATE), lambda b: (b, 0, 0, 0)),
      pl.BlockSpec((g, HG_HEADS, HG_DK, HG_DV), lambda b: (b, 0, 0, 0)),
  ]
  in_specs = [pl.BlockSpec((g * valid, PROJ_W), lambda b: (b, 0))] + state_specs + [_resident(a.shape) for a in prm]
  out_specs = tuple([pl.BlockSpec((g, SAMPLE_ROWS, D_MIX), lambda b: (b, 0, 0))] + state_specs)
  return pl.pallas_call(
      functools.partial(_rec_sample_kernel, valid=valid),
      grid=(bsz // g,),
      in_specs=in_specs,
      out_specs=out_specs,
      out_shape=out_shape,
      scratch_shapes=[pltpu.VMEM((g, 2 * SUBLANES, CONV_DIM), F32), pltpu.VMEM((g, SAMPLE_ROWS, PROJ_W), F32)],
      compiler_params=_params(("arbitrary",)),
      name="rec_sample",
  )(proj, conv0, ssm0, hg0, *prm)


def _tail(x1s, x1p, mixs, mixp, ps, pp, wout, nf2, wup, wdn, nple, wgate, wproj, ppost, nfin):
  assert x1s.shape[0] == FFN_TILE and x1p.shape[0] % FFN_TILE == 0
  weights = (wout, nf2, wup, wdn, nple, wgate, wproj, ppost, nfin)
  n_prompt_tiles = x1p.shape[0] // FFN_TILE

  def in_specs(width):
    sample, prompt = _sample_then_prompt_specs(width, n_prompt_tiles, first_step=W_STEPS)
    return pl.BlockSpec(sample.block_shape, sample.index_map, pipeline_mode=pl.Buffered(1)), prompt

  weight_specs = [_resident(wout.shape), _resident(nf2.shape), *_ffn_weight_specs()] + [
      _resident(w.shape) for w in (nple, wgate, wproj, ppost, nfin)]
  return pl.pallas_call(
      _tail_kernel,
      grid=(W_STEPS + 1 + n_prompt_tiles,),
      in_specs=[*in_specs(D_MODEL), *in_specs(D_MIX), *in_specs(PLE_DIM)] + weight_specs,
      out_specs=_sample_then_prompt_specs(D_MODEL, n_prompt_tiles, first_step=W_STEPS),
      out_shape=(jax.ShapeDtypeStruct(x1s.shape, F32), jax.ShapeDtypeStruct(x1p.shape, F32)),
      scratch_shapes=FFN_WEIGHT_SCRATCH,
      compiler_params=_params(("arbitrary",)),
      name="tail",
  )(x1s, x1p, mixs, mixp, ps, pp, *weights)


def kernel(x_prompt, x_sample, state_conv, state_ssm, state_hgrn, p_prompt, p_sample, norm_ffn1, w_ffn1_up, w_ffn1_down, norm_mix, w_in, conv_w, conv_b, dt_bias, a_log, d_skip, ssd_norm, hg_lb_logits, hg_norm, w_out, norm_ffn2, w_ffn2_up, w_ffn2_down, norm_ple, w_ple_gate, w_ple_proj, ple_post_norm, norm_final):
  bp, seq, _ = x_prompt.shape
  bs, dec_seq, _ = x_sample.shape
  row = lambda v: v.reshape(1, -1).astype(F32)
  per_head = lambda v: jnp.repeat(v.astype(F32), SSD_HEADDIM).reshape(1, SSD_DINNER)
  head_lanes = lambda v: jnp.pad(v.astype(F32), (0, LANES - SSD_HEADS)).reshape(1, LANES)

  w1u, w1d = w_ffn1_up[0].astype(F32), w_ffn1_down[0].astype(F32)
  w2u, w2d = w_ffn2_up[0].astype(F32), w_ffn2_down[0].astype(F32)
  wall = _split_w_in(jnp.swapaxes(w_in[0], 0, 1))
  prm = MixParams(
      conv_w=conv_w[0].astype(F32), conv_b=row(conv_b[0]), dtb_c=head_lanes(dt_bias[0]),
      alog_c=head_lanes(a_log[0]), dskip_e=per_head(d_skip[0]), ssd_norm=row(ssd_norm[0]),
      lb_logits=hg_lb_logits.astype(F32), hg_norm=row(hg_norm[0]))
  nmix = row(norm_mix[0])
  tail_w = (w_out[0].astype(BF16), row(norm_ffn2[0]), w2u, w2d, row(norm_ple[0]), w_ple_gate[0].astype(BF16),
            w_ple_proj[0].astype(BF16), row(ple_post_norm[0]), row(norm_final))

  xp = x_prompt.reshape(bp * seq, D_MODEL)
  xs = x_sample.reshape(bs * dec_seq, D_MODEL)
  x1s, x1p = _ffn1(xs, xp, row(norm_ffn1[0]), w1u, w1d)

  mixed_p, conv_p, ssm_p, hg_p = _mix_prompt(x1p.reshape(bp, seq, D_MODEL), nmix, wall, prm)
  proj_s = _proj_sample(x1s, nmix, wall)
  mixed_s, conv_s, ssm_s, hg_s = _rec_sample(proj_s, state_conv[0], state_ssm[0], state_hgrn[0], prm, dec_seq)
  mixed_s = mixed_s[:, :dec_seq].reshape(bs * dec_seq, D_MIX)

  y_sample, y_prompt = _tail(x1s, x1p, mixed_s, mixed_p.reshape(bp * seq, D_MIX),
                             p_sample[0].reshape(bs * dec_seq, PLE_DIM), p_prompt[0].reshape(bp * seq, PLE_DIM), *tail_w)

  return (y_prompt.reshape(bp, seq, D_MODEL), y_sample.reshape(bs, dec_seq, D_MODEL), conv_p[None], ssm_p[None],
          hg_p[None], conv_s[None], ssm_s[None], hg_s[None])
```

```python
import functools
from typing import NamedTuple

import jax
import jax.numpy as jnp
from jax import lax
from jax.experimental import pallas as pl
from jax.experimental.pallas import tpu as pltpu

F32 = jnp.float32
BF16 = jnp.bfloat16

D_MODEL = 1024
D_FF = 2816
PLE_DIM = 256
EPS = 1e-6
SSD_HEADS = 16
SSD_HEADDIM = 64
SSD_DINNER = SSD_HEADS * SSD_HEADDIM
SSD_STATE = 128
SSD_GROUPS = 2
SSD_HEADS_PER_GROUP = SSD_HEADS // SSD_GROUPS
SSD_GROUP_WIDTH = SSD_DINNER // SSD_GROUPS
CONV_W = 4
CONV_DIM = SSD_DINNER + 2 * SSD_GROUPS * SSD_STATE
HG_HEADS = 8
HG_DK = 128
HG_DV = 128
HG_WIDTH = HG_HEADS * HG_DV
HG_BLOCK = 16
D_MIX = SSD_DINNER + HG_WIDTH

LANES = 128
SUBLANES = 8
VMEM_LIMIT_BYTES = 56 * 1024 * 1024

CHUNK = 128
SAMPLE_ROWS = 8
SAMPLE_SEQS_PER_STEP = 8
FFN_TILE = 512
FF_CHUNK = 256
N_FF_CHUNKS = D_FF // FF_CHUNK
W_STEPS = N_FF_CHUNKS
UP_BLOCK = 2 * D_FF // W_STEPS

OFF_Z = 0
OFF_XBC = OFF_Z + SSD_DINNER
OFF_Q = OFF_XBC + CONV_DIM
OFF_FR = OFF_Q + HG_WIDTH
OFF_IV = OFF_FR + HG_WIDTH
OFF_OG = OFF_IV + HG_WIDTH
OFF_DT = OFF_OG + HG_WIDTH
PROJ_W = OFF_DT + LANES
PROJ_BLOCK = 256
MIX_PHASE_WEIGHT = 34


class MixParams(NamedTuple):
  conv_w: jax.Array
  conv_b: jax.Array
  dtb_c: jax.Array
  alog_c: jax.Array
  dskip_e: jax.Array
  ssd_norm: jax.Array
  lb_logits: jax.Array
  hg_norm: jax.Array


def _dot(a, b):
  return jnp.dot(a, b, preferred_element_type=F32)


def _dot_nt(a, b):
  return lax.dot_general(a, b, (((1,), (1,)), ((), ())), preferred_element_type=F32)


def _dot_tn(a, b):
  return lax.dot_general(a, b, (((0,), (0,)), ((), ())), preferred_element_type=F32)


NEG_LOG2_E = -1.4426950408889634


def _sigmoid(x):
  return 1.0 / (1.0 + jnp.exp2(x * NEG_LOG2_E))


def _silu(x):
  return x * _sigmoid(x)


def _softplus(x):
  return jnp.maximum(x, 0.0) + jnp.log1p(jnp.exp(-jnp.abs(x)))


def _rmsnorm(x, w):
  ms = jnp.mean(x * x, axis=-1, keepdims=True)
  return x * lax.rsqrt(ms + EPS) * w


def _cumsum_rows(x):
  c, n = x.shape
  g = c // SUBLANES
  x3 = x.reshape(g, SUBLANES, n)
  sub = lax.broadcasted_iota(jnp.int32, (1, SUBLANES, n), 1)
  s = 1
  while s < SUBLANES:
    x3 = x3 + jnp.where(sub >= s, pltpu.roll(x3, s, axis=1), 0.0)
    s *= 2
  if g > 1:
    tot = jnp.broadcast_to(x3[:, SUBLANES - 1:SUBLANES, :], (g, SUBLANES, n))
    offs = [jnp.zeros((1, SUBLANES, n), x.dtype)]
    for k in range(1, g):
      offs.append(offs[-1] + tot[k - 1:k])
    x3 = x3 + jnp.concatenate(offs, axis=0)
  return x3.reshape(c, n)


def _expand_heads(xc):
  c = xc.shape[0]
  first_head = lax.broadcasted_iota(jnp.int32, (1, LANES), 1) < SSD_HEADDIM
  parts = []
  for j in range(SSD_HEADS // 2):
    a = jnp.broadcast_to(xc[:, 2 * j:2 * j + 1], (c, LANES))
    b = jnp.broadcast_to(xc[:, 2 * j + 1:2 * j + 2], (c, LANES))
    parts.append(jnp.where(first_head, a, b))
  return jnp.concatenate(parts, axis=1)


def _col_matrix(row_vec, n_lanes):
  n = row_vec.shape[1]
  return jnp.transpose(jnp.broadcast_to(row_vec, (n_lanes, n)))


def _mix_chunk(proj_ref, *, c, valid, p, xb_ref, ssm_ref, hg_ref, ssm_out_ref=None, hg_out_ref=None):
  cs = max(c, LANES)

  def padr(x):
    if x.shape[0] == cs:
      return x
    return jnp.concatenate([x, jnp.zeros((cs - x.shape[0], x.shape[1]), x.dtype)], axis=0)

  row_valid = None
  if valid < c:
    row_valid = lax.broadcasted_iota(jnp.int32, (c, 1), 0) < valid

  def mask_rows(x):
    return x if row_valid is None else jnp.where(row_valid, x, 0.0)

  cols = lambda off, width: proj_ref[:, off:off + width]

  xbc = cols(OFF_XBC, CONV_DIM)
  xb_ref[SUBLANES:SUBLANES + c, :] = xbc
  conv = p.conv_b + p.conv_w[CONV_W - 1:CONV_W] * xbc
  for j in range(CONV_W - 1):
    conv = conv + p.conv_w[j:j + 1] * xb_ref[SUBLANES - (CONV_W - 1) + j:SUBLANES - (CONV_W - 1) + j + c, :]
  yield 2
  conv = _silu(conv)
  yield 2
  xs = conv[:, :SSD_DINNER]
  bm = conv[:, SSD_DINNER:SSD_DINNER + SSD_GROUPS * SSD_STATE]
  cm = conv[:, SSD_DINNER + SSD_GROUPS * SSD_STATE:]

  head_lane = lax.broadcasted_iota(jnp.int32, (1, LANES), 1) < SSD_HEADS
  dt_c = mask_rows(jnp.where(head_lane, _softplus(cols(OFF_DT, LANES) + p.dtb_c), 0.0))
  acs_c = _cumsum_rows(dt_c * -jnp.exp(p.alog_c))
  last_c = acs_c[c - 1:c, :]
  dec_in = _expand_heads(jnp.exp(acs_c))
  xd_end = xs * _expand_heads(dt_c * jnp.exp(last_c - acs_c))
  yield 2
  acs_t = jnp.transpose(padr(acs_c))
  dt_t = jnp.transpose(padr(dt_c))
  chunk_decay_t = jnp.transpose(jnp.broadcast_to(jnp.exp(last_c), (LANES, LANES)))

  t_idx = lax.broadcasted_iota(jnp.int32, (c, cs), 0)
  s_idx = lax.broadcasted_iota(jnp.int32, (c, cs), 1)
  causal = s_idx <= t_idx
  causal_bias = jnp.where(causal, 0.0, -jnp.inf)
  lane = lax.broadcasted_iota(jnp.int32, (1, LANES), 1)
  first_head = lane < SSD_HEADDIM

  group_heads = [range(g * SSD_HEADS_PER_GROUP, (g + 1) * SSD_HEADS_PER_GROUP) for g in range(SSD_GROUPS)]
  bg_p = [padr(bm[:, g * SSD_STATE:(g + 1) * SSD_STATE]).astype(BF16) for g in range(SSD_GROUPS)]
  cg = [cm[:, g * SSD_STATE:(g + 1) * SSD_STATE].astype(BF16) for g in range(SSD_GROUPS)]
  h0 = [ssm_ref[heads.start:heads.stop].reshape(SSD_GROUP_WIDTH, SSD_STATE) for heads in group_heads]
  cb = [_dot_nt(cg[g], bg_p[g]) for g in range(SSD_GROUPS)]
  y_inter = [_dot_nt(cg[g], h0[g].astype(BF16)) for g in range(SSD_GROUPS)]
  xs_bf = padr(xs).astype(BF16)
  zero_bf = jnp.zeros((cs, LANES), BF16)
  y_intra = []
  for g in range(SSD_GROUPS):
    for j in range(SSD_HEADS_PER_GROUP // 2):
      h_first = g * SSD_HEADS_PER_GROUP + 2 * j
      scores = []
      for h in (h_first, h_first + 1):
        diff = acs_c[:, h:h + 1] - acs_t[h:h + 1, :]
        decay_dt = jnp.exp(diff + causal_bias) * dt_t[h:h + 1, :]
        scores.append((cb[g] * decay_dt).astype(BF16))
      pair = xs_bf[:, h_first * SSD_HEADDIM:(h_first + 2) * SSD_HEADDIM]
      rhs = jnp.concatenate([jnp.where(first_head, pair, zero_bf), jnp.where(first_head, zero_bf, pair)], axis=0)
      y_intra.append(_dot(jnp.concatenate(scores, axis=1), rhs))
      yield 1
  for g, heads in enumerate(group_heads):
    xd_g = padr(xd_end[:, g * SSD_GROUP_WIDTH:(g + 1) * SSD_GROUP_WIDTH]).astype(BF16)
    decay_cols = jnp.concatenate(
        [jnp.broadcast_to(chunk_decay_t[h:h + 1, :], (SSD_HEADDIM, SSD_STATE)) for h in heads], axis=0)
    h1 = h0[g] * decay_cols + _dot_tn(xd_g, bg_p[g])
    ssm_dst = ssm_ref if ssm_out_ref is None else ssm_out_ref
    ssm_dst[heads.start:heads.stop] = h1.reshape(SSD_HEADS_PER_GROUP, SSD_HEADDIM, SSD_STATE)
  y = jnp.concatenate(y_intra, axis=1) + jnp.concatenate(y_inter, axis=1) * dec_in + p.dskip_e * xs
  yield 1
  yg = y * _silu(cols(OFF_Z, SSD_DINNER))
  yield 1
  y_norm = []
  for g in range(SSD_GROUPS):
    blk = yg[:, g * SSD_GROUP_WIDTH:(g + 1) * SSD_GROUP_WIDTH]
    y_norm.append(blk * lax.rsqrt(jnp.mean(blk * blk, axis=-1, keepdims=True) + EPS))
  y_ssd = jnp.concatenate(y_norm, axis=1) * p.ssd_norm

  lg = p.lb_logits
  lg_max = jnp.max(lg, axis=0, keepdims=True)
  lg_exp = jnp.exp(lg - lg_max)
  lb = lg_exp[0:1] / jnp.sum(lg_exp, axis=0, keepdims=True)
  f = lb + (1.0 - lb) * _sigmoid(cols(OFF_FR, HG_WIDTH))
  yield 1
  logf = mask_rows(jnp.log(f))
  kk = mask_rows(1.0 - f)
  qq = _silu(cols(OFF_Q, HG_WIDTH))
  yield 1
  b = _cumsum_rows(logf)
  yield 2
  b_last = b[c - 1:c, :]
  q_inter = (qq * jnp.exp(b)).astype(BF16)
  k_end = kk * jnp.exp(b_last - b)
  state_decay = jnp.exp(b_last)
  yield 1

  levels = []
  m = c // 2
  while m >= HG_BLOCK:
    q_rows, k_rows = [], []
    zero_half = jnp.zeros((m, HG_WIDTH), BF16)
    for i in range(c // (2 * m)):
      lo = i * 2 * m
      b_ref_row = b[lo + m - 1:lo + m, :]
      k_rows += [(kk[lo:lo + m] * jnp.exp(b_ref_row - b[lo:lo + m])).astype(BF16), zero_half]
      q_rows += [zero_half, (qq[lo + m:lo + 2 * m] * jnp.exp(b[lo + m:lo + 2 * m] - b_ref_row)).astype(BF16)]
    shift = m.bit_length() - 1
    mask = ((t_idx >> (shift + 1)) == (s_idx >> (shift + 1))) & (((t_idx >> shift) & 1) == 1) & (
        ((s_idx >> shift) & 1) == 0)
    levels.append((jnp.concatenate(q_rows, axis=0), jnp.concatenate(k_rows, axis=0), mask))
    yield 1
    m //= 2
  blk = min(HG_BLOCK, c)
  q_rows, k_rows = [], []
  for i in range(c // blk):
    lo = i * blk
    b_loc = b[lo:lo + blk] if i == 0 else b[lo:lo + blk] - b[lo - 1:lo, :]
    q_rows.append(qq[lo:lo + blk] * jnp.exp(b_loc))
    k_rows.append(kk[lo:lo + blk] * jnp.exp(-b_loc))
  shift = blk.bit_length() - 1
  levels.append((jnp.concatenate(q_rows, axis=0).astype(BF16), padr(jnp.concatenate(k_rows, axis=0)).astype(BF16),
                 ((t_idx >> shift) == (s_idx >> shift)) & causal))

  yield 2
  head_slices = [slice(h * HG_DK, (h + 1) * HG_DK) for h in range(HG_HEADS)]
  v_bf = padr(cols(OFF_IV, HG_WIDTH)).astype(BF16)
  k_end_bf = padr(k_end).astype(BF16)
  att_bf = []
  for sl in head_slices:
    att = None
    for li, (ql, kl, mk) in enumerate(levels):
      block_scores = _dot_nt(ql[:, sl], kl[:, sl])
      if li == 0 and len(levels) > 1:
        att = block_scores
      else:
        att = jnp.where(mk, block_scores, 0.0 if att is None else att)
    att_bf.append(att.astype(BF16))
    yield 1
  s0 = [hg_ref[h] for h in range(HG_HEADS)]
  o_heads = [_dot(att_bf[h], v_bf[:, sl]) + _dot(q_inter[:, sl], s0[h].astype(BF16))
             for h, sl in enumerate(head_slices)]
  for h, sl in enumerate(head_slices):
    hg_dst = hg_ref if hg_out_ref is None else hg_out_ref
    hg_dst[h] = s0[h] * _col_matrix(state_decay[:, sl], HG_DV) + _dot_tn(k_end_bf[:, sl], v_bf[:, sl])
  o_parts = [o_h * lax.rsqrt(jnp.mean(o_h * o_h, axis=-1, keepdims=True) + EPS) for o_h in o_heads]
  o = jnp.concatenate(o_parts, axis=1) * p.hg_norm * _silu(cols(OFF_OG, HG_WIDTH))
  return jnp.concatenate([y_ssd, o], axis=1)


def _load_mix_params(refs):
  return MixParams(*[r[...] for r in refs])


def _drive(gen, between=lambda n: None):
  while True:
    try:
      between(next(gen))
    except StopIteration as done:
      return done.value


def _drive_round_robin(gens):
  results = [None] * len(gens)
  active = list(range(len(gens)))
  while active:
    for idx in list(active):
      try:
        next(gens[idx])
      except StopIteration as done:
        results[idx] = done.value
        active.remove(idx)
  return results


def _stage_ffn_weights(step, wup_f32_ref, wdn_f32_ref, wup_ref, wdn_ref):
  wup_ref[step] = wup_f32_ref[...].astype(BF16)
  wdn_ref[step] = wdn_f32_ref[...].astype(BF16)


def _staged_up_cols(wup_ref, lo):
  return wup_ref[lo // UP_BLOCK, :, lo % UP_BLOCK:lo % UP_BLOCK + FF_CHUNK]


def _swiglu(h, wup_ref, wdn_ref):
  acc = jnp.zeros((h.shape[0], D_MODEL), F32)
  for ci in range(N_FF_CHUNKS):
    lo = ci * FF_CHUNK
    g = _dot(h, _staged_up_cols(wup_ref, lo))
    u = _dot(h, _staged_up_cols(wup_ref, D_FF + lo))
    acc = acc + _dot((_silu(g) * u).astype(BF16), wdn_ref[ci])
  return acc


def _first_tile_select(first_ref, rest_ref, first_step):
  return jnp.where(pl.program_id(0) == first_step, first_ref[...], rest_ref[...])


def _ffn1_kernel(xs_ref, xp_ref, nw_ref, wup_f32_ref, wdn_f32_ref, os_ref, op_ref, wup_ref, wdn_ref):
  i = pl.program_id(0)

  @pl.when(i < W_STEPS)
  def _():
    _stage_ffn_weights(i, wup_f32_ref, wdn_f32_ref, wup_ref, wdn_ref)

  @pl.when(i >= W_STEPS)
  def _():
    x = _first_tile_select(xs_ref, xp_ref, W_STEPS)
    h = _rmsnorm(x, nw_ref[...]).astype(BF16)
    op_ref[...] = x + 0.5 * _swiglu(h, wup_ref, wdn_ref)

  @pl.when(i == W_STEPS)
  def _():
    os_ref[...] = op_ref[...]


def _split_w_in_kernel(wt_ref, wall_ref):
  o_dt = SSD_DINNER + CONV_DIM
  o_q = o_dt + SSD_HEADS
  wall_ref[:, OFF_Z:OFF_Q] = jnp.transpose(wt_ref[0:o_dt, :]).astype(BF16)
  wall_ref[:, OFF_Q:OFF_DT] = jnp.transpose(wt_ref[o_q:, :]).astype(BF16)
  dt_rows = jnp.concatenate([wt_ref[o_dt:o_q, :], jnp.zeros((LANES - SSD_HEADS, LANES), F32)], axis=0)
  wall_ref[:, OFF_DT:PROJ_W] = jnp.transpose(dt_rows).astype(BF16)


def _proj_blocks():
  return [(lo, min(lo + PROJ_BLOCK, PROJ_W)) for lo in range(0, PROJ_W, PROJ_BLOCK)]


def _mix_prompt_kernel(x_ref, nmix_ref, wall_ref, *rest, chunks_per_seq):
  prm_refs = rest[:len(MixParams._fields)]
  mixed_ref, conv_ref, ssm_ref, hg_ref, xb_ref, proj_a, proj_b = rest[len(MixParams._fields):]
  i = pl.program_id(0)
  t_cur = lax.rem(jnp.maximum(i - 1, 0), chunks_per_seq)

  @pl.when(i == 0)
  def _():
    proj_b[...] = jnp.zeros(proj_b.shape, F32)

  @pl.when(t_cur == 0)
  def _():
    xb_ref[0:SUBLANES, :] = jnp.zeros((SUBLANES, CONV_DIM), F32)
    ssm_ref[...] = jnp.zeros(ssm_ref.shape, F32)
    hg_ref[...] = jnp.zeros(hg_ref.shape, F32)

  def step(cur_ref, next_ref):
    h = _rmsnorm(x_ref[0], nmix_ref[...]).astype(BF16)
    pending = _proj_blocks()
    n_blocks = len(pending)
    credit = [0.0]

    def project_blocks(n):
      credit[0] += n * n_blocks / MIX_PHASE_WEIGHT
      while pending and credit[0] >= 1.0:
        credit[0] -= 1.0
        lo, hi = pending.pop(0)
        next_ref[:, lo:hi] = _dot(h, wall_ref[:, lo:hi])

    mixed = _drive(_mix_chunk(cur_ref, c=CHUNK, valid=CHUNK, p=_load_mix_params(prm_refs), xb_ref=xb_ref,
                              ssm_ref=ssm_ref.at[0], hg_ref=hg_ref.at[0]), project_blocks)
    project_blocks(MIX_PHASE_WEIGHT)
    mixed_ref[0] = mixed.astype(BF16)
    tail = xb_ref[CHUNK:CHUNK + SUBLANES, :]
    xb_ref[0:SUBLANES, :] = tail

  parity = lax.rem(i, 2)
  pl.when(parity == 0)(functools.partial(step, proj_b, proj_a))
  pl.when(parity == 1)(functools.partial(step, proj_a, proj_b))

  @pl.when((t_cur == chunks_per_seq - 1) & (i > 0))
  def _():
    conv_ref[0] = xb_ref[SUBLANES + CHUNK - (CONV_W - 1):SUBLANES + CHUNK, :]


def _proj_sample_kernel(x_ref, nmix_ref, wall_ref, o_ref):
  h = _rmsnorm(x_ref[...], nmix_ref[...]).astype(BF16)
  for lo, hi in _proj_blocks():
    o_ref[:, lo:hi] = _dot(h, wall_ref[:, lo:hi])


def _rec_sample_kernel(proj_ref, conv_in_ref, ssm_in_ref, hg_in_ref, *rest, valid):
  prm_refs = rest[:len(MixParams._fields)]
  mixed_ref, conv_ref, ssm_ref, hg_ref, xb_ref, padded_ref = rest[len(MixParams._fields):]
  n_seq = conv_in_ref.shape[0]
  prm = _load_mix_params(prm_refs)
  gens = []
  for s in range(n_seq):
    xb_ref[s, 0:SUBLANES, :] = jnp.zeros((SUBLANES, CONV_DIM), F32)
    xb_ref[s, SUBLANES - (CONV_W - 1):SUBLANES, :] = conv_in_ref[s]
    padded_ref[s, 0:valid, :] = proj_ref[s * valid:(s + 1) * valid, :]
    padded_ref[s, valid:SAMPLE_ROWS, :] = jnp.zeros((SAMPLE_ROWS - valid, PROJ_W), F32)
    gens.append(_mix_chunk(padded_ref.at[s], c=SAMPLE_ROWS, valid=valid, p=prm, xb_ref=xb_ref.at[s],
                           ssm_ref=ssm_in_ref.at[s], hg_ref=hg_in_ref.at[s],
                           ssm_out_ref=ssm_ref.at[s], hg_out_ref=hg_ref.at[s]))
  for s, mixed in enumerate(_drive_round_robin(gens)):
    mixed_ref[s] = mixed.astype(BF16)
    conv_ref[s] = xb_ref[s, SUBLANES + valid - (CONV_W - 1):SUBLANES + valid, :]


def _tail_kernel(x1s_ref, x1p_ref, mixs_ref, mixp_ref, ps_ref, pp_ref, wout_ref, nf2_ref, wup_f32_ref, wdn_f32_ref,
                 nple_ref, wgate_ref, wproj_ref, ppost_ref, nfin_ref, os_ref, op_ref, wup_ref, wdn_ref):
  i = pl.program_id(0)

  @pl.when(i < W_STEPS)
  def _():
    _stage_ffn_weights(i, wup_f32_ref, wdn_f32_ref, wup_ref, wdn_ref)

  @pl.when(i >= W_STEPS)
  def _():
    pick = lambda s_ref, p_ref: _first_tile_select(s_ref, p_ref, W_STEPS)
    x2 = pick(x1s_ref, x1p_ref) + _dot(pick(mixs_ref, mixp_ref), wout_ref[...])
    h = _rmsnorm(x2, nf2_ref[...]).astype(BF16)
    e = _rmsnorm(_dot(pick(ps_ref, pp_ref).astype(BF16), wproj_ref[...]), ppost_ref[...])
    x3 = x2 + 0.5 * _swiglu(h, wup_ref, wdn_ref)
    gate = _sigmoid(_dot(_rmsnorm(x3, nple_ref[...]).astype(BF16), wgate_ref[...]))
    x4 = x3 + gate * e
    op_ref[...] = _rmsnorm(x4, nfin_ref[...])

  @pl.when(i == W_STEPS)
  def _():
    os_ref[...] = op_ref[...]


def _resident(shape):
  nd = len(shape)
  return pl.BlockSpec(shape, lambda *_: (0,) * nd, pipeline_mode=pl.Buffered(1))


def _params(semantics):
  return pltpu.CompilerParams(dimension_semantics=semantics, vmem_limit_bytes=VMEM_LIMIT_BYTES)


def _sample_then_prompt_specs(width, n_prompt_tiles, first_step):
  sample = pl.BlockSpec((FFN_TILE, width), lambda i: (0, 0))
  prompt = pl.BlockSpec((FFN_TILE, width), lambda i: (jnp.clip(i - first_step - 1, 0, n_prompt_tiles - 1), 0))
  return sample, prompt


def _ffn_weight_specs():
  staged = lambda i: jnp.minimum(i, W_STEPS - 1)
  return [pl.BlockSpec((D_MODEL, UP_BLOCK), lambda i: (0, staged(i))),
          pl.BlockSpec((FF_CHUNK, D_MODEL), lambda i: (staged(i), 0))]


FFN_WEIGHT_SCRATCH = [pltpu.VMEM((W_STEPS, D_MODEL, UP_BLOCK), BF16), pltpu.VMEM((W_STEPS, FF_CHUNK, D_MODEL), BF16)]


def _ffn1(xs, xp, nw, wup, wdn):
  assert xs.shape[0] == FFN_TILE and xp.shape[0] % FFN_TILE == 0
  n_prompt_tiles = xp.shape[0] // FFN_TILE
  specs = _sample_then_prompt_specs(D_MODEL, n_prompt_tiles, first_step=W_STEPS)
  return pl.pallas_call(
      _ffn1_kernel,
      grid=(W_STEPS + 1 + n_prompt_tiles,),
      in_specs=[*specs, _resident(nw.shape), *_ffn_weight_specs()],
      out_specs=specs,
      out_shape=(jax.ShapeDtypeStruct(xs.shape, F32), jax.ShapeDtypeStruct(xp.shape, F32)),
      scratch_shapes=FFN_WEIGHT_SCRATCH,
      compiler_params=_params(("arbitrary",)),
      name="ffn1",
  )(xs, xp, nw, wup, wdn)


def _split_w_in(wt):
  rows = LANES
  assert wt.shape == (SSD_DINNER + CONV_DIM + SSD_HEADS + 4 * HG_WIDTH, D_MODEL)
  return pl.pallas_call(
      _split_w_in_kernel,
      grid=(D_MODEL // rows,),
      in_specs=[pl.BlockSpec((wt.shape[0], rows), lambda i: (0, i))],
      out_specs=pl.BlockSpec((rows, PROJ_W), lambda i: (i, 0)),
      out_shape=jax.ShapeDtypeStruct((D_MODEL, PROJ_W), BF16),
      compiler_params=_params(("arbitrary",)),
      name="split_w_in",
  )(wt)


def _mix_prompt(x1, nmix, wall, prm):
  bsz, seq, _ = x1.shape
  nt = seq // CHUNK
  n_chunks = bsz * nt
  out_shape = (
      jax.ShapeDtypeStruct((bsz, seq, D_MIX), BF16),
      jax.ShapeDtypeStruct((bsz, CONV_W - 1, CONV_DIM), F32),
      jax.ShapeDtypeStruct((bsz, SSD_HEADS, SSD_HEADDIM, SSD_STATE), F32),
      jax.ShapeDtypeStruct((bsz, HG_HEADS, HG_DK, HG_DV), F32),
  )
  proj_chunk = lambda i: jnp.minimum(i, n_chunks - 1)
  mix_chunk = lambda i: jnp.maximum(i - 1, 0)
  out_specs = (
      pl.BlockSpec((1, CHUNK, D_MIX), lambda i: (mix_chunk(i) // nt, mix_chunk(i) % nt, 0)),
      pl.BlockSpec((1, CONV_W - 1, CONV_DIM), lambda i: (mix_chunk(i) // nt, 0, 0)),
      pl.BlockSpec((1, SSD_HEADS, SSD_HEADDIM, SSD_STATE), lambda i: (mix_chunk(i) // nt, 0, 0, 0)),
      pl.BlockSpec((1, HG_HEADS, HG_DK, HG_DV), lambda i: (mix_chunk(i) // nt, 0, 0, 0)),
  )
  in_specs = [pl.BlockSpec((1, CHUNK, D_MODEL), lambda i: (proj_chunk(i) // nt, proj_chunk(i) % nt, 0)),
              _resident(nmix.shape)] + [_resident(a.shape) for a in (wall, *prm)]
  return pl.pallas_call(
      functools.partial(_mix_prompt_kernel, chunks_per_seq=nt),
      grid=(n_chunks + 1,),
      in_specs=in_specs,
      out_specs=out_specs,
      out_shape=out_shape,
      scratch_shapes=[pltpu.VMEM((SUBLANES + CHUNK, CONV_DIM), F32), pltpu.VMEM((CHUNK, PROJ_W), F32),
                      pltpu.VMEM((CHUNK, PROJ_W), F32)],
      compiler_params=_params(("arbitrary",)),
      name="mix_prompt",
  )(x1, nmix, wall, *prm)


def _proj_sample(x1p, nmix, wall):
  n = x1p.shape[0]
  tile = 256
  return pl.pallas_call(
      _proj_sample_kernel,
      grid=(n // tile,),
      in_specs=[pl.BlockSpec((tile, D_MODEL), lambda i: (i, 0)), _resident(nmix.shape), _resident(wall.shape)],
      out_specs=pl.BlockSpec((tile, PROJ_W), lambda i: (i, 0)),
      out_shape=jax.ShapeDtypeStruct((n, PROJ_W), F32),
      compiler_params=_params(("arbitrary",)),
      name="proj_sample",
  )(x1p, nmix, wall)


def _rec_sample(proj, conv0, ssm0, hg0, prm, valid):
  bsz = conv0.shape[0]
  assert proj.shape[0] == bsz * valid
  out_shape = (
      jax.ShapeDtypeStruct((bsz, SAMPLE_ROWS, D_MIX), BF16),
      jax.ShapeDtypeStruct((bsz, CONV_W - 1, CONV_DIM), F32),
      jax.ShapeDtypeStruct((bsz, SSD_HEADS, SSD_HEADDIM, SSD_STATE), F32),
      jax.ShapeDtypeStruct((bsz, HG_HEADS, HG_DK, HG_DV), F32),
  )
  g = SAMPLE_SEQS_PER_STEP
  state_specs = [
      pl.BlockSpec((g, CONV_W - 1, CONV_DIM), lambda b: (b, 0, 0)),
      pl.BlockSpec((g, SSD_HEADS, SSD_HEADDIM, SSD_STATE), lambda b: (b, 0, 0, 0)),
      pl.BlockSpec((g, HG_HEADS, HG_DK, HG_DV), lambda b: (b, 0, 0, 0)),
  ]
  in_specs = [pl.BlockSpec((g * valid, PROJ_W), lambda b: (b, 0))] + state_specs + [_resident(a.shape) for a in prm]
  out_specs = tuple([pl.BlockSpec((g, SAMPLE_ROWS, D_MIX), lambda b: (b, 0, 0))] + state_specs)
  return pl.pallas_call(
      functools.partial(_rec_sample_kernel, valid=valid),
      grid=(bsz // g,),
      in_specs=in_specs,
      out_specs=out_specs,
      out_shape=out_shape,
      scratch_shapes=[pltpu.VMEM((g, 2 * SUBLANES, CONV_DIM), F32), pltpu.VMEM((g, SAMPLE_ROWS, PROJ_W), F32)],
      compiler_params=_params(("arbitrary",)),
      name="rec_sample",
  )(proj, conv0, ssm0, hg0, *prm)


def _tail(x1s, x1p, mixs, mixp, ps, pp, wout, nf2, wup, wdn, nple, wgate, wproj, ppost, nfin):
  assert x1s.shape[0] == FFN_TILE and x1p.shape[0] % FFN_TILE == 0
  weights = (wout, nf2, wup, wdn, nple, wgate, wproj, ppost, nfin)
  n_prompt_tiles = x1p.shape[0] // FFN_TILE

  def in_specs(width):
    sample, prompt = _sample_then_prompt_specs(width, n_prompt_tiles, first_step=W_STEPS)
    return pl.BlockSpec(sample.block_shape, sample.index_map, pipeline_mode=pl.Buffered(1)), prompt

  weight_specs = [_resident(wout.shape), _resident(nf2.shape), *_ffn_weight_specs()] + [
      _resident(w.shape) for w in (nple, wgate, wproj, ppost, nfin)]
  return pl.pallas_call(
      _tail_kernel,
      grid=(W_STEPS + 1 + n_prompt_tiles,),
      in_specs=[*in_specs(D_MODEL), *in_specs(D_MIX), *in_specs(PLE_DIM)] + weight_specs,
      out_specs=_sample_then_prompt_specs(D_MODEL, n_prompt_tiles, first_step=W_STEPS),
      out_shape=(jax.ShapeDtypeStruct(x1s.shape, F32), jax.ShapeDtypeStruct(x1p.shape, F32)),
      scratch_shapes=FFN_WEIGHT_SCRATCH,
      compiler_params=_params(("arbitrary",)),
      name="tail",
  )(x1s, x1p, mixs, mixp, ps, pp, *weights)


def kernel(x_prompt, x_sample, state_conv, state_ssm, state_hgrn, p_prompt, p_sample, norm_ffn1, w_ffn1_up, w_ffn1_down, norm_mix, w_in, conv_w, conv_b, dt_bias, a_log, d_skip, ssd_norm, hg_lb_logits, hg_norm, w_out, norm_ffn2, w_ffn2_up, w_ffn2_down, norm_ple, w_ple_gate, w_ple_proj, ple_post_norm, norm_final):
  bp, seq, _ = x_prompt.shape
  bs, dec_seq, _ = x_sample.shape
  row = lambda v: v.reshape(1, -1).astype(F32)
  per_head = lambda v: jnp.repeat(v.astype(F32), SSD_HEADDIM).reshape(1, SSD_DINNER)
  head_lanes = lambda v: jnp.pad(v.astype(F32), (0, LANES - SSD_HEADS)).reshape(1, LANES)

  w1u, w1d = w_ffn1_up[0].astype(F32), w_ffn1_down[0].astype(F32)
  w2u, w2d = w_ffn2_up[0].astype(F32), w_ffn2_down[0].astype(F32)
  wall = _split_w_in(jnp.swapaxes(w_in[0], 0, 1))
  prm = MixParams(
      conv_w=conv_w[0].astype(F32), conv_b=row(conv_b[0]), dtb_c=head_lanes(dt_bias[0]),
      alog_c=head_lanes(a_log[0]), dskip_e=per_head(d_skip[0]), ssd_norm=row(ssd_norm[0]),
      lb_logits=hg_lb_logits.astype(F32), hg_norm=row(hg_norm[0]))
  nmix = row(norm_mix[0])
  tail_w = (w_out[0].astype(BF16), row(norm_ffn2[0]), w2u, w2d, row(norm_ple[0]), w_ple_gate[0].astype(BF16),
            w_ple_proj[0].astype(BF16), row(ple_post_norm[0]), row(norm_final))

  xp = x_prompt.reshape(bp * seq, D_MODEL)
  xs = x_sample.reshape(bs * dec_seq, D_MODEL)
  x1s, x1p = _ffn1(xs, xp, row(norm_ffn1[0]), w1u, w1d)

  mixed_p, conv_p, ssm_p, hg_p = _mix_prompt(x1p.reshape(bp, seq, D_MODEL), nmix, wall, prm)
  proj_s = _proj_sample(x1s, nmix, wall)
  mixed_s, conv_s, ssm_s, hg_s = _rec_sample(proj_s, state_conv[0], state_ssm[0], state_hgrn[0], prm, dec_seq)
  mixed_s = mixed_s[:, :dec_seq].reshape(bs * dec_seq, D_MIX)

  y_sample, y_prompt = _tail(x1s, x1p, mixed_s, mixed_p.reshape(bp * seq, D_MIX),
                             p_sample[0].reshape(bs * dec_seq, PLE_DIM), p_prompt[0].reshape(bp * seq, PLE_DIM), *tail_w)

  return (y_prompt.reshape(bp, seq, D_MODEL), y_sample.reshape(bs, dec_seq, D_MODEL), conv_p[None], ssm_p[None],
          hg_p[None], conv_s[None], ssm_s[None], hg_s[None])
```

```python
import functools
from typing import NamedTuple

import jax
import jax.numpy as jnp
from jax import lax
from jax.experimental import pallas as pl
from jax.experimental.pallas import tpu as pltpu

F32 = jnp.float32
BF16 = jnp.bfloat16

D_MODEL = 1024
D_FF = 2816
PLE_DIM = 256
EPS = 1e-6
SSD_HEADS = 16
SSD_HEADDIM = 64
SSD_DINNER = SSD_HEADS * SSD_HEADDIM
SSD_STATE = 128
SSD_GROUPS = 2
SSD_HEADS_PER_GROUP = SSD_HEADS // SSD_GROUPS
SSD_GROUP_WIDTH = SSD_DINNER // SSD_GROUPS
CONV_W = 4
CONV_DIM = SSD_DINNER + 2 * SSD_GROUPS * SSD_STATE
HG_HEADS = 8
HG_DK = 128
HG_DV = 128
HG_WIDTH = HG_HEADS * HG_DV
HG_BLOCK = 16
D_MIX = SSD_DINNER + HG_WIDTH

LANES = 128
SUBLANES = 8
VMEM_LIMIT_BYTES = 56 * 1024 * 1024

CHUNK = 128
SAMPLE_ROWS = 8
SAMPLE_SEQS_PER_STEP = 8
FFN_TILE = 512
FF_CHUNK = 256
N_FF_CHUNKS = D_FF // FF_CHUNK
W_STEPS = N_FF_CHUNKS
UP_BLOCK = 2 * D_FF // W_STEPS

OFF_Z = 0
OFF_XBC = OFF_Z + SSD_DINNER
OFF_Q = OFF_XBC + CONV_DIM
OFF_FR = OFF_Q + HG_WIDTH
OFF_IV = OFF_FR + HG_WIDTH
OFF_OG = OFF_IV + HG_WIDTH
OFF_DT = OFF_OG + HG_WIDTH
PROJ_W = OFF_DT + LANES
PROJ_BLOCK = 256
MIX_PHASE_WEIGHT = 34


class MixParams(NamedTuple):
  conv_w: jax.Array
  conv_b: jax.Array
  dtb_c: jax.Array
  alog_c: jax.Array
  dskip_e: jax.Array
  ssd_norm: jax.Array
  lb_logits: jax.Array
  hg_norm: jax.Array


def _dot(a, b):
  return jnp.dot(a, b, preferred_element_type=F32)


def _dot_nt(a, b):
  return lax.dot_general(a, b, (((1,), (1,)), ((), ())), preferred_element_type=F32)


def _dot_tn(a, b):
  return lax.dot_general(a, b, (((0,), (0,)), ((), ())), preferred_element_type=F32)


NEG_LOG2_E = -1.4426950408889634


def _sigmoid(x):
  return 1.0 / (1.0 + jnp.exp2(x * NEG_LOG2_E))


def _silu(x):
  return x * _sigmoid(x)


def _softplus(x):
  return jnp.maximum(x, 0.0) + jnp.log1p(jnp.exp(-jnp.abs(x)))


def _rmsnorm(x, w):
  ms = jnp.mean(x * x, axis=-1, keepdims=True)
  return x * lax.rsqrt(ms + EPS) * w


def _cumsum_rows(x):
  c, n = x.shape
  g = c // SUBLANES
  x3 = x.reshape(g, SUBLANES, n)
  sub = lax.broadcasted_iota(jnp.int32, (1, SUBLANES, n), 1)
  s = 1
  while s < SUBLANES:
    x3 = x3 + jnp.where(sub >= s, pltpu.roll(x3, s, axis=1), 0.0)
    s *= 2
  if g > 1:
    tot = jnp.broadcast_to(x3[:, SUBLANES - 1:SUBLANES, :], (g, SUBLANES, n))
    offs = [jnp.zeros((1, SUBLANES, n), x.dtype)]
    for k in range(1, g):
      offs.append(offs[-1] + tot[k - 1:k])
    x3 = x3 + jnp.concatenate(offs, axis=0)
  return x3.reshape(c, n)


def _expand_heads(xc):
  c = xc.shape[0]
  first_head = lax.broadcasted_iota(jnp.int32, (1, LANES), 1) < SSD_HEADDIM
  parts = []
  for j in range(SSD_HEADS // 2):
    a = jnp.broadcast_to(xc[:, 2 * j:2 * j + 1], (c, LANES))
    b = jnp.broadcast_to(xc[:, 2 * j + 1:2 * j + 2], (c, LANES))
    parts.append(jnp.where(first_head, a, b))
  return jnp.concatenate(parts, axis=1)


def _col_matrix(row_vec, n_lanes):
  n = row_vec.shape[1]
  return jnp.transpose(jnp.broadcast_to(row_vec, (n_lanes, n)))


def _mix_chunk(proj_ref, *, c, valid, p, xb_ref, ssm_ref, hg_ref, ssm_out_ref=None, hg_out_ref=None):
  cs = max(c, LANES)

  def padr(x):
    if x.shape[0] == cs:
      return x
    return jnp.concatenate([x, jnp.zeros((cs - x.shape[0], x.shape[1]), x.dtype)], axis=0)

  row_valid = None
  if valid < c:
    row_valid = lax.broadcasted_iota(jnp.int32, (c, 1), 0) < valid

  def mask_rows(x):
    return x if row_valid is None else jnp.where(row_valid, x, 0.0)

  cols = lambda off, width: proj_ref[:, off:off + width]

  xbc = cols(OFF_XBC, CONV_DIM)
  xb_ref[SUBLANES:SUBLANES + c, :] = xbc
  conv = p.conv_b + p.conv_w[CONV_W - 1:CONV_W] * xbc
  for j in range(CONV_W - 1):
    conv = conv + p.conv_w[j:j + 1] * xb_ref[SUBLANES - (CONV_W - 1) + j:SUBLANES - (CONV_W - 1) + j + c, :]
  yield 2
  conv = _silu(conv)
  yield 2
  xs = conv[:, :SSD_DINNER]
  bm = conv[:, SSD_DINNER:SSD_DINNER + SSD_GROUPS * SSD_STATE]
  cm = conv[:, SSD_DINNER + SSD_GROUPS * SSD_STATE:]

  head_lane = lax.broadcasted_iota(jnp.int32, (1, LANES), 1) < SSD_HEADS
  dt_c = mask_rows(jnp.where(head_lane, _softplus(cols(OFF_DT, LANES) + p.dtb_c), 0.0))
  acs_c = _cumsum_rows(dt_c * -jnp.exp(p.alog_c))
  last_c = acs_c[c - 1:c, :]
  dec_in = _expand_heads(jnp.exp(acs_c))
  xd_end = xs * _expand_heads(dt_c * jnp.exp(last_c - acs_c))
  yield 2
  acs_t = jnp.transpose(padr(acs_c))
  dt_t = jnp.transpose(padr(dt_c))
  chunk_decay_t = jnp.transpose(jnp.broadcast_to(jnp.exp(last_c), (LANES, LANES)))

  t_idx = lax.broadcasted_iota(jnp.int32, (c, cs), 0)
  s_idx = lax.broadcasted_iota(jnp.int32, (c, cs), 1)
  causal = s_idx <= t_idx
  causal_bias = jnp.where(causal, 0.0, -jnp.inf)
  lane = lax.broadcasted_iota(jnp.int32, (1, LANES), 1)
  first_head = lane < SSD_HEADDIM

  group_heads = [range(g * SSD_HEADS_PER_GROUP, (g + 1) * SSD_HEADS_PER_GROUP) for g in range(SSD_GROUPS)]
  bg_p = [padr(bm[:, g * SSD_STATE:(g + 1) * SSD_STATE]).astype(BF16) for g in range(SSD_GROUPS)]
  cg = [cm[:, g * SSD_STATE:(g + 1) * SSD_STATE].astype(BF16) for g in range(SSD_GROUPS)]
  h0 = [ssm_ref[heads.start:heads.stop].reshape(SSD_GROUP_WIDTH, SSD_STATE) for heads in group_heads]
  cb = [_dot_nt(cg[g], bg_p[g]) for g in range(SSD_GROUPS)]
  y_inter = [_dot_nt(cg[g], h0[g].astype(BF16)) for g in range(SSD_GROUPS)]
  xs_bf = padr(xs).astype(BF16)
  zero_bf = jnp.zeros((cs, LANES), BF16)
  y_intra = []
  for g in range(SSD_GROUPS):
    for j in range(SSD_HEADS_PER_GROUP // 2):
      h_first = g * SSD_HEADS_PER_GROUP + 2 * j
      scores = []
      for h in (h_first, h_first + 1):
        diff = acs_c[:, h:h + 1] - acs_t[h:h + 1, :]
        decay_dt = jnp.exp(diff + causal_bias) * dt_t[h:h + 1, :]
        scores.append((cb[g] * decay_dt).astype(BF16))
      pair = xs_bf[:, h_first * SSD_HEADDIM:(h_first + 2) * SSD_HEADDIM]
      rhs = jnp.concatenate([jnp.where(first_head, pair, zero_bf), jnp.where(first_head, zero_bf, pair)], axis=0)
      y_intra.append(_dot(jnp.concatenate(scores, axis=1), rhs))
      yield 1
  for g, heads in enumerate(group_heads):
    xd_g = padr(xd_end[:, g * SSD_GROUP_WIDTH:(g + 1) * SSD_GROUP_WIDTH]).astype(BF16)
    decay_cols = jnp.concatenate(
        [jnp.broadcast_to(chunk_decay_t[h:h + 1, :], (SSD_HEADDIM, SSD_STATE)) for h in heads], axis=0)
    h1 = h0[g] * decay_cols + _dot_tn(xd_g, bg_p[g])
    ssm_dst = ssm_ref if ssm_out_ref is None else ssm_out_ref
    ssm_dst[heads.start:heads.stop] = h1.reshape(SSD_HEADS_PER_GROUP, SSD_HEADDIM, SSD_STATE)
  y = jnp.concatenate(y_intra, axis=1) + jnp.concatenate(y_inter, axis=1) * dec_in + p.dskip_e * xs
  yield 1
  yg = y * _silu(cols(OFF_Z, SSD_DINNER))
  yield 1
  y_norm = []
  for g in range(SSD_GROUPS):
    blk = yg[:, g * SSD_GROUP_WIDTH:(g + 1) * SSD_GROUP_WIDTH]
    y_norm.append(blk * lax.rsqrt(jnp.mean(blk * blk, axis=-1, keepdims=True) + EPS))
  y_ssd = jnp.concatenate(y_norm, axis=1) * p.ssd_norm

  lg = p.lb_logits
  lg_max = jnp.max(lg, axis=0, keepdims=True)
  lg_exp = jnp.exp(lg - lg_max)
  lb = lg_exp[0:1] / jnp.sum(lg_exp, axis=0, keepdims=True)
  f = lb + (1.0 - lb) * _sigmoid(cols(OFF_FR, HG_WIDTH))
  yield 1
  logf = mask_rows(jnp.log(f))
  kk = mask_rows(1.0 - f)
  qq = _silu(cols(OFF_Q, HG_WIDTH))
  yield 1
  b = _cumsum_rows(logf)
  yield 2
  b_last = b[c - 1:c, :]
  q_inter = (qq * jnp.exp(b)).astype(BF16)
  k_end = kk * jnp.exp(b_last - b)
  state_decay = jnp.exp(b_last)
  yield 1

  levels = []
  m = c // 2
  while m >= HG_BLOCK:
    q_rows, k_rows = [], []
    zero_half = jnp.zeros((m, HG_WIDTH), BF16)
    for i in range(c // (2 * m)):
      lo = i * 2 * m
      b_ref_row = b[lo + m - 1:lo + m, :]
      k_rows += [(kk[lo:lo + m] * jnp.exp(b_ref_row - b[lo:lo + m])).astype(BF16), zero_half]
      q_rows += [zero_half, (qq[lo + m:lo + 2 * m] * jnp.exp(b[lo + m:lo + 2 * m] - b_ref_row)).astype(BF16)]
    shift = m.bit_length() - 1
    mask = ((t_idx >> (shift + 1)) == (s_idx >> (shift + 1))) & (((t_idx >> shift) & 1) == 1) & (
        ((s_idx >> shift) & 1) == 0)
    levels.append((jnp.concatenate(q_rows, axis=0), jnp.concatenate(k_rows, axis=0), mask))
    yield 1
    m //= 2
  blk = min(HG_BLOCK, c)
  q_rows, k_rows = [], []
  for i in range(c // blk):
    lo = i * blk
    b_loc = b[lo:lo + blk] if i == 0 else b[lo:lo + blk] - b[lo - 1:lo, :]
    q_rows.append(qq[lo:lo + blk] * jnp.exp(b_loc))
    k_rows.append(kk[lo:lo + blk] * jnp.exp(-b_loc))
  shift = blk.bit_length() - 1
  levels.append((jnp.concatenate(q_rows, axis=0).astype(BF16), padr(jnp.concatenate(k_rows, axis=0)).astype(BF16),
                 ((t_idx >> shift) == (s_idx >> shift)) & causal))

  yield 2
  head_slices = [slice(h * HG_DK, (h + 1) * HG_DK) for h in range(HG_HEADS)]
  v_bf = padr(cols(OFF_IV, HG_WIDTH)).astype(BF16)
  k_end_bf = padr(k_end).astype(BF16)
  att_bf = []
  for sl in head_slices:
    att = None
    for li, (ql, kl, mk) in enumerate(levels):
      block_scores = _dot_nt(ql[:, sl], kl[:, sl])
      if li == 0 and len(levels) > 1:
        att = block_scores
      else:
        att = jnp.where(mk, block_scores, 0.0 if att is None else att)
    att_bf.append(att.astype(BF16))
    yield 1
  s0 = [hg_ref[h] for h in range(HG_HEADS)]
  o_heads = [_dot(att_bf[h], v_bf[:, sl]) + _dot(q_inter[:, sl], s0[h].astype(BF16))
             for h, sl in enumerate(head_slices)]
  for h, sl in enumerate(head_slices):
    hg_dst = hg_ref if hg_out_ref is None else hg_out_ref
    hg_dst[h] = s0[h] * _col_matrix(state_decay[:, sl], HG_DV) + _dot_tn(k_end_bf[:, sl], v_bf[:, sl])
  o_parts = [o_h * lax.rsqrt(jnp.mean(o_h * o_h, axis=-1, keepdims=True) + EPS) for o_h in o_heads]
  o = jnp.concatenate(o_parts, axis=1) * p.hg_norm * _silu(cols(OFF_OG, HG_WIDTH))
  return jnp.concatenate([y_ssd, o], axis=1)


def _load_mix_params(refs):
  return MixParams(*[r[...] for r in refs])


def _drive(gen, between=lambda n: None):
  while True:
    try:
      between(next(gen))
    except StopIteration as done:
      return done.value


def _drive_round_robin(gens):
  results = [None] * len(gens)
  active = list(range(len(gens)))
  while active:
    for idx in list(active):
      try:
        next(gens[idx])
      except StopIteration as done:
        results[idx] = done.value
        active.remove(idx)
  return results


def _stage_ffn_weights(step, wup_f32_ref, wdn_f32_ref, wup_ref, wdn_ref):
  wup_ref[step] = wup_f32_ref[...].astype(BF16)
  wdn_ref[step] = wdn_f32_ref[...].astype(BF16)


def _staged_up_cols(wup_ref, lo):
  return wup_ref[lo // UP_BLOCK, :, lo % UP_BLOCK:lo % UP_BLOCK + FF_CHUNK]


def _swiglu(h, wup_ref, wdn_ref):
  acc = jnp.zeros((h.shape[0], D_MODEL), F32)
  for ci in range(N_FF_CHUNKS):
    lo = ci * FF_CHUNK
    g = _dot(h, _staged_up_cols(wup_ref, lo))
    u = _dot(h, _staged_up_cols(wup_ref, D_FF + lo))
    acc = acc + _dot((_silu(g) * u).astype(BF16), wdn_ref[ci])
  return acc


def _first_tile_select(first_ref, rest_ref, first_step):
  return jnp.where(pl.program_id(0) == first_step, first_ref[...], rest_ref[...])


def _ffn1_kernel(xs_ref, xp_ref, nw_ref, wup_f32_ref, wdn_f32_ref, os_ref, op_ref, wup_ref, wdn_ref):
  i = pl.program_id(0)

  @pl.when(i < W_STEPS)
  def _():
    _stage_ffn_weights(i, wup_f32_ref, wdn_f32_ref, wup_ref, wdn_ref)

  @pl.when(i >= W_STEPS)
  def _():
    x = _first_tile_select(xs_ref, xp_ref, W_STEPS)
    h = _rmsnorm(x, nw_ref[...]).astype(BF16)
    op_ref[...] = x + 0.5 * _swiglu(h, wup_ref, wdn_ref)

  @pl.when(i == W_STEPS)
  def _():
    os_ref[...] = op_ref[...]


def _split_w_in_kernel(wt_ref, wall_ref):
  o_dt = SSD_DINNER + CONV_DIM
  o_q = o_dt + SSD_HEADS
  wall_ref[:, OFF_Z:OFF_Q] = jnp.transpose(wt_ref[0:o_dt, :]).astype(BF16)
  wall_ref[:, OFF_Q:OFF_DT] = jnp.transpose(wt_ref[o_q:, :]).astype(BF16)
  dt_rows = jnp.concatenate([wt_ref[o_dt:o_q, :], jnp.zeros((LANES - SSD_HEADS, LANES), F32)], axis=0)
  wall_ref[:, OFF_DT:PROJ_W] = jnp.transpose(dt_rows).astype(BF16)


def _proj_blocks():
  return [(lo, min(lo + PROJ_BLOCK, PROJ_W)) for lo in range(0, PROJ_W, PROJ_BLOCK)]


def _mix_prompt_kernel(x_ref, nmix_ref, wall_ref, sproj_ref, sconv_in_ref, sssm_in_ref, shg_in_ref, *rest,
                       chunks_per_seq, valid):
  prm_refs = rest[:len(MixParams._fields)]
  (mixed_ref, conv_ref, ssm_ref, hg_ref, smixed_ref, sconv_ref, sssm_ref, shg_ref, xb_ref, proj_a, proj_b, sxb_ref,
   spad_ref) = rest[len(MixParams._fields):]
  i = pl.program_id(0)
  t_cur = lax.rem(jnp.maximum(i - 1, 0), chunks_per_seq)

  @pl.when(i == 0)
  def _():
    proj_b[...] = jnp.zeros(proj_b.shape, F32)

  @pl.when(t_cur == 0)
  def _():
    xb_ref[0:SUBLANES, :] = jnp.zeros((SUBLANES, CONV_DIM), F32)
    ssm_ref[...] = jnp.zeros(ssm_ref.shape, F32)
    hg_ref[...] = jnp.zeros(hg_ref.shape, F32)

  def step(cur_ref, next_ref):
    h = _rmsnorm(x_ref[0], nmix_ref[...]).astype(BF16)
    pending = _proj_blocks()
    n_blocks = len(pending)
    credit = [0.0]

    def project_blocks(n):
      credit[0] += n * n_blocks / MIX_PHASE_WEIGHT
      while pending and credit[0] >= 1.0:
        credit[0] -= 1.0
        lo, hi = pending.pop(0)
        next_ref[:, lo:hi] = _dot(h, wall_ref[:, lo:hi])

    prm = _load_mix_params(prm_refs)
    sxb_ref[0:SUBLANES, :] = jnp.zeros((SUBLANES, CONV_DIM), F32)
    sxb_ref[SUBLANES - (CONV_W - 1):SUBLANES, :] = sconv_in_ref[0]
    spad_ref[0:valid, :] = sproj_ref[0]
    spad_ref[valid:SAMPLE_ROWS, :] = jnp.zeros((SAMPLE_ROWS - valid, PROJ_W), F32)
    sgen = _mix_chunk(spad_ref, c=SAMPLE_ROWS, valid=valid, p=prm, xb_ref=sxb_ref, ssm_ref=sssm_in_ref.at[0],
                      hg_ref=shg_in_ref.at[0], ssm_out_ref=sssm_ref.at[0], hg_out_ref=shg_ref.at[0])
    smixed = []

    def advance_sample():
      if not smixed:
        try:
          next(sgen)
        except StopIteration as done:
          smixed.append(done.value)

    def between(n):
      project_blocks(n)
      advance_sample()

    mixed = _drive(_mix_chunk(cur_ref, c=CHUNK, valid=CHUNK, p=prm, xb_ref=xb_ref,
                              ssm_ref=ssm_ref.at[0], hg_ref=hg_ref.at[0]), between)
    project_blocks(MIX_PHASE_WEIGHT)
    while not smixed:
      advance_sample()
    smixed_ref[0] = smixed[0].astype(BF16)
    sconv_ref[0] = sxb_ref[SUBLANES + valid - (CONV_W - 1):SUBLANES + valid, :]
    mixed_ref[0] = mixed.astype(BF16)
    tail = xb_ref[CHUNK:CHUNK + SUBLANES, :]
    xb_ref[0:SUBLANES, :] = tail

  parity = lax.rem(i, 2)
  pl.when(parity == 0)(functools.partial(step, proj_b, proj_a))
  pl.when(parity == 1)(functools.partial(step, proj_a, proj_b))

  @pl.when((t_cur == chunks_per_seq - 1) & (i > 0))
  def _():
    conv_ref[0] = xb_ref[SUBLANES + CHUNK - (CONV_W - 1):SUBLANES + CHUNK, :]


def _proj_sample_kernel(x_ref, nmix_ref, wall_ref, o_ref):
  h = _rmsnorm(x_ref[...], nmix_ref[...]).astype(BF16)
  for lo, hi in _proj_blocks():
    o_ref[:, lo:hi] = _dot(h, wall_ref[:, lo:hi])


def _rec_sample_kernel(proj_ref, conv_in_ref, ssm_in_ref, hg_in_ref, *rest, valid):
  prm_refs = rest[:len(MixParams._fields)]
  mixed_ref, conv_ref, ssm_ref, hg_ref, xb_ref, padded_ref = rest[len(MixParams._fields):]
  n_seq = conv_in_ref.shape[0]
  prm = _load_mix_params(prm_refs)
  ssm_ref[...] = ssm_in_ref[...]
  hg_ref[...] = hg_in_ref[...]
  gens = []
  for s in range(n_seq):
    xb_ref[s, 0:SUBLANES, :] = jnp.zeros((SUBLANES, CONV_DIM), F32)
    xb_ref[s, SUBLANES - (CONV_W - 1):SUBLANES, :] = conv_in_ref[s]
    padded_ref[s, 0:valid, :] = proj_ref[s * valid:(s + 1) * valid, :]
    padded_ref[s, valid:SAMPLE_ROWS, :] = jnp.zeros((SAMPLE_ROWS - valid, PROJ_W), F32)
    gens.append(_mix_chunk(padded_ref.at[s], c=SAMPLE_ROWS, valid=valid, p=prm, xb_ref=xb_ref.at[s],
                           ssm_ref=ssm_ref.at[s], hg_ref=hg_ref.at[s]))
  for s, mixed in enumerate(_drive_round_robin(gens)):
    mixed_ref[s] = mixed.astype(BF16)
    conv_ref[s] = xb_ref[s, SUBLANES + valid - (CONV_W - 1):SUBLANES + valid, :]


def _tail_kernel(x1s_ref, x1p_ref, mixs_ref, mixp_ref, ps_ref, pp_ref, wout_ref, nf2_ref, wup_f32_ref, wdn_f32_ref,
                 nple_ref, wgate_ref, wproj_ref, ppost_ref, nfin_ref, os_ref, op_ref, wup_ref, wdn_ref):
  i = pl.program_id(0)

  @pl.when(i < W_STEPS)
  def _():
    _stage_ffn_weights(i, wup_f32_ref, wdn_f32_ref, wup_ref, wdn_ref)

  @pl.when(i >= W_STEPS)
  def _():
    pick = lambda s_ref, p_ref: _first_tile_select(s_ref, p_ref, W_STEPS)
    x2 = pick(x1s_ref, x1p_ref) + _dot(pick(mixs_ref, mixp_ref), wout_ref[...])
    h = _rmsnorm(x2, nf2_ref[...]).astype(BF16)
    e = _rmsnorm(_dot(pick(ps_ref, pp_ref).astype(BF16), wproj_ref[...]), ppost_ref[...])
    x3 = x2 + 0.5 * _swiglu(h, wup_ref, wdn_ref)
    gate = _sigmoid(_dot(_rmsnorm(x3, nple_ref[...]).astype(BF16), wgate_ref[...]))
    x4 = x3 + gate * e
    op_ref[...] = _rmsnorm(x4, nfin_ref[...])

  @pl.when(i == W_STEPS)
  def _():
    os_ref[...] = op_ref[...]


def _resident(shape):
  nd = len(shape)
  return pl.BlockSpec(shape, lambda *_: (0,) * nd, pipeline_mode=pl.Buffered(1))


def _params(semantics):
  return pltpu.CompilerParams(dimension_semantics=semantics, vmem_limit_bytes=VMEM_LIMIT_BYTES)


def _sample_then_prompt_specs(width, n_prompt_tiles, first_step):
  sample = pl.BlockSpec((FFN_TILE, width), lambda i: (0, 0))
  prompt = pl.BlockSpec((FFN_TILE, width), lambda i: (jnp.clip(i - first_step - 1, 0, n_prompt_tiles - 1), 0))
  return sample, prompt


def _ffn_weight_specs():
  staged = lambda i: jnp.minimum(i, W_STEPS - 1)
  return [pl.BlockSpec((D_MODEL, UP_BLOCK), lambda i: (0, staged(i))),
          pl.BlockSpec((FF_CHUNK, D_MODEL), lambda i: (staged(i), 0))]


FFN_WEIGHT_SCRATCH = [pltpu.VMEM((W_STEPS, D_MODEL, UP_BLOCK), BF16), pltpu.VMEM((W_STEPS, FF_CHUNK, D_MODEL), BF16)]


def _ffn1(xs, xp, nw, wup, wdn):
  assert xs.shape[0] == FFN_TILE and xp.shape[0] % FFN_TILE == 0
  n_prompt_tiles = xp.shape[0] // FFN_TILE
  specs = _sample_then_prompt_specs(D_MODEL, n_prompt_tiles, first_step=W_STEPS)
  return pl.pallas_call(
      _ffn1_kernel,
      grid=(W_STEPS + 1 + n_prompt_tiles,),
      in_specs=[*specs, _resident(nw.shape), *_ffn_weight_specs()],
      out_specs=specs,
      out_shape=(jax.ShapeDtypeStruct(xs.shape, F32), jax.ShapeDtypeStruct(xp.shape, F32)),
      scratch_shapes=FFN_WEIGHT_SCRATCH,
      compiler_params=_params(("arbitrary",)),
      name="ffn1",
  )(xs, xp, nw, wup, wdn)


def _split_w_in(wt):
  rows = LANES
  assert wt.shape == (SSD_DINNER + CONV_DIM + SSD_HEADS + 4 * HG_WIDTH, D_MODEL)
  return pl.pallas_call(
      _split_w_in_kernel,
      grid=(D_MODEL // rows,),
      in_specs=[pl.BlockSpec((wt.shape[0], rows), lambda i: (0, i))],
      out_specs=pl.BlockSpec((rows, PROJ_W), lambda i: (i, 0)),
      out_shape=jax.ShapeDtypeStruct((D_MODEL, PROJ_W), BF16),
      compiler_params=_params(("arbitrary",)),
      name="split_w_in",
  )(wt)


def _mix_prompt(x1, nmix, wall, prm, sproj, conv0, ssm0, hg0):
  bsz, seq, _ = x1.shape
  nt = seq // CHUNK
  n_chunks = bsz * nt
  n_s, valid, _ = sproj.shape
  assert n_s <= n_chunks + 1
  out_shape = (
      jax.ShapeDtypeStruct((bsz, seq, D_MIX), BF16),
      jax.ShapeDtypeStruct((bsz, CONV_W - 1, CONV_DIM), F32),
      jax.ShapeDtypeStruct((bsz, SSD_HEADS, SSD_HEADDIM, SSD_STATE), F32),
      jax.ShapeDtypeStruct((bsz, HG_HEADS, HG_DK, HG_DV), F32),
      jax.ShapeDtypeStruct((n_s, SAMPLE_ROWS, D_MIX), BF16),
      jax.ShapeDtypeStruct((n_s, CONV_W - 1, CONV_DIM), F32),
      jax.ShapeDtypeStruct((n_s, SSD_HEADS, SSD_HEADDIM, SSD_STATE), F32),
      jax.ShapeDtypeStruct((n_s, HG_HEADS, HG_DK, HG_DV), F32),
  )
  s_seq = lambda i: jnp.minimum(i, n_s - 1)
  s_state_specs = [
      pl.BlockSpec((1, CONV_W - 1, CONV_DIM), lambda i: (s_seq(i), 0, 0)),
      pl.BlockSpec((1, SSD_HEADS, SSD_HEADDIM, SSD_STATE), lambda i: (s_seq(i), 0, 0, 0)),
      pl.BlockSpec((1, HG_HEADS, HG_DK, HG_DV), lambda i: (s_seq(i), 0, 0, 0)),
  ]
  proj_chunk = lambda i: jnp.minimum(i, n_chunks - 1)
  mix_chunk = lambda i: jnp.maximum(i - 1, 0)
  out_specs = (
      pl.BlockSpec((1, CHUNK, D_MIX), lambda i: (mix_chunk(i) // nt, mix_chunk(i) % nt, 0)),
      pl.BlockSpec((1, CONV_W - 1, CONV_DIM), lambda i: (mix_chunk(i) // nt, 0, 0)),
      pl.BlockSpec((1, SSD_HEADS, SSD_HEADDIM, SSD_STATE), lambda i: (mix_chunk(i) // nt, 0, 0, 0)),
      pl.BlockSpec((1, HG_HEADS, HG_DK, HG_DV), lambda i: (mix_chunk(i) // nt, 0, 0, 0)),
      pl.BlockSpec((1, SAMPLE_ROWS, D_MIX), lambda i: (s_seq(i), 0, 0)),
      *s_state_specs,
  )
  in_specs = ([pl.BlockSpec((1, CHUNK, D_MODEL), lambda i: (proj_chunk(i) // nt, proj_chunk(i) % nt, 0)),
               _resident(nmix.shape), _resident(wall.shape),
               pl.BlockSpec((1, valid, PROJ_W), lambda i: (s_seq(i), 0, 0))] + s_state_specs
              + [_resident(a.shape) for a in prm])
  return pl.pallas_call(
      functools.partial(_mix_prompt_kernel, chunks_per_seq=nt, valid=valid),
      grid=(n_chunks + 1,),
      in_specs=in_specs,
      out_specs=out_specs,
      out_shape=out_shape,
      scratch_shapes=[pltpu.VMEM((SUBLANES + CHUNK, CONV_DIM), F32), pltpu.VMEM((CHUNK, PROJ_W), F32),
                      pltpu.VMEM((CHUNK, PROJ_W), F32), pltpu.VMEM((2 * SUBLANES, CONV_DIM), F32),
                      pltpu.VMEM((SAMPLE_ROWS, PROJ_W), F32)],
      compiler_params=_params(("arbitrary",)),
      name="mix_prompt",
  )(x1, nmix, wall, sproj, conv0, ssm0, hg0, *prm)


def _proj_sample(x1p, nmix, wall):
  n = x1p.shape[0]
  tile = 256
  return pl.pallas_call(
      _proj_sample_kernel,
      grid=(n // tile,),
      in_specs=[pl.BlockSpec((tile, D_MODEL), lambda i: (i, 0)), _resident(nmix.shape), _resident(wall.shape)],
      out_specs=pl.BlockSpec((tile, PROJ_W), lambda i: (i, 0)),
      out_shape=jax.ShapeDtypeStruct((n, PROJ_W), F32),
      compiler_params=_params(("arbitrary",)),
      name="proj_sample",
  )(x1p, nmix, wall)


def _rec_sample(proj, conv0, ssm0, hg0, prm, valid):
  bsz = conv0.shape[0]
  assert proj.shape[0] == bsz * valid
  out_shape = (
      jax.ShapeDtypeStruct((bsz, SAMPLE_ROWS, D_MIX), BF16),
      jax.ShapeDtypeStruct((bsz, CONV_W - 1, CONV_DIM), F32),
      jax.ShapeDtypeStruct((bsz, SSD_HEADS, SSD_HEADDIM, SSD_STATE), F32),
      jax.ShapeDtypeStruct((bsz, HG_HEADS, HG_DK, HG_DV), F32),
  )
  g = SAMPLE_SEQS_PER_STEP
  state_specs = [
      pl.BlockSpec((g, CONV_W - 1, CONV_DIM), lambda b: (b, 0, 0)),
      pl.BlockSpec((g, SSD_HEADS, SSD_HEADDIM, SSD_STATE), lambda b: (b, 0, 0, 0)),
      pl.BlockSpec((g, HG_HEADS, HG_DK, HG_DV), lambda b: (b, 0, 0, 0)),
  ]
  in_specs = [pl.BlockSpec((g * valid, PROJ_W), lambda b: (b, 0))] + state_specs + [_resident(a.shape) for a in prm]
  out_specs = tuple([pl.BlockSpec((g, SAMPLE_ROWS, D_MIX), lambda b: (b, 0, 0))] + state_specs)
  return pl.pallas_call(
      functools.partial(_rec_sample_kernel, valid=valid),
      grid=(bsz // g,),
      in_specs=in_specs,
      out_specs=out_specs,
      out_shape=out_shape,
      scratch_shapes=[pltpu.VMEM((g, 2 * SUBLANES, CONV_DIM), F32), pltpu.VMEM((g, SAMPLE_ROWS, PROJ_W), F32)],
      compiler_params=_params(("arbitrary",)),
      name="rec_sample",
  )(proj, conv0, ssm0, hg0, *prm)


def _tail(x1s, x1p, mixs, mixp, ps, pp, wout, nf2, wup, wdn, nple, wgate, wproj, ppost, nfin):
  assert x1s.shape[0] == FFN_TILE and x1p.shape[0] % FFN_TILE == 0
  weights = (wout, nf2, wup, wdn, nple, wgate, wproj, ppost, nfin)
  n_prompt_tiles = x1p.shape[0] // FFN_TILE

  def in_specs(width):
    sample, prompt = _sample_then_prompt_specs(width, n_prompt_tiles, first_step=W_STEPS)
    return pl.BlockSpec(sample.block_shape, sample.index_map, pipeline_mode=pl.Buffered(1)), prompt

  weight_specs = [_resident(wout.shape), _resident(nf2.shape), *_ffn_weight_specs()] + [
      _resident(w.shape) for w in (nple, wgate, wproj, ppost, nfin)]
  return pl.pallas_call(
      _tail_kernel,
      grid=(W_STEPS + 1 + n_prompt_tiles,),
      in_specs=[*in_specs(D_MODEL), *in_specs(D_MIX), *in_specs(PLE_DIM)] + weight_specs,
      out_specs=_sample_then_prompt_specs(D_MODEL, n_prompt_tiles, first_step=W_STEPS),
      out_shape=(jax.ShapeDtypeStruct(x1s.shape, F32), jax.ShapeDtypeStruct(x1p.shape, F32)),
      scratch_shapes=FFN_WEIGHT_SCRATCH,
      compiler_params=_params(("arbitrary",)),
      name="tail",
  )(x1s, x1p, mixs, mixp, ps, pp, *weights)


def kernel(x_prompt, x_sample, state_conv, state_ssm, state_hgrn, p_prompt, p_sample, norm_ffn1, w_ffn1_up, w_ffn1_down, norm_mix, w_in, conv_w, conv_b, dt_bias, a_log, d_skip, ssd_norm, hg_lb_logits, hg_norm, w_out, norm_ffn2, w_ffn2_up, w_ffn2_down, norm_ple, w_ple_gate, w_ple_proj, ple_post_norm, norm_final):
  bp, seq, _ = x_prompt.shape
  bs, dec_seq, _ = x_sample.shape
  row = lambda v: v.reshape(1, -1).astype(F32)
  per_head = lambda v: jnp.repeat(v.astype(F32), SSD_HEADDIM).reshape(1, SSD_DINNER)
  head_lanes = lambda v: jnp.pad(v.astype(F32), (0, LANES - SSD_HEADS)).reshape(1, LANES)

  w1u, w1d = w_ffn1_up[0].astype(F32), w_ffn1_down[0].astype(F32)
  w2u, w2d = w_ffn2_up[0].astype(F32), w_ffn2_down[0].astype(F32)
  wall = _split_w_in(jnp.swapaxes(w_in[0], 0, 1))
  prm = MixParams(
      conv_w=conv_w[0].astype(F32), conv_b=row(conv_b[0]), dtb_c=head_lanes(dt_bias[0]),
      alog_c=head_lanes(a_log[0]), dskip_e=per_head(d_skip[0]), ssd_norm=row(ssd_norm[0]),
      lb_logits=hg_lb_logits.astype(F32), hg_norm=row(hg_norm[0]))
  nmix = row(norm_mix[0])
  tail_w = (w_out[0].astype(BF16), row(norm_ffn2[0]), w2u, w2d, row(norm_ple[0]), w_ple_gate[0].astype(BF16),
            w_ple_proj[0].astype(BF16), row(ple_post_norm[0]), row(norm_final))

  xp = x_prompt.reshape(bp * seq, D_MODEL)
  xs = x_sample.reshape(bs * dec_seq, D_MODEL)
  x1s, x1p = _ffn1(xs, xp, row(norm_ffn1[0]), w1u, w1d)

  proj_s = _proj_sample(x1s, nmix, wall).reshape(bs, dec_seq, PROJ_W)
  mixed_p, conv_p, ssm_p, hg_p, mixed_s, conv_s, ssm_s, hg_s = _mix_prompt(
      x1p.reshape(bp, seq, D_MODEL), nmix, wall, prm, proj_s, state_conv[0], state_ssm[0], state_hgrn[0])
  mixed_s = mixed_s[:, :dec_seq].reshape(bs * dec_seq, D_MIX)

  y_sample, y_prompt = _tail(x1s, x1p, mixed_s, mixed_p.reshape(bp * seq, D_MIX),
                             p_sample[0].reshape(bs * dec_seq, PLE_DIM), p_prompt[0].reshape(bp * seq, PLE_DIM), *tail_w)

  return (y_prompt.reshape(bp, seq, D_MODEL), y_sample.reshape(bs, dec_seq, D_MODEL), conv_p[None], ssm_p[None],
          hg_p[None], conv_s[None], ssm_s[None], hg_s[None])
```

```python
import functools
from typing import NamedTuple

import jax
import jax.numpy as jnp
from jax import lax
from jax.experimental import pallas as pl
from jax.experimental.pallas import tpu as pltpu

F32 = jnp.float32
BF16 = jnp.bfloat16

D_MODEL = 1024
D_FF = 2816
PLE_DIM = 256
EPS = 1e-6
SSD_HEADS = 16
SSD_HEADDIM = 64
SSD_DINNER = SSD_HEADS * SSD_HEADDIM
SSD_STATE = 128
SSD_GROUPS = 2
SSD_HEADS_PER_GROUP = SSD_HEADS // SSD_GROUPS
SSD_GROUP_WIDTH = SSD_DINNER // SSD_GROUPS
CONV_W = 4
CONV_DIM = SSD_DINNER + 2 * SSD_GROUPS * SSD_STATE
HG_HEADS = 8
HG_DK = 128
HG_DV = 128
HG_WIDTH = HG_HEADS * HG_DV
HG_BLOCK = 16
D_MIX = SSD_DINNER + HG_WIDTH

LANES = 128
SUBLANES = 8
VMEM_LIMIT_BYTES = 56 * 1024 * 1024

CHUNK = 128
SAMPLE_ROWS = 8
SAMPLE_SEQS_PER_STEP = 8
FFN_TILE = 512
FF_CHUNK = 256
N_FF_CHUNKS = D_FF // FF_CHUNK
W_STEPS = N_FF_CHUNKS
UP_BLOCK = 2 * D_FF // W_STEPS

OFF_Z = 0
OFF_XBC = OFF_Z + SSD_DINNER
OFF_Q = OFF_XBC + CONV_DIM
OFF_FR = OFF_Q + HG_WIDTH
OFF_IV = OFF_FR + HG_WIDTH
OFF_OG = OFF_IV + HG_WIDTH
OFF_DT = OFF_OG + HG_WIDTH
PROJ_W = OFF_DT + LANES
PROJ_BLOCK = 256
MIX_PHASE_WEIGHT = 34


class MixParams(NamedTuple):
  conv_w: jax.Array
  conv_b: jax.Array
  dtb_c: jax.Array
  alog_c: jax.Array
  dskip_e: jax.Array
  ssd_norm: jax.Array
  lb_logits: jax.Array
  hg_norm: jax.Array


def _dot(a, b):
  return jnp.dot(a, b, preferred_element_type=F32)


def _dot_nt(a, b):
  return lax.dot_general(a, b, (((1,), (1,)), ((), ())), preferred_element_type=F32)


def _dot_tn(a, b):
  return lax.dot_general(a, b, (((0,), (0,)), ((), ())), preferred_element_type=F32)


NEG_LOG2_E = -1.4426950408889634


def _sigmoid(x):
  return pl.reciprocal(1.0 + jnp.exp2(x * NEG_LOG2_E), approx=True)


def _silu(x):
  return x * _sigmoid(x)


def _softplus(x):
  return jnp.maximum(x, 0.0) + jnp.log1p(jnp.exp(-jnp.abs(x)))


def _rmsnorm(x, w):
  ms = jnp.mean(x * x, axis=-1, keepdims=True)
  return x * lax.rsqrt(ms + EPS) * w


def _cumsum_rows(x):
  c, n = x.shape
  g = c // SUBLANES
  x3 = x.reshape(g, SUBLANES, n)
  sub = lax.broadcasted_iota(jnp.int32, (1, SUBLANES, n), 1)
  s = 1
  while s < SUBLANES:
    x3 = x3 + jnp.where(sub >= s, pltpu.roll(x3, s, axis=1), 0.0)
    s *= 2
  if g > 1:
    tot = jnp.broadcast_to(x3[:, SUBLANES - 1:SUBLANES, :], (g, SUBLANES, n))
    offs = [jnp.zeros((1, SUBLANES, n), x.dtype)]
    for k in range(1, g):
      offs.append(offs[-1] + tot[k - 1:k])
    x3 = x3 + jnp.concatenate(offs, axis=0)
  return x3.reshape(c, n)


def _expand_heads(xc):
  c = xc.shape[0]
  first_head = lax.broadcasted_iota(jnp.int32, (1, LANES), 1) < SSD_HEADDIM
  parts = []
  for j in range(SSD_HEADS // 2):
    a = jnp.broadcast_to(xc[:, 2 * j:2 * j + 1], (c, LANES))
    b = jnp.broadcast_to(xc[:, 2 * j + 1:2 * j + 2], (c, LANES))
    parts.append(jnp.where(first_head, a, b))
  return jnp.concatenate(parts, axis=1)


def _col_matrix(row_vec, n_lanes):
  n = row_vec.shape[1]
  return jnp.transpose(jnp.broadcast_to(row_vec, (n_lanes, n)))


def _mix_chunk(proj_ref, *, c, valid, p, xb_ref, ssm_ref, hg_ref):
  cs = max(c, LANES)

  def padr(x):
    if x.shape[0] == cs:
      return x
    return jnp.concatenate([x, jnp.zeros((cs - x.shape[0], x.shape[1]), x.dtype)], axis=0)

  row_valid = None
  if valid < c:
    row_valid = lax.broadcasted_iota(jnp.int32, (c, 1), 0) < valid

  def mask_rows(x):
    return x if row_valid is None else jnp.where(row_valid, x, 0.0)

  cols = lambda off, width: proj_ref[:, off:off + width]

  xbc = cols(OFF_XBC, CONV_DIM)
  xb_ref[SUBLANES:SUBLANES + c, :] = xbc
  conv = p.conv_b + p.conv_w[CONV_W - 1:CONV_W] * xbc
  for j in range(CONV_W - 1):
    conv = conv + p.conv_w[j:j + 1] * xb_ref[SUBLANES - (CONV_W - 1) + j:SUBLANES - (CONV_W - 1) + j + c, :]
  yield 2
  conv = _silu(conv)
  yield 2
  xs = conv[:, :SSD_DINNER]
  bm = conv[:, SSD_DINNER:SSD_DINNER + SSD_GROUPS * SSD_STATE]
  cm = conv[:, SSD_DINNER + SSD_GROUPS * SSD_STATE:]

  head_lane = lax.broadcasted_iota(jnp.int32, (1, LANES), 1) < SSD_HEADS
  dt_c = mask_rows(jnp.where(head_lane, _softplus(cols(OFF_DT, LANES) + p.dtb_c), 0.0))
  acs_c = _cumsum_rows(dt_c * -jnp.exp(p.alog_c))
  last_c = acs_c[c - 1:c, :]
  dec_in = _expand_heads(jnp.exp(acs_c))
  xd_end = xs * _expand_heads(dt_c * jnp.exp(last_c - acs_c))
  yield 2
  acs_t = jnp.transpose(padr(acs_c))
  dt_t = jnp.transpose(padr(dt_c))
  chunk_decay_t = jnp.transpose(jnp.broadcast_to(jnp.exp(last_c), (LANES, LANES)))

  t_idx = lax.broadcasted_iota(jnp.int32, (c, cs), 0)
  s_idx = lax.broadcasted_iota(jnp.int32, (c, cs), 1)
  causal = s_idx <= t_idx
  causal_bias = jnp.where(causal, 0.0, -jnp.inf)
  lane = lax.broadcasted_iota(jnp.int32, (1, LANES), 1)
  first_head = lane < SSD_HEADDIM

  group_heads = [range(g * SSD_HEADS_PER_GROUP, (g + 1) * SSD_HEADS_PER_GROUP) for g in range(SSD_GROUPS)]
  bg_p = [padr(bm[:, g * SSD_STATE:(g + 1) * SSD_STATE]).astype(BF16) for g in range(SSD_GROUPS)]
  cg = [cm[:, g * SSD_STATE:(g + 1) * SSD_STATE].astype(BF16) for g in range(SSD_GROUPS)]
  h0 = [ssm_ref[heads.start:heads.stop].reshape(SSD_GROUP_WIDTH, SSD_STATE) for heads in group_heads]
  cb = [_dot_nt(cg[g], bg_p[g]) for g in range(SSD_GROUPS)]
  y_inter = [_dot_nt(cg[g], h0[g].astype(BF16)) for g in range(SSD_GROUPS)]
  xs_bf = padr(xs).astype(BF16)
  zero_bf = jnp.zeros((cs, LANES), BF16)
  y_intra = []
  for g in range(SSD_GROUPS):
    for j in range(SSD_HEADS_PER_GROUP // 2):
      h_first = g * SSD_HEADS_PER_GROUP + 2 * j
      scores = []
      for h in (h_first, h_first + 1):
        diff = acs_c[:, h:h + 1] - acs_t[h:h + 1, :]
        decay_dt = jnp.exp(diff + causal_bias) * dt_t[h:h + 1, :]
        scores.append((cb[g] * decay_dt).astype(BF16))
      pair = xs_bf[:, h_first * SSD_HEADDIM:(h_first + 2) * SSD_HEADDIM]
      rhs = jnp.concatenate([jnp.where(first_head, pair, zero_bf), jnp.where(first_head, zero_bf, pair)], axis=0)
      y_intra.append(_dot(jnp.concatenate(scores, axis=1), rhs))
      yield 1
  for g, heads in enumerate(group_heads):
    xd_g = padr(xd_end[:, g * SSD_GROUP_WIDTH:(g + 1) * SSD_GROUP_WIDTH]).astype(BF16)
    decay_cols = jnp.concatenate(
        [jnp.broadcast_to(chunk_decay_t[h:h + 1, :], (SSD_HEADDIM, SSD_STATE)) for h in heads], axis=0)
    h1 = h0[g] * decay_cols + _dot_tn(xd_g, bg_p[g])
    ssm_ref[heads.start:heads.stop] = h1.reshape(SSD_HEADS_PER_GROUP, SSD_HEADDIM, SSD_STATE)
  y = jnp.concatenate(y_intra, axis=1) + jnp.concatenate(y_inter, axis=1) * dec_in + p.dskip_e * xs
  yield 1
  yg = y * _silu(cols(OFF_Z, SSD_DINNER))
  yield 1
  y_norm = []
  for g in range(SSD_GROUPS):
    blk = yg[:, g * SSD_GROUP_WIDTH:(g + 1) * SSD_GROUP_WIDTH]
    y_norm.append(blk * lax.rsqrt(jnp.mean(blk * blk, axis=-1, keepdims=True) + EPS))
  y_ssd = jnp.concatenate(y_norm, axis=1) * p.ssd_norm

  lg = p.lb_logits
  lg_max = jnp.max(lg, axis=0, keepdims=True)
  lg_exp = jnp.exp(lg - lg_max)
  lb = lg_exp[0:1] / jnp.sum(lg_exp, axis=0, keepdims=True)
  f = lb + (1.0 - lb) * _sigmoid(cols(OFF_FR, HG_WIDTH))
  yield 1
  logf = mask_rows(jnp.log(f))
  kk = mask_rows(1.0 - f)
  qq = _silu(cols(OFF_Q, HG_WIDTH))
  yield 1
  b = _cumsum_rows(logf)
  yield 2
  b_last = b[c - 1:c, :]
  q_inter = (qq * jnp.exp(b)).astype(BF16)
  k_end = kk * jnp.exp(b_last - b)
  state_decay = jnp.exp(b_last)
  yield 1

  levels = []
  m = c // 2
  while m >= HG_BLOCK:
    q_rows, k_rows = [], []
    zero_half = jnp.zeros((m, HG_WIDTH), BF16)
    for i in range(c // (2 * m)):
      lo = i * 2 * m
      b_ref_row = b[lo + m - 1:lo + m, :]
      k_rows += [(kk[lo:lo + m] * jnp.exp(b_ref_row - b[lo:lo + m])).astype(BF16), zero_half]
      q_rows += [zero_half, (qq[lo + m:lo + 2 * m] * jnp.exp(b[lo + m:lo + 2 * m] - b_ref_row)).astype(BF16)]
    shift = m.bit_length() - 1
    mask = ((t_idx >> (shift + 1)) == (s_idx >> (shift + 1))) & (((t_idx >> shift) & 1) == 1) & (
        ((s_idx >> shift) & 1) == 0)
    levels.append((jnp.concatenate(q_rows, axis=0), jnp.concatenate(k_rows, axis=0), mask))
    yield 1
    m //= 2
  blk = min(HG_BLOCK, c)
  q_rows, k_rows = [], []
  for i in range(c // blk):
    lo = i * blk
    b_loc = b[lo:lo + blk] if i == 0 else b[lo:lo + blk] - b[lo - 1:lo, :]
    q_rows.append(qq[lo:lo + blk] * jnp.exp(b_loc))
    k_rows.append(kk[lo:lo + blk] * jnp.exp(-b_loc))
  shift = blk.bit_length() - 1
  levels.append((jnp.concatenate(q_rows, axis=0).astype(BF16), padr(jnp.concatenate(k_rows, axis=0)).astype(BF16),
                 ((t_idx >> shift) == (s_idx >> shift)) & causal))

  yield 2
  head_slices = [slice(h * HG_DK, (h + 1) * HG_DK) for h in range(HG_HEADS)]
  v_bf = padr(cols(OFF_IV, HG_WIDTH)).astype(BF16)
  k_end_bf = padr(k_end).astype(BF16)
  att_bf = []
  for sl in head_slices:
    att = None
    for li, (ql, kl, mk) in enumerate(levels):
      block_scores = _dot_nt(ql[:, sl], kl[:, sl])
      if li == 0 and len(levels) > 1:
        att = block_scores
      else:
        att = jnp.where(mk, block_scores, 0.0 if att is None else att)
    att_bf.append(att.astype(BF16))
    yield 1
  s0 = [hg_ref[h] for h in range(HG_HEADS)]
  o_heads = [_dot(att_bf[h], v_bf[:, sl]) + _dot(q_inter[:, sl], s0[h].astype(BF16))
             for h, sl in enumerate(head_slices)]
  for h, sl in enumerate(head_slices):
    hg_ref[h] = s0[h] * _col_matrix(state_decay[:, sl], HG_DV) + _dot_tn(k_end_bf[:, sl], v_bf[:, sl])
  o_parts = [o_h * lax.rsqrt(jnp.mean(o_h * o_h, axis=-1, keepdims=True) + EPS) for o_h in o_heads]
  o = jnp.concatenate(o_parts, axis=1) * p.hg_norm * _silu(cols(OFF_OG, HG_WIDTH))
  return jnp.concatenate([y_ssd, o], axis=1)


def _load_mix_params(refs):
  return MixParams(*[r[...] for r in refs])


def _drive(gen, between=lambda n: None):
  while True:
    try:
      between(next(gen))
    except StopIteration as done:
      return done.value


def _drive_round_robin(gens):
  results = [None] * len(gens)
  active = list(range(len(gens)))
  while active:
    for idx in list(active):
      try:
        next(gens[idx])
      except StopIteration as done:
        results[idx] = done.value
        active.remove(idx)
  return results


def _stage_ffn_weights(step, wup_f32_ref, wdn_f32_ref, wup_ref, wdn_ref):
  wup_ref[step] = wup_f32_ref[...].astype(BF16)
  wdn_ref[step] = wdn_f32_ref[...].astype(BF16)


def _staged_up_cols(wup_ref, lo):
  return wup_ref[lo // UP_BLOCK, :, lo % UP_BLOCK:lo % UP_BLOCK + FF_CHUNK]


def _swiglu(h, wup_ref, wdn_ref):
  acc = jnp.zeros((h.shape[0], D_MODEL), F32)
  for ci in range(N_FF_CHUNKS):
    lo = ci * FF_CHUNK
    g = _dot(h, _staged_up_cols(wup_ref, lo))
    u = _dot(h, _staged_up_cols(wup_ref, D_FF + lo))
    acc = acc + _dot((_silu(g) * u).astype(BF16), wdn_ref[ci])
  return acc


def _first_tile_select(first_ref, rest_ref, first_step):
  return jnp.where(pl.program_id(0) == first_step, first_ref[...], rest_ref[...])


def _ffn1_kernel(xs_ref, xp_ref, nw_ref, wup_f32_ref, wdn_f32_ref, os_ref, op_ref, wup_ref, wdn_ref):
  i = pl.program_id(0)

  @pl.when(i < W_STEPS)
  def _():
    _stage_ffn_weights(i, wup_f32_ref, wdn_f32_ref, wup_ref, wdn_ref)

  @pl.when(i >= W_STEPS)
  def _():
    x = _first_tile_select(xs_ref, xp_ref, W_STEPS)
    h = _rmsnorm(x, nw_ref[...]).astype(BF16)
    op_ref[...] = x + 0.5 * _swiglu(h, wup_ref, wdn_ref)

  @pl.when(i == W_STEPS)
  def _():
    os_ref[...] = op_ref[...]


def _split_w_in_kernel(wt_ref, wall_ref):
  o_dt = SSD_DINNER + CONV_DIM
  o_q = o_dt + SSD_HEADS
  wall_ref[:, OFF_Z:OFF_Q] = jnp.transpose(wt_ref[0:o_dt, :]).astype(BF16)
  wall_ref[:, OFF_Q:OFF_DT] = jnp.transpose(wt_ref[o_q:, :]).astype(BF16)
  dt_rows = jnp.concatenate([wt_ref[o_dt:o_q, :], jnp.zeros((LANES - SSD_HEADS, LANES), F32)], axis=0)
  wall_ref[:, OFF_DT:PROJ_W] = jnp.transpose(dt_rows).astype(BF16)


def _proj_blocks():
  return [(lo, min(lo + PROJ_BLOCK, PROJ_W)) for lo in range(0, PROJ_W, PROJ_BLOCK)]


def _mix_prompt_kernel(x_ref, nmix_ref, wall_ref, *rest, chunks_per_seq):
  prm_refs = rest[:len(MixParams._fields)]
  mixed_ref, conv_ref, ssm_ref, hg_ref, xb_ref, proj_a, proj_b = rest[len(MixParams._fields):]
  i = pl.program_id(0)
  t_cur = lax.rem(jnp.maximum(i - 1, 0), chunks_per_seq)

  @pl.when(i == 0)
  def _():
    proj_b[...] = jnp.zeros(proj_b.shape, F32)

  @pl.when(t_cur == 0)
  def _():
    xb_ref[0:SUBLANES, :] = jnp.zeros((SUBLANES, CONV_DIM), F32)
    ssm_ref[...] = jnp.zeros(ssm_ref.shape, F32)
    hg_ref[...] = jnp.zeros(hg_ref.shape, F32)

  def step(cur_ref, next_ref):
    h = _rmsnorm(x_ref[0], nmix_ref[...]).astype(BF16)
    pending = _proj_blocks()
    n_blocks = len(pending)
    credit = [0.0]

    def project_blocks(n):
      credit[0] += n * n_blocks / MIX_PHASE_WEIGHT
      while pending and credit[0] >= 1.0:
        credit[0] -= 1.0
        lo, hi = pending.pop(0)
        next_ref[:, lo:hi] = _dot(h, wall_ref[:, lo:hi])

    mixed = _drive(_mix_chunk(cur_ref, c=CHUNK, valid=CHUNK, p=_load_mix_params(prm_refs), xb_ref=xb_ref,
                              ssm_ref=ssm_ref.at[0], hg_ref=hg_ref.at[0]), project_blocks)
    project_blocks(MIX_PHASE_WEIGHT)
    mixed_ref[0] = mixed.astype(BF16)
    tail = xb_ref[CHUNK:CHUNK + SUBLANES, :]
    xb_ref[0:SUBLANES, :] = tail

  parity = lax.rem(i, 2)
  pl.when(parity == 0)(functools.partial(step, proj_b, proj_a))
  pl.when(parity == 1)(functools.partial(step, proj_a, proj_b))

  @pl.when((t_cur == chunks_per_seq - 1) & (i > 0))
  def _():
    conv_ref[0] = xb_ref[SUBLANES + CHUNK - (CONV_W - 1):SUBLANES + CHUNK, :]


def _proj_sample_kernel(x_ref, nmix_ref, wall_ref, o_ref):
  h = _rmsnorm(x_ref[...], nmix_ref[...]).astype(BF16)
  for lo, hi in _proj_blocks():
    o_ref[:, lo:hi] = _dot(h, wall_ref[:, lo:hi])


def _rec_sample_kernel(proj_ref, conv_in_ref, ssm_in_ref, hg_in_ref, *rest, valid):
  prm_refs = rest[:len(MixParams._fields)]
  mixed_ref, conv_ref, ssm_ref, hg_ref, xb_ref, padded_ref = rest[len(MixParams._fields):]
  n_seq = conv_in_ref.shape[0]
  prm = _load_mix_params(prm_refs)
  ssm_ref[...] = ssm_in_ref[...]
  hg_ref[...] = hg_in_ref[...]
  gens = []
  for s in range(n_seq):
    xb_ref[s, 0:SUBLANES, :] = jnp.zeros((SUBLANES, CONV_DIM), F32)
    xb_ref[s, SUBLANES - (CONV_W - 1):SUBLANES, :] = conv_in_ref[s]
    padded_ref[s, 0:valid, :] = proj_ref[s * valid:(s + 1) * valid, :]
    padded_ref[s, valid:SAMPLE_ROWS, :] = jnp.zeros((SAMPLE_ROWS - valid, PROJ_W), F32)
    gens.append(_mix_chunk(padded_ref.at[s], c=SAMPLE_ROWS, valid=valid, p=prm, xb_ref=xb_ref.at[s],
                           ssm_ref=ssm_ref.at[s], hg_ref=hg_ref.at[s]))
  for s, mixed in enumerate(_drive_round_robin(gens)):
    mixed_ref[s] = mixed.astype(BF16)
    conv_ref[s] = xb_ref[s, SUBLANES + valid - (CONV_W - 1):SUBLANES + valid, :]


def _tail_kernel(x1s_ref, x1p_ref, mixs_ref, mixp_ref, ps_ref, pp_ref, wout_ref, nf2_ref, wup_f32_ref, wdn_f32_ref,
                 nple_ref, wgate_ref, wproj_ref, ppost_ref, nfin_ref, os_ref, op_ref, wup_ref, wdn_ref):
  i = pl.program_id(0)

  @pl.when(i < W_STEPS)
  def _():
    _stage_ffn_weights(i, wup_f32_ref, wdn_f32_ref, wup_ref, wdn_ref)

  @pl.when(i >= W_STEPS)
  def _():
    pick = lambda s_ref, p_ref: _first_tile_select(s_ref, p_ref, W_STEPS)
    x2 = pick(x1s_ref, x1p_ref) + _dot(pick(mixs_ref, mixp_ref), wout_ref[...])
    h = _rmsnorm(x2, nf2_ref[...]).astype(BF16)
    e = _rmsnorm(_dot(pick(ps_ref, pp_ref).astype(BF16), wproj_ref[...]), ppost_ref[...])
    x3 = x2 + 0.5 * _swiglu(h, wup_ref, wdn_ref)
    gate = _sigmoid(_dot(_rmsnorm(x3, nple_ref[...]).astype(BF16), wgate_ref[...]))
    x4 = x3 + gate * e
    op_ref[...] = _rmsnorm(x4, nfin_ref[...])

  @pl.when(i == W_STEPS)
  def _():
    os_ref[...] = op_ref[...]


def _resident(shape):
  nd = len(shape)
  return pl.BlockSpec(shape, lambda *_: (0,) * nd, pipeline_mode=pl.Buffered(1))


def _params(semantics):
  return pltpu.CompilerParams(dimension_semantics=semantics, vmem_limit_bytes=VMEM_LIMIT_BYTES)


def _sample_then_prompt_specs(width, n_prompt_tiles, first_step):
  sample = pl.BlockSpec((FFN_TILE, width), lambda i: (0, 0))
  prompt = pl.BlockSpec((FFN_TILE, width), lambda i: (jnp.clip(i - first_step - 1, 0, n_prompt_tiles - 1), 0))
  return sample, prompt


def _ffn_weight_specs():
  staged = lambda i: jnp.minimum(i, W_STEPS - 1)
  return [pl.BlockSpec((D_MODEL, UP_BLOCK), lambda i: (0, staged(i))),
          pl.BlockSpec((FF_CHUNK, D_MODEL), lambda i: (staged(i), 0))]


FFN_WEIGHT_SCRATCH = [pltpu.VMEM((W_STEPS, D_MODEL, UP_BLOCK), BF16), pltpu.VMEM((W_STEPS, FF_CHUNK, D_MODEL), BF16)]


def _ffn1(xs, xp, nw, wup, wdn):
  assert xs.shape[0] == FFN_TILE and xp.shape[0] % FFN_TILE == 0
  n_prompt_tiles = xp.shape[0] // FFN_TILE
  specs = _sample_then_prompt_specs(D_MODEL, n_prompt_tiles, first_step=W_STEPS)
  return pl.pallas_call(
      _ffn1_kernel,
      grid=(W_STEPS + 1 + n_prompt_tiles,),
      in_specs=[*specs, _resident(nw.shape), *_ffn_weight_specs()],
      out_specs=specs,
      out_shape=(jax.ShapeDtypeStruct(xs.shape, F32), jax.ShapeDtypeStruct(xp.shape, F32)),
      scratch_shapes=FFN_WEIGHT_SCRATCH,
      compiler_params=_params(("arbitrary",)),
      name="ffn1",
  )(xs, xp, nw, wup, wdn)


def _split_w_in(wt):
  rows = LANES
  assert wt.shape == (SSD_DINNER + CONV_DIM + SSD_HEADS + 4 * HG_WIDTH, D_MODEL)
  return pl.pallas_call(
      _split_w_in_kernel,
      grid=(D_MODEL // rows,),
      in_specs=[pl.BlockSpec((wt.shape[0], rows), lambda i: (0, i))],
      out_specs=pl.BlockSpec((rows, PROJ_W), lambda i: (i, 0)),
      out_shape=jax.ShapeDtypeStruct((D_MODEL, PROJ_W), BF16),
      compiler_params=_params(("arbitrary",)),
      name="split_w_in",
  )(wt)


def _mix_prompt(x1, nmix, wall, prm):
  bsz, seq, _ = x1.shape
  nt = seq // CHUNK
  n_chunks = bsz * nt
  out_shape = (
      jax.ShapeDtypeStruct((bsz, seq, D_MIX), BF16),
      jax.ShapeDtypeStruct((bsz, CONV_W - 1, CONV_DIM), F32),
      jax.ShapeDtypeStruct((bsz, SSD_HEADS, SSD_HEADDIM, SSD_STATE), F32),
      jax.ShapeDtypeStruct((bsz, HG_HEADS, HG_DK, HG_DV), F32),
  )
  proj_chunk = lambda i: jnp.minimum(i, n_chunks - 1)
  mix_chunk = lambda i: jnp.maximum(i - 1, 0)
  out_specs = (
      pl.BlockSpec((1, CHUNK, D_MIX), lambda i: (mix_chunk(i) // nt, mix_chunk(i) % nt, 0)),
      pl.BlockSpec((1, CONV_W - 1, CONV_DIM), lambda i: (mix_chunk(i) // nt, 0, 0)),
      pl.BlockSpec((1, SSD_HEADS, SSD_HEADDIM, SSD_STATE), lambda i: (mix_chunk(i) // nt, 0, 0, 0)),
      pl.BlockSpec((1, HG_HEADS, HG_DK, HG_DV), lambda i: (mix_chunk(i) // nt, 0, 0, 0)),
  )
  in_specs = [pl.BlockSpec((1, CHUNK, D_MODEL), lambda i: (proj_chunk(i) // nt, proj_chunk(i) % nt, 0)),
              _resident(nmix.shape)] + [_resident(a.shape) for a in (wall, *prm)]
  return pl.pallas_call(
      functools.partial(_mix_prompt_kernel, chunks_per_seq=nt),
      grid=(n_chunks + 1,),
      in_specs=in_specs,
      out_specs=out_specs,
      out_shape=out_shape,
      scratch_shapes=[pltpu.VMEM((SUBLANES + CHUNK, CONV_DIM), F32), pltpu.VMEM((CHUNK, PROJ_W), F32),
                      pltpu.VMEM((CHUNK, PROJ_W), F32)],
      compiler_params=_params(("arbitrary",)),
      name="mix_prompt",
  )(x1, nmix, wall, *prm)


def _proj_sample(x1p, nmix, wall):
  n = x1p.shape[0]
  tile = 256
  return pl.pallas_call(
      _proj_sample_kernel,
      grid=(n // tile,),
      in_specs=[pl.BlockSpec((tile, D_MODEL), lambda i: (i, 0)), _resident(nmix.shape), _resident(wall.shape)],
      out_specs=pl.BlockSpec((tile, PROJ_W), lambda i: (i, 0)),
      out_shape=jax.ShapeDtypeStruct((n, PROJ_W), F32),
      compiler_params=_params(("arbitrary",)),
      name="proj_sample",
  )(x1p, nmix, wall)


def _rec_sample(proj, conv0, ssm0, hg0, prm, valid):
  bsz = conv0.shape[0]
  assert proj.shape[0] == bsz * valid
  out_shape = (
      jax.ShapeDtypeStruct((bsz, SAMPLE_ROWS, D_MIX), BF16),
      jax.ShapeDtypeStruct((bsz, CONV_W - 1, CONV_DIM), F32),
      jax.ShapeDtypeStruct((bsz, SSD_HEADS, SSD_HEADDIM, SSD_STATE), F32),
      jax.ShapeDtypeStruct((bsz, HG_HEADS, HG_DK, HG_DV), F32),
  )
  g = SAMPLE_SEQS_PER_STEP
  state_specs = [
      pl.BlockSpec((g, CONV_W - 1, CONV_DIM), lambda b: (b, 0, 0)),
      pl.BlockSpec((g, SSD_HEADS, SSD_HEADDIM, SSD_STATE), lambda b: (b, 0, 0, 0)),
      pl.BlockSpec((g, HG_HEADS, HG_DK, HG_DV), lambda b: (b, 0, 0, 0)),
  ]
  in_specs = [pl.BlockSpec((g * valid, PROJ_W), lambda b: (b, 0))] + state_specs + [_resident(a.shape) for a in prm]
  out_specs = tuple([pl.BlockSpec((g, SAMPLE_ROWS, D_MIX), lambda b: (b, 0, 0))] + state_specs)
  return pl.pallas_call(
      functools.partial(_rec_sample_kernel, valid=valid),
      grid=(bsz // g,),
      in_specs=in_specs,
      out_specs=out_specs,
      out_shape=out_shape,
      scratch_shapes=[pltpu.VMEM((g, 2 * SUBLANES, CONV_DIM), F32), pltpu.VMEM((g, SAMPLE_ROWS, PROJ_W), F32)],
      compiler_params=_params(("arbitrary",)),
      name="rec_sample",
  )(proj, conv0, ssm0, hg0, *prm)


def _tail(x1s, x1p, mixs, mixp, ps, pp, wout, nf2, wup, wdn, nple, wgate, wproj, ppost, nfin):
  assert x1s.shape[0] == FFN_TILE and x1p.shape[0] % FFN_TILE == 0
  weights = (wout, nf2, wup, wdn, nple, wgate, wproj, ppost, nfin)
  n_prompt_tiles = x1p.shape[0] // FFN_TILE

  def in_specs(width):
    sample, prompt = _sample_then_prompt_specs(width, n_prompt_tiles, first_step=W_STEPS)
    return pl.BlockSpec(sample.block_shape, sample.index_map, pipeline_mode=pl.Buffered(1)), prompt

  weight_specs = [_resident(wout.shape), _resident(nf2.shape), *_ffn_weight_specs()] + [
      _resident(w.shape) for w in (nple, wgate, wproj, ppost, nfin)]
  return pl.pallas_call(
      _tail_kernel,
      grid=(W_STEPS + 1 + n_prompt_tiles,),
      in_specs=[*in_specs(D_MODEL), *in_specs(D_MIX), *in_specs(PLE_DIM)] + weight_specs,
      out_specs=_sample_then_prompt_specs(D_MODEL, n_prompt_tiles, first_step=W_STEPS),
      out_shape=(jax.ShapeDtypeStruct(x1s.shape, F32), jax.ShapeDtypeStruct(x1p.shape, F32)),
      scratch_shapes=FFN_WEIGHT_SCRATCH,
      compiler_params=_params(("arbitrary",)),
      name="tail",
  )(x1s, x1p, mixs, mixp, ps, pp, *weights)


def kernel(x_prompt, x_sample, state_conv, state_ssm, state_hgrn, p_prompt, p_sample, norm_ffn1, w_ffn1_up, w_ffn1_down, norm_mix, w_in, conv_w, conv_b, dt_bias, a_log, d_skip, ssd_norm, hg_lb_logits, hg_norm, w_out, norm_ffn2, w_ffn2_up, w_ffn2_down, norm_ple, w_ple_gate, w_ple_proj, ple_post_norm, norm_final):
  bp, seq, _ = x_prompt.shape
  bs, dec_seq, _ = x_sample.shape
  row = lambda v: v.reshape(1, -1).astype(F32)
  per_head = lambda v: jnp.repeat(v.astype(F32), SSD_HEADDIM).reshape(1, SSD_DINNER)
  head_lanes = lambda v: jnp.pad(v.astype(F32), (0, LANES - SSD_HEADS)).reshape(1, LANES)

  w1u, w1d = w_ffn1_up[0].astype(F32), w_ffn1_down[0].astype(F32)
  w2u, w2d = w_ffn2_up[0].astype(F32), w_ffn2_down[0].astype(F32)
  wall = _split_w_in(jnp.swapaxes(w_in[0], 0, 1))
  prm = MixParams(
      conv_w=conv_w[0].astype(F32), conv_b=row(conv_b[0]), dtb_c=head_lanes(dt_bias[0]),
      alog_c=head_lanes(a_log[0]), dskip_e=per_head(d_skip[0]), ssd_norm=row(ssd_norm[0]),
      lb_logits=hg_lb_logits.astype(F32), hg_norm=row(hg_norm[0]))
  nmix = row(norm_mix[0])
  tail_w = (w_out[0].astype(BF16), row(norm_ffn2[0]), w2u, w2d, row(norm_ple[0]), w_ple_gate[0].astype(BF16),
            w_ple_proj[0].astype(BF16), row(ple_post_norm[0]), row(norm_final))

  xp = x_prompt.reshape(bp * seq, D_MODEL)
  xs = x_sample.reshape(bs * dec_seq, D_MODEL)
  x1s, x1p = _ffn1(xs, xp, row(norm_ffn1[0]), w1u, w1d)

  mixed_p, conv_p, ssm_p, hg_p = _mix_prompt(x1p.reshape(bp, seq, D_MODEL), nmix, wall, prm)
  proj_s = _proj_sample(x1s, nmix, wall)
  mixed_s, conv_s, ssm_s, hg_s = _rec_sample(proj_s, state_conv[0], state_ssm[0], state_hgrn[0], prm, dec_seq)
  mixed_s = mixed_s[:, :dec_seq].reshape(bs * dec_seq, D_MIX)

  y_sample, y_prompt = _tail(x1s, x1p, mixed_s, mixed_p.reshape(bp * seq, D_MIX),
                             p_sample[0].reshape(bs * dec_seq, PLE_DIM), p_prompt[0].reshape(bp * seq, PLE_DIM), *tail_w)

  return (y_prompt.reshape(bp, seq, D_MODEL), y_sample.reshape(bs, dec_seq, D_MODEL), conv_p[None], ssm_p[None],
          hg_p[None], conv_s[None], ssm_s[None], hg_s[None])
```
